```python
import functools
import jax, jax.numpy as jnp
from jax import lax
import numpy as np

D_MODEL = 1024
BATCH = 4
SEQ = 4096
DEPTH = 1
DEC_BATCH = 128
DEC_SEQ = 1
PAST_LEN = 8192
PAGE_SIZE = 128

RW_HEADS = 8
RW_HEAD_DIM = 64
RW_WIDTH = RW_HEADS * RW_HEAD_DIM
DECAY_LORA = 64
ICLR_LORA = 64
GATE_LORA = 128
RW_COLS = 3 * RW_WIDTH + DECAY_LORA + ICLR_LORA + GATE_LORA
GN_EPS = 64e-5

MLA_HEADS = 8
QK_NOPE = 64
QK_ROPE = 32
V_HEAD = 64
Q_LORA = 384
KV_LORA = 256
MLA_COLS = Q_LORA + KV_LORA + QK_ROPE
ROPE_THETA = 10000.0
Q_BLOCK = 128
ATTN_SCALE = (QK_NOPE + QK_ROPE) ** -0.5

IN_COLS = RW_COLS + MLA_COLS + 2 * D_MODEL

N_EXPERTS = 64
TOP_K = 8
N_GROUPS = 8
TOPK_GROUPS = 4
EXPERT_FF = 256
SHARED_FF = 256
ROUTED_SCALE = 2.5
MOE_BLOCK = 128

DN_ALPHA = (2 * DEPTH) ** 0.25
DN_BETA = (8 * DEPTH) ** -0.25
LN_EPS = 1e-5
RMS_EPS = 1e-6

kernel_name = 'hybrid_rwkv7_mla_moe_decode_step'


def _layernorm(x, g, b):
    xf = x.astype(jnp.float32)
    mu = xf.mean(-1, keepdims=True)
    var = jnp.square(xf - mu).mean(-1, keepdims=True)
    return ((xf - mu) * lax.rsqrt(var + LN_EPS) * g + b).astype(x.dtype)


def _rmsnorm(x, g):
    xf = x.astype(jnp.float32)
    return (xf * lax.rsqrt(jnp.mean(xf * xf, -1, keepdims=True) + RMS_EPS) * g).astype(x.dtype)


def _rope(x, pos):
    half = QK_ROPE // 2
    inv = ROPE_THETA ** (-jnp.arange(half, dtype=jnp.float32) / half)
    ang = pos.astype(jnp.float32)[:, None] * inv[None, :]
    cos = jnp.cos(ang)[None, :, None, :]
    sin = jnp.sin(ang)[None, :, None, :]
    xf = x.astype(jnp.float32)
    x1, x2 = xf[..., :half], xf[..., half:]
    return jnp.concatenate([x1 * cos - x2 * sin, x1 * sin + x2 * cos], -1).astype(x.dtype)


def _swiglu(x, wg, wu, wd):
    return (jax.nn.silu(x @ wg) * (x @ wu)) @ wd


def _rwkv7(pm, wkv0, P):
    bsz, t, _ = pm.shape
    o1, o2, o3 = RW_WIDTH, 2 * RW_WIDTH, 3 * RW_WIDTH
    o4, o5 = o3 + DECAY_LORA, o3 + DECAY_LORA + ICLR_LORA
    r, k, v, xw, xa, xg = jnp.split(pm, [o1, o2, o3, o4, o5], axis=-1)
    w_log = -jax.nn.softplus(-(P['w_decay0'] + jnp.tanh(xw) @ P['w_decay_up']).astype(jnp.float32)) - 0.5
    decay = jnp.exp(-jnp.exp(w_log))
    a = jax.nn.sigmoid((P['a0'] + xa @ P['w_iclr_up']).astype(jnp.float32))
    g = jax.nn.sigmoid(xg) @ P['w_gate_up']
    hd = lambda z: z.astype(jnp.float32).reshape(bsz, t, RW_HEADS, RW_HEAD_DIM)
    hp = lambda z: z.astype(jnp.float32).reshape(RW_HEADS, RW_HEAD_DIM)
    kk = hd(k * P['k_k'])
    kk = kk * lax.rsqrt(jnp.sum(kk * kk, -1, keepdims=True) + 1e-12)
    k_mod = hd(k) * (1.0 + (hd(a) - 1.0) * hp(P['k_a']))
    r_h, v_h, a_h, w_h = hd(r), hd(v), hd(a), hd(decay)

    def step(S, inp):
        r_t, k_t, v_t, kk_t, a_t, w_t = inp
        sa = jnp.einsum('bhvk,bhk->bhv', S, -kk_t)
        S = (S * w_t[:, :, None, :] + sa[..., None] * (kk_t * a_t)[:, :, None, :]
             + v_t[..., None] * k_t[:, :, None, :])
        return S, jnp.einsum('bhvk,bhk->bhv', S, r_t)

    xs = tuple(jnp.moveaxis(z, 1, 0) for z in (r_h, k_mod, v_h, kk, a_h, w_h))
    S_T, o = lax.scan(step, wkv0.astype(jnp.float32), xs)
    o = jnp.moveaxis(o, 0, 1)
    mu = o.mean(-1, keepdims=True)
    var = jnp.square(o - mu).mean(-1, keepdims=True)
    o = ((o - mu) * lax.rsqrt(var + GN_EPS)).reshape(bsz, t, RW_WIDTH) * P['lnx_g'] + P['lnx_b']
    bonus = jnp.sum(r_h * k_mod * hp(P['r_k']), -1, keepdims=True) * v_h
    o = (o + bonus.reshape(bsz, t, RW_WIDTH)) * g
    return o.astype(pm.dtype) @ P['w_branch_a'], S_T


def _mla_prompt(q_nope, q_rope, ckv, k_rope, w_uk, w_uv):
    bsz, t = q_nope.shape[0], q_nope.shape[1]
    k_nope = jnp.einsum('btc,chd->bthd', ckv, w_uk.reshape(KV_LORA, MLA_HEADS, QK_NOPE))
    v = jnp.einsum('btc,chd->bthd', ckv, w_uv.reshape(KV_LORA, MLA_HEADS, V_HEAD))
    qb = min(Q_BLOCK, t)
    nb = t // qb
    qn = q_nope.reshape(bsz, nb, qb, MLA_HEADS, QK_NOPE).swapaxes(0, 1)
    qr = q_rope.reshape(bsz, nb, qb, MLA_HEADS, QK_ROPE).swapaxes(0, 1)
    kpos = jnp.arange(t)

    def block(args):
        qn_b, qr_b, i = args
        s = (jnp.einsum('bqhd,bkhd->bhqk', qn_b, k_nope)
             + jnp.einsum('bqhr,bkr->bhqk', qr_b, k_rope)).astype(jnp.float32) * ATTN_SCALE
        qpos = i * qb + jnp.arange(qb)
        s = jnp.where(kpos[None, :] <= qpos[:, None], s, -jnp.inf)
        p = jax.nn.softmax(s, axis=-1).astype(v.dtype)
        return jnp.einsum('bhqk,bkhd->bqhd', p, v)

    o = lax.map(block, (qn, qr, jnp.arange(nb)))
    return o.swapaxes(0, 1).reshape(bsz, t, MLA_HEADS, V_HEAD)


def _mla_sample(past_ckv, past_kr, q_nope, q_rope, ckv, k_rope, w_uk, w_uv):
    past = past_ckv.shape[1]
    s_len = q_nope.shape[1]
    w_uk3 = w_uk.reshape(KV_LORA, MLA_HEADS, QK_NOPE)
    q_lat = jnp.einsum('bshd,chd->bshc', q_nope, w_uk3)
    s_past = (jnp.einsum('bshc,btc->bhst', q_lat, past_ckv)
              + jnp.einsum('bshr,btr->bhst', q_rope, past_kr))
    s_new = (jnp.einsum('bshc,buc->bhsu', q_lat, ckv)
             + jnp.einsum('bshr,bur->bhsu', q_rope, k_rope))
    causal = jnp.arange(s_len)[None, :] <= jnp.arange(s_len)[:, None]
    s_new = jnp.where(causal, s_new.astype(jnp.float32), -jnp.inf)
    s = jnp.concatenate([s_past.astype(jnp.float32), s_new], -1) * ATTN_SCALE
    p = jax.nn.softmax(s, axis=-1).astype(ckv.dtype)
    o_lat = (jnp.einsum('bhst,btc->bshc', p[..., :past], past_ckv)
             + jnp.einsum('bhsu,buc->bshc', p[..., past:], ckv))
    return jnp.einsum('bshc,chd->bshd', o_lat, w_uv.reshape(KV_LORA, MLA_HEADS, V_HEAD))


def _routed_experts(h, e_idx, wts, w_g, w_u, w_d):
    n = h.shape[0]
    a = n * TOP_K
    flat_e = e_idx.reshape(a)
    order = jnp.argsort(flat_e)
    e_sorted = flat_e[order]
    tok_sorted = (order // TOP_K).astype(jnp.int32)
    counts = jnp.zeros((N_EXPERTS,), jnp.int32).at[flat_e].add(1)
    padded = (counts + MOE_BLOCK - 1) // MOE_BLOCK * MOE_BLOCK
    pad_end = jnp.cumsum(padded)
    pad_start = pad_end - padded
    start = jnp.cumsum(counts) - counts
    dest = pad_start[e_sorted] + jnp.arange(a, dtype=jnp.int32) - start[e_sorted]
    n_blocks = -(-a // MOE_BLOCK) + N_EXPERTS
    slot_tok = jnp.full((n_blocks * MOE_BLOCK,), n, jnp.int32).at[dest].set(tok_sorted)
    block_e = jnp.minimum(jnp.searchsorted(pad_end, jnp.arange(n_blocks, dtype=jnp.int32) * MOE_BLOCK,
                                           side='right'), N_EXPERTS - 1)
    h_ext = jnp.concatenate([h, jnp.zeros((1, h.shape[1]), h.dtype)], 0)
    xb = h_ext[slot_tok].reshape(n_blocks, MOE_BLOCK, h.shape[1])

    def one_block(args):
        xblk, e = args
        return _swiglu(xblk, w_g[e], w_u[e], w_d[e])

    yb = lax.map(one_block, (xb, block_e)).reshape(n_blocks * MOE_BLOCK, h.shape[1])
    y_assign = (yb[dest] * wts.reshape(a)[order][:, None]).astype(h.dtype)
    return jnp.zeros_like(h).at[tok_sorted].add(y_assign)


def _moe(h, P):
    n = h.shape[0]
    s = jax.nn.sigmoid((h @ P['w_router']).astype(jnp.float32))
    sb = s + P['b_router'].astype(jnp.float32)
    g_score = lax.top_k(sb.reshape(n, N_GROUPS, N_EXPERTS // N_GROUPS), 2)[0].sum(-1)
    _, g_idx = lax.top_k(g_score, TOPK_GROUPS)
    g_mask = jax.nn.one_hot(g_idx, N_GROUPS).sum(1) > 0
    e_mask = jnp.repeat(g_mask, N_EXPERTS // N_GROUPS, axis=1)
    _, e_idx = lax.top_k(jnp.where(e_mask, sb, -jnp.inf), TOP_K)
    wts = jnp.take_along_axis(s, e_idx, axis=1)
    wts = wts / jnp.sum(wts, -1, keepdims=True) * ROUTED_SCALE
    routed = _routed_experts(h, e_idx, wts, P['w_exp_gate'], P['w_exp_up'], P['w_exp_down'])
    return routed + _swiglu(h, P['w_sh_gate'], P['w_sh_up'], P['w_sh_down'])


def _layer(x, c, shift0, wkv0, pos, attend, P):
    bsz, t, _ = x.shape
    mod = jax.nn.silu(c) @ P['w_ada'] + P['b_ada']
    sh1, sc1, gt1, sh2, sc2, gt2 = jnp.split(mod[:, None, :], 6, axis=-1)
    h = x * (1 + sc1) + sh1
    proj = h @ P['w_in']
    o1 = RW_COLS
    o2 = o1 + Q_LORA
    o3 = o2 + KV_LORA
    o4 = o3 + QK_ROPE
    o5 = o4 + D_MODEL
    p_rw, q_low, kv_low, kr_low, gpre_a, gpre_b = jnp.split(proj, [o1, o2, o3, o4, o5], axis=-1)
    p_prev = jnp.concatenate([shift0[:, None, :].astype(p_rw.dtype), p_rw[:, :-1]], axis=1)
    p_mix = p_rw + (p_prev - p_rw) * P['mu_shift']
    y_a, wkv_new = _rwkv7(p_mix, wkv0, P)
    cq = _rmsnorm(q_low, P['g_qnorm'])
    q = (cq @ P['w_uq']).reshape(bsz, t, MLA_HEADS, QK_NOPE + QK_ROPE)
    q_nope = q[..., :QK_NOPE]
    q_rope = _rope(q[..., QK_NOPE:], pos)
    ckv = _rmsnorm(kv_low, P['g_kvnorm'])
    k_rope = _rope(kr_low[:, :, None, :], pos)[:, :, 0, :]
    o_b = attend(q_nope, q_rope, ckv, k_rope, P['w_uk'], P['w_uv'])
    y_b = o_b.reshape(bsz, t, MLA_HEADS * V_HEAD) @ P['w_branch_b']
    merged = jax.nn.sigmoid(gpre_a) * y_a + jax.nn.sigmoid(gpre_b) * y_b
    x = _layernorm(DN_ALPHA * x + gt1 * (merged @ P['w_out']), P['ln1_g'], P['ln1_b'])
    h2 = x * (1 + sc2) + sh2
    ffn = _moe(h2.reshape(bsz * t, D_MODEL), P).reshape(x.shape)
    x = _layernorm(DN_ALPHA * x + gt2 * ffn, P['ln2_g'], P['ln2_b'])
    return x, ckv, k_rope, wkv_new, p_rw[:, -1]


def setup_inputs(seed: int = 0) -> dict:
    key = jax.random.key(seed)
    ks = iter(jax.random.split(key, 64))
    f32 = jnp.float32
    L, D = DEPTH, D_MODEL

    def nrm(shape, scale):
        return jax.random.normal(next(ks), shape, f32) * scale

    def gain(shape):
        return 1.0 + nrm(shape, 0.02)

    n_pages = PAST_LEN // PAGE_SIZE
    n_used = DEC_BATCH * n_pages
    n_phys = n_used + max(1, n_used // 4)
    page_table = jax.random.permutation(next(ks), n_phys)[:n_used].reshape(DEC_BATCH, n_pages).astype(jnp.int32)
    col_scale = jnp.ones((IN_COLS,), f32).at[2 * RW_WIDTH:3 * RW_WIDTH].set(DN_BETA)
    x_prompt = nrm((BATCH, SEQ, D), 1.0)
    x_sample = nrm((DEC_BATCH, DEC_SEQ, D), 1.0)
    c_prompt = nrm((BATCH, D), 1.0)
    c_sample = nrm((DEC_BATCH, D), 1.0)
    cache_ckv = nrm((L, n_phys, PAGE_SIZE, KV_LORA), 1.0)
    cache_krope = nrm((L, n_phys, PAGE_SIZE, QK_ROPE), 1.0)
    state_wkv = nrm((L, DEC_BATCH, RW_HEADS, RW_HEAD_DIM, RW_HEAD_DIM), 0.3)
    state_shift = nrm((L, DEC_BATCH, RW_COLS), 1.0)
    return {
        'x_prompt': x_prompt,
        'x_sample': x_sample,
        'c_prompt': c_prompt,
        'c_sample': c_sample,
        'cache_ckv': cache_ckv,
        'cache_krope': cache_krope,
        'state_wkv': state_wkv,
        'state_shift': state_shift,
        'page_table': page_table,
        'w_ada': nrm((L, D, 6 * D), D ** -0.5),
        'b_ada': nrm((L, 6 * D), 0.02),
        'w_in': nrm((L, D, IN_COLS), D ** -0.5) * col_scale,
        'mu_shift': jax.random.uniform(next(ks), (L, RW_COLS), f32),
        'w_decay0': -6.0 + 5.0 * jax.random.uniform(next(ks), (L, RW_WIDTH), f32),
        'w_decay_up': nrm((L, DECAY_LORA, RW_WIDTH), 0.1),
        'a0': nrm((L, RW_WIDTH), 0.1),
        'w_iclr_up': nrm((L, ICLR_LORA, RW_WIDTH), 0.5 * ICLR_LORA ** -0.5),
        'w_gate_up': nrm((L, GATE_LORA, RW_WIDTH), GATE_LORA ** -0.5),
        'k_k': 0.85 + nrm((L, RW_WIDTH), 0.02),
        'k_a': gain((L, RW_WIDTH)),
        'r_k': nrm((L, RW_WIDTH), 0.1),
        'lnx_g': gain((L, RW_WIDTH)),
        'lnx_b': nrm((L, RW_WIDTH), 0.02),
        'w_branch_a': nrm((L, RW_WIDTH, D), RW_WIDTH ** -0.5 * DN_BETA),
        'g_qnorm': gain((L, Q_LORA)),
        'w_uq': nrm((L, Q_LORA, MLA_HEADS * (QK_NOPE + QK_ROPE)), Q_LORA ** -0.5),
        'g_kvnorm': gain((L, KV_LORA)),
        'w_uk': nrm((L, KV_LORA, MLA_HEADS * QK_NOPE), KV_LORA ** -0.5),
        'w_uv': nrm((L, KV_LORA, MLA_HEADS * V_HEAD), KV_LORA ** -0.5 * DN_BETA),
        'w_branch_b': nrm((L, MLA_HEADS * V_HEAD, D), (MLA_HEADS * V_HEAD) ** -0.5 * DN_BETA),
        'w_out': nrm((L, D, D), D ** -0.5 * DN_BETA),
        'ln1_g': gain((L, D)),
        'ln1_b': nrm((L, D), 0.02),
        'w_router': nrm((L, D, N_EXPERTS), D ** -0.5),
        'b_router': nrm((L, N_EXPERTS), 0.01),
        'w_exp_gate': nrm((L, N_EXPERTS, D, EXPERT_FF), D ** -0.5 * DN_BETA),
        'w_exp_up': nrm((L, N_EXPERTS, D, EXPERT_FF), D ** -0.5 * DN_BETA),
        'w_exp_down': nrm((L, N_EXPERTS, EXPERT_FF, D), EXPERT_FF ** -0.5 * DN_BETA),
        'w_sh_gate': nrm((L, D, SHARED_FF), D ** -0.5 * DN_BETA),
        'w_sh_up': nrm((L, D, SHARED_FF), D ** -0.5 * DN_BETA),
        'w_sh_down': nrm((L, SHARED_FF, D), SHARED_FF ** -0.5 * DN_BETA),
        'ln2_g': gain((L, D)),
        'ln2_b': nrm((L, D), 0.02),
    }


def reference(x_prompt, x_sample, c_prompt, c_sample, cache_ckv, cache_krope, state_wkv, state_shift,
              page_table, w_ada, b_ada, w_in, mu_shift, w_decay0, w_decay_up, a0, w_iclr_up, w_gate_up,
              k_k, k_a, r_k, lnx_g, lnx_b, w_branch_a, g_qnorm, w_uq, g_kvnorm, w_uk, w_uv, w_branch_b,
              w_out, ln1_g, ln1_b, w_router, b_router, w_exp_gate, w_exp_up, w_exp_down,
              w_sh_gate, w_sh_up, w_sh_down, ln2_g, ln2_b):
    P = dict(w_ada=w_ada, b_ada=b_ada, w_in=w_in, mu_shift=mu_shift, w_decay0=w_decay0,
             w_decay_up=w_decay_up, a0=a0, w_iclr_up=w_iclr_up, w_gate_up=w_gate_up, k_k=k_k, k_a=k_a,
             r_k=r_k, lnx_g=lnx_g, lnx_b=lnx_b, w_branch_a=w_branch_a, g_qnorm=g_qnorm, w_uq=w_uq,
             g_kvnorm=g_kvnorm, w_uk=w_uk, w_uv=w_uv, w_branch_b=w_branch_b, w_out=w_out,
             ln1_g=ln1_g, ln1_b=ln1_b, w_router=w_router, b_router=b_router, w_exp_gate=w_exp_gate,
             w_exp_up=w_exp_up, w_exp_down=w_exp_down, w_sh_gate=w_sh_gate, w_sh_up=w_sh_up,
             w_sh_down=w_sh_down, ln2_g=ln2_g, ln2_b=ln2_b)
    bp, t_p = x_prompt.shape[0], x_prompt.shape[1]
    bd, s_new = x_sample.shape[0], x_sample.shape[1]
    past = page_table.shape[1] * PAGE_SIZE
    pos_p = jnp.arange(t_p, dtype=jnp.int32)
    pos_s = past + jnp.arange(s_new, dtype=jnp.int32)
    yp, ys = x_prompt, x_sample
    ckv_p, kr_p, wkv_p, sh_p = [], [], [], []
    ckv_s, kr_s, wkv_s, sh_s = [], [], [], []
    for l in range(DEPTH):
        Pl = {name: arr[l] for name, arr in P.items()}
        shift0 = jnp.zeros((bp, RW_COLS), x_prompt.dtype)
        wkv0 = jnp.zeros((bp, RW_HEADS, RW_HEAD_DIM, RW_HEAD_DIM), jnp.float32)
        yp, a_ckv, a_kr, a_wkv, a_sh = _layer(yp, c_prompt, shift0, wkv0, pos_p, _mla_prompt, Pl)
        past_ckv = cache_ckv[l][page_table].reshape(bd, past, KV_LORA)
        past_kr = cache_krope[l][page_table].reshape(bd, past, QK_ROPE)
        attend_s = functools.partial(_mla_sample, past_ckv, past_kr)
        ys, b_ckv, b_kr, b_wkv, b_sh = _layer(ys, c_sample, state_shift[l], state_wkv[l], pos_s, attend_s, Pl)
        ckv_p.append(a_ckv)
        kr_p.append(a_kr)
        wkv_p.append(a_wkv)
        sh_p.append(a_sh)
        ckv_s.append(b_ckv)
        kr_s.append(b_kr)
        wkv_s.append(b_wkv)
        sh_s.append(b_sh)
    return (yp, ys, jnp.stack(ckv_p), jnp.stack(kr_p), jnp.stack(wkv_p), jnp.stack(sh_p),
            jnp.stack(ckv_s), jnp.stack(kr_s), jnp.stack(wkv_s), jnp.stack(sh_s))
```

```python
import functools
import math

import numpy as np
import jax
import jax.numpy as jnp
from jax import lax
from jax.experimental import pallas as pl
from jax.experimental.pallas import tpu as pltpu

F32 = jnp.float32
BF16 = jnp.bfloat16

D_MODEL = 1024
PAGE_SIZE = 128
RW_HEADS = 8
RW_HEAD_DIM = 64
RW_WIDTH = RW_HEADS * RW_HEAD_DIM
DECAY_LORA = 64
ICLR_LORA = 64
GATE_LORA = 128
RW_COLS = 3 * RW_WIDTH + DECAY_LORA + ICLR_LORA + GATE_LORA
GN_EPS = 64e-5
MLA_HEADS = 8
QK_NOPE = 64
QK_ROPE = 32
V_HEAD = 64
Q_LORA = 384
KV_LORA = 256
ROPE_THETA = 10000.0
ATTN_SCALE = (QK_NOPE + QK_ROPE) ** -0.5
N_EXPERTS = 64
TOP_K = 8
N_GROUPS = 8
TOPK_GROUPS = 4
EXPERT_FF = 256
SHARED_FF = 256
ROUTED_SCALE = 2.5
DEPTH = 1
DN_ALPHA = (2 * DEPTH) ** 0.25
LN_EPS = 1e-5
RMS_EPS = 1e-6

LANES = 128
HEAD_SLOT = LANES
MLA_LOW_COLS = Q_LORA + KV_LORA + 2 * LANES
SCAN_CHUNK = 64
VMEM_LIMIT = 56 * 1024 * 1024


def _cparams(*sem):
    return pltpu.CompilerParams(dimension_semantics=sem, vmem_limit_bytes=VMEM_LIMIT)


def _mm(a, b):
    return jnp.dot(a.astype(BF16), b.astype(BF16), preferred_element_type=F32)


def _mm_t(a, b):
    return lax.dot_general(a.astype(BF16), b.astype(BF16), (((1,), (1,)), ((), ())), preferred_element_type=F32)


def _mm_tl(a, b):
    return lax.dot_general(a.astype(BF16), b.astype(BF16), (((0,), (0,)), ((), ())), preferred_element_type=F32)


def _split(x):
    hi = x.astype(BF16)
    lo = (x - hi.astype(F32)).astype(BF16)
    return hi, lo


def _three_pass(f, a, b):
    ah, al = _split(a)
    bh, bl = _split(b)
    return f(ah, bh) + (f(ah, bl) + f(al, bh))


def _mm3(a, b):
    return _three_pass(_mm, a, b)


def _mm3_t(a, b):
    return _three_pass(_mm_t, a, b)


def _mm3_tl(a, b):
    return _three_pass(_mm_tl, a, b)


_mm_tinv = _mm3


def _mm_exact_rhs(a, b_exact):
    ah, al = _split(a)
    return _mm(ah, b_exact) + _mm(al, b_exact)


def _silu(x):
    return x * jax.nn.sigmoid(x)


def _iota(shape, dim):
    return lax.broadcasted_iota(jnp.int32, shape, dim)


def _mod_kernel(c_ref, w_ref, b_ref, o_ref):
    o_ref[...] = _mm(_silu(c_ref[...]), w_ref[...]) + b_ref[...]


def _adaln_mod(c_all, w_ada, b_ada):
    n = c_all.shape[0]
    d = D_MODEL
    return pl.pallas_call(
        _mod_kernel,
        grid=(6,),
        in_specs=[pl.BlockSpec((n, d), lambda j: (0, 0)),
                  pl.BlockSpec((d, d), lambda j: (0, j)),
                  pl.BlockSpec((1, d), lambda j: (0, j))],
        out_specs=pl.BlockSpec((n, d), lambda j: (0, j)),
        out_shape=jax.ShapeDtypeStruct((n, 6 * d), F32),
        compiler_params=_cparams("arbitrary"),
        name="adaln_mod",
    )(c_all, w_ada, b_ada)


class _Mod:
    def __init__(self, arr, per_token, tile):
        self.arr = arr
        self.per_token = per_token
        self.tile = tile

    def spec(self, j):
        if self.per_token:
            return pl.BlockSpec((self.tile, D_MODEL), lambda b, i, *_: (i, j))
        return pl.BlockSpec((None, 1, D_MODEL), lambda b, i, *_: (b * 6 + j, 0, 0))


def _const_spec(arr):
    nd = arr.ndim
    return pl.BlockSpec(arr.shape, lambda *_: (0,) * nd)


def _inproj_kernel(x_ref, sh_ref, sc_ref, wrw_ref, wmla_ref, wg_ref, prw_ref, mla_ref, gs_ref):
    h = (x_ref[...] * (1 + sc_ref[...]) + sh_ref[...]).astype(BF16)
    prw_ref[...] = jnp.dot(h, wrw_ref[...], preferred_element_type=F32)
    mla_ref[...] = jnp.dot(h, wmla_ref[...], preferred_element_type=F32)
    gs_ref[...] = jax.nn.sigmoid(jnp.dot(h, wg_ref[...], preferred_element_type=F32))


def _inproj(x3, mod, wts, tile):
    bsz, t, d = x3.shape
    row = lambda w: pl.BlockSpec((None, tile, w), lambda b, i: (b, i, 0))
    return pl.pallas_call(
        _inproj_kernel,
        grid=(bsz, t // tile),
        in_specs=[row(d), mod.spec(0), mod.spec(1),
                  _const_spec(wts["w_rw"]), _const_spec(wts["w_mla"]), _const_spec(wts["w_gate"])],
        out_specs=[row(RW_COLS), row(MLA_LOW_COLS), row(2 * d)],
        out_shape=[jax.ShapeDtypeStruct((bsz, t, RW_COLS), F32),
                   jax.ShapeDtypeStruct((bsz, t, MLA_LOW_COLS), F32),
                   jax.ShapeDtypeStruct((bsz, t, 2 * d), F32)],
        compiler_params=_cparams("parallel", "arbitrary"),
        name="inproj",
    )(x3, mod.arr, mod.arr, wts["w_rw"], wts["w_mla"], wts["w_gate"])


RW_OUT_NAMES = ("r", "kmod", "v", "kkn", "bvec", "logw", "bonus", "g")


def _rw_elementwise(p, pp, mu, w_lora, w_gate_up, w_decay0, a0, k_k, k_a, r_k, ones_blk):
    w = RW_WIDTH
    pm = p + (pp - p) * mu
    r, k, v = pm[:, 0:w], pm[:, w:2 * w], pm[:, 2 * w:3 * w]
    xwa = pm[:, 3 * w:3 * w + DECAY_LORA + ICLR_LORA]
    xg = pm[:, 3 * w + DECAY_LORA + ICLR_LORA:]
    lane = _iota((1, DECAY_LORA + ICLR_LORA), 1)
    z = jnp.where(lane < DECAY_LORA, jnp.tanh(xwa), xwa)
    lora = _mm(z, w_lora)
    y = -(w_decay0 + lora[:, :w])
    softplus = jnp.maximum(y, 0.0) + jnp.log1p(jnp.exp(-jnp.abs(y)))
    w_log = -softplus - 0.5
    logw = -jnp.exp(w_log)
    a = jax.nn.sigmoid(a0 + lora[:, w:])
    g = _mm(jax.nn.sigmoid(xg), w_gate_up)
    kk = k * k_k
    kk = kk * lax.rsqrt(_mm_exact_rhs(kk * kk, ones_blk) + 1e-12)
    kmod = k * (1.0 + (a - 1.0) * k_a)
    bonus = _mm_exact_rhs(r * kmod * r_k, ones_blk) * v
    return r, kmod, v, -kk, kk * a, logw, bonus, g


def _rwprep_prompt_kernel(p_ref, prev_ref, s0_ref, mu_ref, wl_ref, wgu_ref, wd0_ref, a0_ref, kk_ref, ka_ref, rk_ref,
                          ones_ref, *out_refs):
    i = pl.program_id(1)
    p = p_ref[...]
    carry = jnp.where(i == 0, s0_ref[...], prev_ref[7:8, :])
    rolled = pltpu.roll(p, 1, 0)
    pp = jnp.where(_iota(p.shape, 0) == 0, carry, rolled)
    outs = _rw_elementwise(p, pp, mu_ref[...], wl_ref[...], wgu_ref[...], wd0_ref[...], a0_ref[...], kk_ref[...],
                           ka_ref[...], rk_ref[...], ones_ref[...])
    for o_ref, val in zip(out_refs, outs):
        o_ref[...] = val


def _rwprep_sample_kernel(p_ref, pp_ref, mu_ref, wl_ref, wgu_ref, wd0_ref, a0_ref, kk_ref, ka_ref, rk_ref,
                          ones_ref, *out_refs):
    outs = _rw_elementwise(p_ref[...], pp_ref[...], mu_ref[...], wl_ref[...], wgu_ref[...], wd0_ref[...], a0_ref[...],
                           kk_ref[...], ka_ref[...], rk_ref[...], ones_ref[...])
    for o_ref, val in zip(out_refs, outs):
        o_ref[...] = val


def _rw_param_list(wts):
    return [wts["mu_shift"], wts["w_lora"], wts["w_gate_up"], wts["w_decay0"], wts["a0"], wts["k_k"], wts["k_a"],
            wts["r_k"], wts["ones_blk"]]


def _rwprep_prompt(p_rw, shift0, wts, tile):
    bsz, t, _ = p_rw.shape
    params = _rw_param_list(wts)
    row = lambda w: pl.BlockSpec((None, tile, w), lambda b, i: (b, i, 0))
    prev = pl.BlockSpec((None, 8, RW_COLS), lambda b, i: (b, jnp.maximum(i * (tile // 8) - 1, 0), 0))
    s0 = pl.BlockSpec((None, 1, RW_COLS), lambda b, i: (b, 0, 0))
    return pl.pallas_call(
        _rwprep_prompt_kernel,
        grid=(bsz, t // tile),
        in_specs=[row(RW_COLS), prev, s0] + [_const_spec(a) for a in params],
        out_specs=[row(RW_WIDTH)] * 8,
        out_shape=[jax.ShapeDtypeStruct((bsz, t, RW_WIDTH), F32)] * 8,
        compiler_params=_cparams("parallel", "arbitrary"),
        name="rwprep_prompt",
    )(p_rw, p_rw, shift0, *params)


def _rwprep_sample(p_rw, p_prev, wts):
    n = p_rw.shape[0]
    params = _rw_param_list(wts)
    return pl.pallas_call(
        _rwprep_sample_kernel,
        grid=(1,),
        in_specs=[_const_spec(p_rw), _const_spec(p_prev)] + [_const_spec(a) for a in params],
        out_specs=[pl.BlockSpec((n, RW_WIDTH), lambda i: (0, 0))] * 8,
        out_shape=[jax.ShapeDtypeStruct((n, RW_WIDTH), F32)] * 8,
        compiler_params=_cparams("arbitrary"),
        name="rwprep_sample",
    )(p_rw, p_prev, *params)


def _group_norm_pair(o, ones_pair):
    mu = _mm_exact_rhs(o, ones_pair) * (1.0 / RW_HEAD_DIM)
    d = o - mu
    var = _mm_exact_rhs(d * d, ones_pair) * (1.0 / RW_HEAD_DIM)
    return d * lax.rsqrt(var + GN_EPS)


def _scan_kernel(r_ref, k_ref, v_ref, a_ref, b_ref, lw_ref, bon_ref, g_ref, lng_ref, lnb_ref, o_ref, st_ref, h_scr):
    c = pl.program_id(1)
    n_chunks = pl.num_programs(1)
    n_pairs = h_scr.shape[0]
    cl = r_ref.shape[0]

    @pl.when(c == 0)
    def _():
        h_scr[...] = jnp.zeros_like(h_scr)

    row = _iota((cl, cl), 0)
    col = _iota((cl, cl), 1)
    tri_incl = row >= col
    tri_strict = row > col
    tri_ones = tri_incl.astype(BF16)
    eye = (row == col).astype(F32)
    head0 = _iota((1, LANES), 1) < RW_HEAD_DIM
    head0_2 = (_iota((1, 2 * LANES), 1) & RW_HEAD_DIM) == 0
    r128 = _iota((LANES, LANES), 0)
    c128 = _iota((LANES, LANES), 1)
    same_head = (r128 < RW_HEAD_DIM) == (c128 < RW_HEAD_DIM)
    diag = r128 == c128
    ones_pair = same_head.astype(BF16)
    n_square = int(math.log2(cl)) - 1

    for p in range(n_pairs):
        sl = slice(LANES * p, LANES * (p + 1))
        lw = lw_ref[:, sl]
        l1 = lw.astype(BF16)
        rem = lw - l1.astype(F32)
        l2 = rem.astype(BF16)
        l3 = (rem - l2.astype(F32)).astype(BF16)
        lg = _mm(tri_ones, l1) + (_mm(tri_ones, l2) + _mm(tri_ones, l3))
        lg_last = lg[cl - 1:cl, :]
        rr = r_ref[:, sl]
        kk = k_ref[:, sl]
        vv = v_ref[:, sl]
        bb = b_ref[:, sl]
        rh = rr * jnp.exp(lg)
        ah = a_ref[:, sl] * jnp.exp(lg - lw)
        inv = jnp.exp(-lg)
        bh = bb * inv
        kh = kk * inv
        tail = jnp.exp(lg_last - lg)
        bt = bb * tail
        kt = kk * tail
        ar = jnp.concatenate([ah, rh], axis=0)

        t_inv, l_ak, m_rb, m_rk = [], [], [], []
        for h in range(2):
            mask = head0 if h == 0 else jnp.logical_not(head0)
            ar_h = jnp.where(mask, ar, 0.0)
            xb = _mm_t(ar_h, bh)
            xk = _mm_t(ar_h, kh)
            l_ab = jnp.where(tri_strict, xb[:cl], 0.0)
            m_rb.append(jnp.where(tri_incl, xb[cl:], 0.0))
            l_ak.append(jnp.where(tri_strict, xk[:cl], 0.0))
            m_rk.append(jnp.where(tri_incl, xk[cl:], 0.0))
            acc = eye + l_ab
            pw = l_ab
            for _ in range(n_square):
                pw = _mm_tinv(pw, pw)
                acc = acc + _mm_tinv(pw, acc)
            t_inv.append(acc)

        lv = jnp.where(head0, _mm(l_ak[0], vv), _mm(l_ak[1], vv))
        z = jnp.concatenate([ah, lv], axis=1)
        w12 = jnp.where(head0_2, _mm(t_inv[0], z), _mm(t_inv[1], z))
        q12 = jnp.where(head0_2, _mm(m_rb[0], w12), _mm(m_rb[1], w12))
        q1 = rh + q12[:, :LANES]
        q2 = q12[:, LANES:] + jnp.where(head0, _mm(m_rk[0], vv), _mm(m_rk[1], vv))
        g12 = _mm_tl(bt, w12)
        g1 = jnp.where(same_head, g12[:, :LANES], 0.0) + jnp.where(diag, jnp.exp(lg_last), 0.0)
        g2 = jnp.where(same_head, g12[:, LANES:] + _mm_tl(kt, vv), 0.0)

        h0 = h_scr[p]
        o = _mm3(q1, h0) + q2
        h_scr[p] = _mm3(g1, h0) + g2

        y = _group_norm_pair(o, ones_pair) * lng_ref[:, sl] + lnb_ref[:, sl]
        o_ref[:, sl] = (y + bon_ref[:, sl]) * g_ref[:, sl]

    @pl.when(c == n_chunks - 1)
    def _():
        st_ref[...] = h_scr[...]


def _rw_scan(rw, wts):
    bsz, t, w = rw["r"].shape
    n_pairs = w // LANES
    blk = pl.BlockSpec((None, SCAN_CHUNK, w), lambda b, c: (b, c, 0))
    ins = [rw[n] for n in ("r", "kmod", "v", "kkn", "bvec", "logw", "bonus", "g")]
    return pl.pallas_call(
        _scan_kernel,
        grid=(bsz, t // SCAN_CHUNK),
        in_specs=[blk] * 8 + [_const_spec(wts["lnx_g"]), _const_spec(wts["lnx_b"])],
        out_specs=[blk, pl.BlockSpec((None, n_pairs, LANES, LANES), lambda b, c: (b, 0, 0, 0))],
        out_shape=[jax.ShapeDtypeStruct((bsz, t, w), F32),
                   jax.ShapeDtypeStruct((bsz, n_pairs, LANES, LANES), F32)],
        scratch_shapes=[pltpu.VMEM((n_pairs, LANES, LANES), F32)],
        compiler_params=_cparams("parallel", "arbitrary"),
        name="rw_scan",
    )(*ins, wts["lnx_g"], wts["lnx_b"])


def _unpack_state(st):
    hd = RW_HEAD_DIM
    blocks = [st[:, p, j * hd:(j + 1) * hd, j * hd:(j + 1) * hd] for p in range(st.shape[1]) for j in range(2)]
    return jnp.swapaxes(jnp.stack(blocks, axis=1), -1, -2)


def _rwstep_kernel(s_ref, r_ref, k_ref, v_ref, a_ref, b_ref, lw_ref, bon_ref, g_ref, lng_ref, lnb_ref, so_ref, o_ref):
    hd = RW_HEAD_DIM
    outs = []
    for h in range(RW_HEADS):
        sl = slice(hd * h, hd * (h + 1))
        s = s_ref[h]
        sa = _mm3_t(a_ref[:, sl], s)
        left = jnp.concatenate([sa, v_ref[:, sl]], axis=0)
        right = jnp.concatenate([b_ref[:, sl], k_ref[:, sl]], axis=0)
        s_new = s * jnp.exp(lw_ref[:, sl]) + _mm3_tl(left, right)
        so_ref[h] = s_new
        o = _mm3_t(r_ref[:, sl], s_new)
        mu = jnp.mean(o, axis=-1, keepdims=True)
        d = o - mu
        var = jnp.mean(d * d, axis=-1, keepdims=True)
        outs.append(d * lax.rsqrt(var + GN_EPS))
    y = jnp.concatenate(outs, axis=1) * lng_ref[...] + lnb_ref[...]
    o_ref[...] = (y + bon_ref[...]) * g_ref[...]


def _rw_step(state, rw, wts):
    n = state.shape[0]
    w = RW_WIDTH
    vec = pl.BlockSpec((None, 1, w), lambda i: (i, 0, 0))
    st = pl.BlockSpec((None, RW_HEADS, RW_HEAD_DIM, RW_HEAD_DIM), lambda i: (i, 0, 0, 0))
    ins = [rw[nm].reshape(n, 1, w) for nm in ("r", "kmod", "v", "kkn", "bvec", "logw", "bonus", "g")]
    new_state, o = pl.pallas_call(
        _rwstep_kernel,
        grid=(n,),
        in_specs=[st] + [vec] * 8 + [_const_spec(wts["lnx_g"]), _const_spec(wts["lnx_b"])],
        out_specs=[st, vec],
        out_shape=[jax.ShapeDtypeStruct(state.shape, F32), jax.ShapeDtypeStruct((n, 1, w), F32)],
        compiler_params=_cparams("parallel"),
        name="rw_step",
    )(state, *ins, wts["lnx_g"], wts["lnx_b"])
    return new_state, o.reshape(n, w)


def _rms(x, g):
    return x * lax.rsqrt(jnp.mean(x * x, axis=-1, keepdims=True) + RMS_EPS) * g


def _mla_common(low, pos, gq, gkv, wqa, wqb, invf):
    cq = _rms(low[:, :Q_LORA], gq)
    ckv = _rms(low[:, Q_LORA:Q_LORA + KV_LORA], gkv)
    kr = low[:, Q_LORA + KV_LORA:Q_LORA + KV_LORA + LANES]
    kr_rot = low[:, Q_LORA + KV_LORA + LANES:]
    ang = pos * invf
    cos = jnp.cos(ang)
    sin = jnp.sin(ang)
    cos8 = jnp.concatenate([cos] * MLA_HEADS, axis=1)
    sin8 = jnp.concatenate([sin] * MLA_HEADS, axis=1)
    q = (_mm(cq, wqa) * cos8 + _mm(cq, wqb) * sin8) * ATTN_SCALE
    k_rope = kr * pltpu.roll(cos, LANES - QK_NOPE, 1) + kr_rot * pltpu.roll(sin, LANES - QK_NOPE, 1)
    return q, ckv, k_rope


def _mlaprep_prompt_kernel(low_ref, gq_ref, gkv_ref, wqa_ref, wqb_ref, wuk_ref, wuv_ref, invf_ref,
                           q_ref, k_ref, v_ref, ckv_ref, kr_ref):
    i = pl.program_id(1)
    rows = low_ref.shape[0]
    pos = (i * rows + _iota((rows, 1), 0)).astype(F32)
    q, ckv, k_rope = _mla_common(low_ref[...], pos, gq_ref[...], gkv_ref[...], wqa_ref[...], wqb_ref[...],
                                 invf_ref[...])
    q_ref[...] = q.astype(BF16)
    ckv_ref[...] = ckv
    kr_ref[...] = k_rope[:, :QK_ROPE]
    k_slot = pltpu.roll(k_rope, QK_NOPE, 1)
    k_ref[...] = (_mm(ckv, wuk_ref[...]) + jnp.concatenate([k_slot] * MLA_HEADS, axis=1)).astype(BF16)
    v_ref[...] = _mm(ckv, wuv_ref[...]).astype(BF16)


def _mlaprep_prompt(low, wts, tile):
    bsz, t, _ = low.shape
    params = [wts["g_qnorm"], wts["g_kvnorm"], wts["w_qa"], wts["w_qb"], wts["w_uk_pad"], wts["w_uv"], wts["invf"]]
    row = lambda w: pl.BlockSpec((None, tile, w), lambda b, i: (b, i, 0))
    slots = MLA_HEADS * HEAD_SLOT
    return pl.pallas_call(
        _mlaprep_prompt_kernel,
        grid=(bsz, t // tile),
        in_specs=[row(MLA_LOW_COLS)] + [_const_spec(a) for a in params],
        out_specs=[row(slots), row(slots), row(MLA_HEADS * V_HEAD), row(KV_LORA), row(QK_ROPE)],
        out_shape=[jax.ShapeDtypeStruct((bsz, t, slots), BF16), jax.ShapeDtypeStruct((bsz, t, slots), BF16),
                   jax.ShapeDtypeStruct((bsz, t, MLA_HEADS * V_HEAD), BF16),
                   jax.ShapeDtypeStruct((bsz, t, KV_LORA), F32), jax.ShapeDtypeStruct((bsz, t, QK_ROPE), F32)],
        compiler_params=_cparams("parallel", "arbitrary"),
        name="mlaprep_prompt",
    )(low, *params)


def _mlaprep_sample_kernel(low_ref, gq_ref, gkv_ref, wqa_ref, wqb_ref, wukt_ref, invf_ref, pos_ref,
                           qlat_ref, qr_ref, ckv_ref, kr_ref):
    q, ckv, k_rope = _mla_common(low_ref[...], pos_ref[...], gq_ref[...], gkv_ref[...], wqa_ref[...], wqb_ref[...],
                                 invf_ref[...])
    ckv_ref[...] = ckv
    kr_ref[...] = k_rope[:, :QK_ROPE]
    rope_lanes = _iota((1, HEAD_SLOT), 1) < QK_ROPE
    for h in range(MLA_HEADS):
        slot = q[:, HEAD_SLOT * h:HEAD_SLOT * (h + 1)]
        qlat_ref[:, KV_LORA * h:KV_LORA * (h + 1)] = _mm(slot, wukt_ref[h])
        qr_ref[:, HEAD_SLOT * h:HEAD_SLOT * (h + 1)] = jnp.where(rope_lanes, pltpu.roll(slot, LANES - QK_NOPE, 1), 0.0)


def _mlaprep_sample(low, pos, wts):
    n = low.shape[0]
    params = [wts["g_qnorm"], wts["g_kvnorm"], wts["w_qa"], wts["w_qb"], wts["w_ukt_pad"], wts["invf"], pos]
    full = lambda shape: pl.BlockSpec(shape, lambda i: (0,) * len(shape))
    return pl.pallas_call(
        _mlaprep_sample_kernel,
        grid=(1,),
        in_specs=[_const_spec(low)] + [_const_spec(a) for a in params],
        out_specs=[full((n, MLA_HEADS * KV_LORA)), full((n, MLA_HEADS * HEAD_SLOT)), full((n, KV_LORA)),
                   full((n, QK_ROPE))],
        out_shape=[jax.ShapeDtypeStruct((n, MLA_HEADS * KV_LORA), F32),
                   jax.ShapeDtypeStruct((n, MLA_HEADS * HEAD_SLOT), F32),
                   jax.ShapeDtypeStruct((n, KV_LORA), F32), jax.ShapeDtypeStruct((n, QK_ROPE), F32)],
        compiler_params=_cparams("arbitrary"),
        name="mlaprep_sample",
    )(low, *params)


def _attn_kernel(q_ref, k_ref, v_ref, o_ref, m_scr, l_scr, acc_scr):
    qi = pl.program_id(2)
    ki = pl.program_id(3)
    bq = q_ref.shape[0]
    bk = k_ref.shape[0]

    @pl.when(ki == 0)
    def _():
        m_scr[...] = jnp.full(m_scr.shape, -jnp.inf, F32)
        l_scr[...] = jnp.zeros_like(l_scr)
        acc_scr[...] = jnp.zeros_like(acc_scr)

    @pl.when(ki * bk <= qi * bq + (bq - 1))
    def _():
        qpos = qi * bq + _iota((bq, bk), 0)
        kpos = ki * bk + _iota((bq, bk), 1)
        visible = kpos <= qpos
        v = v_ref[...]
        for h in range(2):
            q = q_ref[:, HEAD_SLOT * h:HEAD_SLOT * (h + 1)]
            k = k_ref[:, HEAD_SLOT * h:HEAD_SLOT * (h + 1)]
            s = jnp.where(visible, _mm_t(q, k), -jnp.inf)
            m_prev = m_scr[h]
            m_new = jnp.maximum(m_prev, jnp.max(s, axis=1, keepdims=True))
            p = jnp.exp(s - m_new)
            alpha = jnp.exp(m_prev - m_new)
            l_scr[h] = alpha * l_scr[h] + jnp.sum(p, axis=1, keepdims=True)
            acc_scr[h] = alpha * acc_scr[h] + _mm(p, v)
            m_scr[h] = m_new

    @pl.when(ki == pl.num_programs(3) - 1)
    def _():
        head0 = _iota((1, LANES), 1) < V_HEAD
        o_ref[...] = jnp.where(head0, acc_scr[0] / l_scr[0], acc_scr[1] / l_scr[1])


def _attention_prompt(q, k, v, bq, bk):
    bsz, t, _ = q.shape
    n_pairs = MLA_HEADS // 2
    last_k = lambda qi: (qi * bq + (bq - 1)) // bk
    return pl.pallas_call(
        _attn_kernel,
        grid=(bsz, n_pairs, t // bq, t // bk),
        in_specs=[pl.BlockSpec((None, bq, 2 * HEAD_SLOT), lambda b, p, qi, ki: (b, qi, p)),
                  pl.BlockSpec((None, bk, 2 * HEAD_SLOT), lambda b, p, qi, ki: (b, jnp.minimum(ki, last_k(qi)), p)),
                  pl.BlockSpec((None, bk, LANES), lambda b, p, qi, ki: (b, jnp.minimum(ki, last_k(qi)), p))],
        out_specs=pl.BlockSpec((None, bq, LANES), lambda b, p, qi, ki: (b, qi, p)),
        out_shape=jax.ShapeDtypeStruct((bsz, t, MLA_HEADS * V_HEAD), F32),
        scratch_shapes=[pltpu.VMEM((2, bq, 1), F32), pltpu.VMEM((2, bq, 1), F32), pltpu.VMEM((2, bq, LANES), F32)],
        compiler_params=_cparams("parallel", "parallel", "parallel", "arbitrary"),
        name="attn_prompt",
    )(q, k, v)


def _sample_attn_kernel(pt_ref, qlat_ref, qr_ref, cn_ref, kn_ref, ckv_hbm, kr_hbm, o_ref, kbuf, rbuf, s_scr, sem):
    b = pl.program_id(0)
    n_seq = pl.num_programs(0)
    n_pages = kbuf.shape[1]
    slot = lax.rem(b, 2)

    def page_copies(seq, slt):
        out = []
        for pg in range(n_pages):
            page = pt_ref[seq, pg]
            out.append(pltpu.make_async_copy(ckv_hbm.at[page], kbuf.at[slt, pg], sem.at[slt, 0]))
            out.append(pltpu.make_async_copy(kr_hbm.at[page], rbuf.at[slt, pg], sem.at[slt, 1]))
        return out

    @pl.when(b == 0)
    def _():
        for cp in page_copies(0, 0):
            cp.start()

    @pl.when(b + 1 < n_seq)
    def _():
        for cp in page_copies(b + 1, 1 - slot):
            cp.start()

    for cp in page_copies(b, slot):
        cp.wait()

    ql = qlat_ref[...]
    qr = qr_ref[:, :QK_ROPE]
    n_chunks = n_pages // 2
    rows = 2 * PAGE_SIZE

    def score(i, carry):
        kc = kbuf[slot, pl.ds(2 * i, 2)].reshape(rows, KV_LORA)
        rc = rbuf[slot, pl.ds(2 * i, 2)].reshape(rows, QK_ROPE)
        s_scr[i] = _mm_t(ql, kc) + _mm_t(qr, rc)
        return carry

    lax.fori_loop(0, n_chunks, score, 0)
    cn = cn_ref[...]
    kn = kn_ref[...]
    s_new = jnp.sum(ql * cn, axis=-1, keepdims=True) + jnp.sum(qr * kn, axis=-1, keepdims=True)
    s_all = s_scr[...]
    m = jnp.maximum(jnp.max(jnp.max(s_all, axis=0), axis=-1, keepdims=True), s_new)
    p_all = jnp.exp(s_all - m)
    p_new = jnp.exp(s_new - m)
    denom = jnp.sum(jnp.sum(p_all, axis=0), axis=-1, keepdims=True) + p_new
    s_scr[...] = p_all

    def accumulate(i, acc):
        kc = kbuf[slot, pl.ds(2 * i, 2)].reshape(rows, KV_LORA)
        return acc + _mm(s_scr[i], kc)

    acc = lax.fori_loop(0, n_chunks, accumulate, jnp.zeros((ql.shape[0], KV_LORA), F32))
    o_ref[...] = (acc + p_new * cn) / denom


def _attention_sample(page_table, qlat, qr, ckv_new, kr_new, cache_ckv, cache_kr):
    n, n_pages = page_table.shape
    grid_spec = pltpu.PrefetchScalarGridSpec(
        num_scalar_prefetch=1,
        grid=(n,),
        in_specs=[pl.BlockSpec((None, MLA_HEADS, KV_LORA), lambda b, pt: (b, 0, 0)),
                  pl.BlockSpec((None, MLA_HEADS, HEAD_SLOT), lambda b, pt: (b, 0, 0)),
                  pl.BlockSpec((None, 1, KV_LORA), lambda b, pt: (b, 0, 0)),
                  pl.BlockSpec((None, 1, QK_ROPE), lambda b, pt: (b, 0, 0)),
                  pl.BlockSpec(memory_space=pl.ANY),
                  pl.BlockSpec(memory_space=pl.ANY)],
        out_specs=pl.BlockSpec((None, MLA_HEADS, KV_LORA), lambda b, pt: (b, 0, 0)),
        scratch_shapes=[pltpu.VMEM((2, n_pages, PAGE_SIZE, KV_LORA), F32),
                        pltpu.VMEM((2, n_pages, PAGE_SIZE, QK_ROPE), F32),
                        pltpu.VMEM((n_pages // 2, MLA_HEADS, 2 * PAGE_SIZE), F32),
                        pltpu.SemaphoreType.DMA((2, 2))],
    )
    return pl.pallas_call(
        _sample_attn_kernel,
        grid_spec=grid_spec,
        out_shape=jax.ShapeDtypeStruct((n, MLA_HEADS, KV_LORA), F32),
        compiler_params=_cparams("arbitrary"),
        name="attn_sample",
    )(page_table, qlat.reshape(n, MLA_HEADS, KV_LORA), qr.reshape(n, MLA_HEADS, HEAD_SLOT),
      ckv_new.reshape(n, 1, KV_LORA), kr_new.reshape(n, 1, QK_ROPE), cache_ckv, cache_kr)


def _uv_kernel(ol_ref, wuv_ref, o_ref):
    outs = [_mm(ol_ref[:, KV_LORA * h:KV_LORA * (h + 1)], wuv_ref[:, V_HEAD * h:V_HEAD * (h + 1)])
            for h in range(MLA_HEADS)]
    o_ref[...] = jnp.concatenate(outs, axis=1)


def _latent_to_heads(o_lat, w_uv):
    n = o_lat.shape[0]
    o_lat = o_lat.reshape(n, MLA_HEADS * KV_LORA)
    return pl.pallas_call(
        _uv_kernel,
        grid=(1,),
        in_specs=[_const_spec(o_lat), _const_spec(w_uv)],
        out_specs=pl.BlockSpec((n, MLA_HEADS * V_HEAD), lambda i: (0, 0)),
        out_shape=jax.ShapeDtypeStruct((n, MLA_HEADS * V_HEAD), F32),
        compiler_params=_cparams("arbitrary"),
        name="latent_to_heads",
    )(o_lat, w_uv)


def _layernorm(z, g, b):
    mu = jnp.mean(z, axis=-1, keepdims=True)
    d = z - mu
    var = jnp.mean(d * d, axis=-1, keepdims=True)
    return d * lax.rsqrt(var + LN_EPS) * g + b


def _first_max(x, idx, sentinel):
    mx = jnp.max(x, axis=0, keepdims=True)
    first = jnp.min(jnp.where(x == mx, idx, sentinel), axis=0, keepdims=True)
    return idx == first


def _route(scores, bias):
    n_tok = scores.shape[1]
    per_group = N_EXPERTS // N_GROUPS
    sb = scores + bias
    sb3 = sb.reshape(N_GROUPS, per_group, n_tok)
    member = _iota(sb3.shape, 1)
    m1 = jnp.max(sb3, axis=1, keepdims=True)
    first = jnp.min(jnp.where(sb3 == m1, member, per_group), axis=1, keepdims=True)
    m2 = jnp.max(jnp.where(member == first, -jnp.inf, sb3), axis=1, keepdims=True)
    g_score = (m1 + m2).reshape(N_GROUPS, n_tok)
    g_idx = _iota(g_score.shape, 0)
    g_sel = jnp.zeros(g_score.shape, F32)
    for _ in range(TOPK_GROUPS):
        hit = _first_max(g_score, g_idx, N_GROUPS)
        g_sel = jnp.where(hit, 1.0, g_sel)
        g_score = jnp.where(hit, -jnp.inf, g_score)
    e_mask = jnp.broadcast_to(g_sel.reshape(N_GROUPS, 1, n_tok), sb3.shape).reshape(N_EXPERTS, n_tok) > 0.5
    cur = jnp.where(e_mask, sb, -jnp.inf)
    e_idx = _iota(cur.shape, 0)
    sel = jnp.zeros(cur.shape, F32)
    for _ in range(TOP_K):
        hit = _first_max(cur, e_idx, N_EXPERTS)
        sel = jnp.where(hit, 1.0, sel)
        cur = jnp.where(hit, -jnp.inf, cur)
    picked = jnp.where(sel > 0.5, scores, 0.0)
    return picked / jnp.sum(picked, axis=0, keepdims=True) * ROUTED_SCALE


def _merge_kernel(x_ref, oa_ref, ob_ref, gs_ref, gt1_ref, sh2_ref, sc2_ref, wba_ref, wbb_ref, wout_ref, g1_ref, b1_ref,
                  wrt_ref, br_ref, x1_ref, h2_ref, cw_ref):
    d = D_MODEL
    ya = _mm(oa_ref[...], wba_ref[...])
    yb = _mm(ob_ref[...], wbb_ref[...])
    merged = gs_ref[:, :d] * ya + gs_ref[:, d:] * yb
    z = DN_ALPHA * x_ref[...] + gt1_ref[...] * _mm(merged, wout_ref[...])
    x1 = _layernorm(z, g1_ref[...], b1_ref[...])
    x1_ref[...] = x1
    h2 = x1 * (1 + sc2_ref[...]) + sh2_ref[...]
    h2_ref[...] = h2.astype(BF16)
    scores = jax.nn.sigmoid(_mm3_t(wrt_ref[...], h2))
    cw_ref[...] = _route(scores, br_ref[...]).T


def _merge(x3, oa, ob, gs, mod, wts, tile):
    bsz, t, d = x3.shape
    params = [wts["w_branch_a"], wts["w_branch_b"], wts["w_out"], wts["ln1_g"], wts["ln1_b"], wts["w_router_t"],
              wts["b_router"]]
    row = lambda w: pl.BlockSpec((None, tile, w), lambda b, i: (b, i, 0))
    return pl.pallas_call(
        _merge_kernel,
        grid=(bsz, t // tile),
        in_specs=[row(d), row(RW_WIDTH), row(MLA_HEADS * V_HEAD), row(2 * d), mod.spec(2), mod.spec(3), mod.spec(4)]
        + [_const_spec(a) for a in params],
        out_specs=[row(d), row(d), row(N_EXPERTS)],
        out_shape=[jax.ShapeDtypeStruct((bsz, t, d), F32), jax.ShapeDtypeStruct((bsz, t, d), BF16),
                   jax.ShapeDtypeStruct((bsz, t, N_EXPERTS), F32)],
        compiler_params=_cparams("parallel", "arbitrary"),
        name="merge",
    )(x3, oa, ob, gs, mod.arr, mod.arr, mod.arr, *params)


def _experts_kernel(h_ref, cw_ref, wg_ref, wu_ref, wd_ref, o_ref):
    e = pl.program_id(1)

    @pl.when(e == 0)
    def _():
        o_ref[...] = jnp.zeros_like(o_ref)

    x = h_ref[...]
    act = _silu(jnp.dot(x, wg_ref[...], preferred_element_type=F32)) * jnp.dot(x, wu_ref[...],
                                                                             preferred_element_type=F32)
    lane = _iota((1, N_EXPERTS), 1)
    w_col = jnp.sum(jnp.where(lane == e, cw_ref[...], 0.0), axis=1, keepdims=True)
    o_ref[...] += _mm(act * w_col, wd_ref[...])


def _experts(h2, cw, wts, tile):
    n, d = h2.shape
    return pl.pallas_call(
        _experts_kernel,
        grid=(n // tile, N_EXPERTS),
        in_specs=[pl.BlockSpec((tile, d), lambda i, e: (i, 0)),
                  pl.BlockSpec((tile, N_EXPERTS), lambda i, e: (i, 0)),
                  pl.BlockSpec((None, d, EXPERT_FF), lambda i, e: (e, 0, 0)),
                  pl.BlockSpec((None, d, EXPERT_FF), lambda i, e: (e, 0, 0)),
                  pl.BlockSpec((None, EXPERT_FF, d), lambda i, e: (e, 0, 0))],
        out_specs=pl.BlockSpec((tile, d), lambda i, e: (i, 0)),
        out_shape=jax.ShapeDtypeStruct((n, d), F32),
        compiler_params=_cparams("parallel", "arbitrary"),
        name="experts",
    )(h2, cw, wts["w_exp_gate"], wts["w_exp_up"], wts["w_exp_down"])


def _final_kernel(x1_ref, h2_ref, routed_ref, gt2_ref, wsg_ref, wsu_ref, wsd_ref, g2_ref, b2_ref, y_ref):
    h2 = h2_ref[...]
    act = _silu(jnp.dot(h2, wsg_ref[...], preferred_element_type=F32)) * jnp.dot(h2, wsu_ref[...],
                                                                               preferred_element_type=F32)
    ffn = routed_ref[...] + _mm(act, wsd_ref[...])
    y_ref[...] = _layernorm(DN_ALPHA * x1_ref[...] + gt2_ref[...] * ffn, g2_ref[...], b2_ref[...])


def _final(x1, h2, routed, mod, wts, tile):
    bsz, t, d = x1.shape
    params = [wts["w_sh_gate"], wts["w_sh_up"], wts["w_sh_down"], wts["ln2_g"], wts["ln2_b"]]
    row = lambda w: pl.BlockSpec((None, tile, w), lambda b, i: (b, i, 0))
    return pl.pallas_call(
        _final_kernel,
        grid=(bsz, t // tile),
        in_specs=[row(d), row(d), row(d), mod.spec(5)] + [_const_spec(a) for a in params],
        out_specs=row(d),
        out_shape=jax.ShapeDtypeStruct((bsz, t, d), F32),
        compiler_params=_cparams("parallel", "arbitrary"),
        name="final",
    )(x1, h2, routed, mod.arr, *params)


def _rope_rotation_columns(w):
    half = QK_ROPE // 2
    return jnp.concatenate([-w[..., half:], w[..., :half]], axis=-1)


def _prepare_weights(P):
    d = D_MODEL
    w = {}
    w_in = P["w_in"]
    o1 = RW_COLS
    o2 = o1 + Q_LORA
    o3 = o2 + KV_LORA
    o4 = o3 + QK_ROPE
    w_kr = w_in[:, o3:o4]
    pad = jnp.zeros((d, LANES - QK_ROPE), F32)
    w["w_rw"] = w_in[:, :o1].astype(BF16)
    w["w_mla"] = jnp.concatenate([w_in[:, o1:o3], w_kr, pad, _rope_rotation_columns(w_kr), pad], axis=1).astype(BF16)
    w["w_gate"] = w_in[:, o4:].astype(BF16)

    row = lambda v: v.reshape(1, -1)
    w["mu_shift"] = row(P["mu_shift"])
    zl = jnp.zeros((DECAY_LORA, RW_WIDTH), F32)
    w["w_lora"] = jnp.concatenate([jnp.concatenate([P["w_decay_up"], zl], axis=1),
                                   jnp.concatenate([zl, P["w_iclr_up"]], axis=1)], axis=0).astype(BF16)
    w["w_gate_up"] = P["w_gate_up"].astype(BF16)
    for nm in ("w_decay0", "a0", "k_k", "k_a", "r_k", "lnx_g", "lnx_b", "g_qnorm", "g_kvnorm", "ln1_g", "ln1_b",
               "ln2_g", "ln2_b"):
        w[nm] = row(P[nm])
    head_of = np.arange(RW_WIDTH) // RW_HEAD_DIM
    w["ones_blk"] = jnp.asarray(head_of[:, None] == head_of[None, :], BF16)

    wq = P["w_uq"].reshape(Q_LORA, MLA_HEADS, QK_NOPE + QK_ROPE)
    zq = jnp.zeros((Q_LORA, MLA_HEADS, HEAD_SLOT - QK_NOPE - QK_ROPE), F32)
    w["w_qa"] = jnp.concatenate([wq, zq], axis=-1).reshape(Q_LORA, MLA_HEADS * HEAD_SLOT).astype(BF16)
    w["w_qb"] = jnp.concatenate([jnp.zeros((Q_LORA, MLA_HEADS, QK_NOPE), F32), _rope_rotation_columns(wq[..., QK_NOPE:]), zq],
                                axis=-1).reshape(Q_LORA, MLA_HEADS * HEAD_SLOT).astype(BF16)
    wuk = P["w_uk"].reshape(KV_LORA, MLA_HEADS, QK_NOPE)
    zk = jnp.zeros((KV_LORA, MLA_HEADS, HEAD_SLOT - QK_NOPE), F32)
    w["w_uk_pad"] = jnp.concatenate([wuk, zk], axis=-1).reshape(KV_LORA, MLA_HEADS * HEAD_SLOT).astype(BF16)
    w["w_ukt_pad"] = jnp.transpose(jnp.concatenate([wuk, zk], axis=-1), (1, 2, 0)).astype(BF16)
    w["w_uv"] = P["w_uv"].astype(BF16)
    half = QK_ROPE // 2
    inv = ROPE_THETA ** (-jnp.arange(half, dtype=F32) / half)
    w["invf"] = jnp.concatenate([jnp.zeros((QK_NOPE,), F32), inv, inv,
                                 jnp.zeros((HEAD_SLOT - QK_NOPE - QK_ROPE,), F32)]).reshape(1, HEAD_SLOT)

    for nm in ("w_branch_a", "w_branch_b", "w_out", "w_exp_gate", "w_exp_up", "w_exp_down", "w_sh_gate", "w_sh_up",
               "w_sh_down"):
        w[nm] = P[nm].astype(BF16)
    w["w_router_t"] = P["w_router"].T
    w["b_router"] = P["b_router"].reshape(N_EXPERTS, 1)
    return w


def _row_tile(t, cap):
    tile = min(t, cap)
    assert t % tile == 0 and tile % 16 == 0, (t, tile)
    return tile


def _layer_tail(x3, oa, ob, gs, mod, wts, tile, moe_tile):
    bsz, t, d = x3.shape
    x1, h2, cw = _merge(x3, oa, ob, gs, mod, wts, tile)
    routed = _experts(h2.reshape(bsz * t, d), cw.reshape(bsz * t, N_EXPERTS), wts, moe_tile).reshape(bsz, t, d)
    return _final(x1, h2, routed, mod, wts, tile)


def _layer_prompt(x, mod_rows, wts):
    bsz, t, d = x.shape
    tile = _row_tile(t, 256)
    mod = _Mod(mod_rows.reshape(bsz * 6, 1, d), per_token=False, tile=tile)
    p_rw, low, gs = _inproj(x, mod, wts, tile)
    rw = dict(zip(RW_OUT_NAMES, _rwprep_prompt(p_rw, jnp.zeros((bsz, 1, RW_COLS), F32), wts, tile)))
    oa, state = _rw_scan(rw, wts)
    q, k, v, ckv, k_rope = _mlaprep_prompt(low, wts, tile)
    ob = _attention_prompt(q, k, v, _row_tile(t, 512), _row_tile(t, 1024))
    y = _layer_tail(x, oa, ob, gs, mod, wts, tile, _row_tile(bsz * t, 1024))
    return y, ckv, k_rope, _unpack_state(state), p_rw[:, -1]


def _layer_sample(x, mod_rows, state, shift, page_table, cache_ckv, cache_kr, wts):
    n, s_new, d = x.shape
    assert s_new == 1
    past = page_table.shape[1] * PAGE_SIZE
    x3 = x.reshape(1, n, d)
    mod = _Mod(mod_rows, per_token=True, tile=n)
    p_rw, low, gs = _inproj(x3, mod, wts, n)
    p_rw2 = p_rw.reshape(n, RW_COLS)
    rw = dict(zip(RW_OUT_NAMES, _rwprep_sample(p_rw2, shift, wts)))
    new_state, oa = _rw_step(state, rw, wts)
    pos = jnp.full((n, 1), past, F32)
    qlat, qr, ckv, k_rope = _mlaprep_sample(low.reshape(n, MLA_LOW_COLS), pos, wts)
    o_lat = _attention_sample(page_table, qlat, qr, ckv, k_rope, cache_ckv, cache_kr)
    ob = _latent_to_heads(o_lat, wts["w_uv"])
    y = _layer_tail(x3, oa.reshape(1, n, RW_WIDTH), ob.reshape(1, n, MLA_HEADS * V_HEAD), gs, mod, wts, n, n)
    return (y.reshape(n, 1, d), ckv.reshape(n, 1, KV_LORA), k_rope.reshape(n, 1, QK_ROPE), new_state, p_rw2)


def kernel(x_prompt, x_sample, c_prompt, c_sample, cache_ckv, cache_krope, state_wkv, state_shift, page_table, w_ada, b_ada, w_in, mu_shift, w_decay0, w_decay_up, a0, w_iclr_up, w_gate_up, k_k, k_a, r_k, lnx_g, lnx_b, w_branch_a, g_qnorm, w_uq, g_kvnorm, w_uk, w_uv, w_branch_b, w_out, ln1_g, ln1_b, w_router, b_router, w_exp_gate, w_exp_up, w_exp_down, w_sh_gate, w_sh_up, w_sh_down, ln2_g, ln2_b):
    params = dict(w_ada=w_ada, b_ada=b_ada, w_in=w_in, mu_shift=mu_shift, w_decay0=w_decay0, w_decay_up=w_decay_up,
                  a0=a0, w_iclr_up=w_iclr_up, w_gate_up=w_gate_up, k_k=k_k, k_a=k_a, r_k=r_k, lnx_g=lnx_g,
                  lnx_b=lnx_b, w_branch_a=w_branch_a, g_qnorm=g_qnorm, w_uq=w_uq, g_kvnorm=g_kvnorm, w_uk=w_uk,
                  w_uv=w_uv, w_branch_b=w_branch_b, w_out=w_out, ln1_g=ln1_g, ln1_b=ln1_b, w_router=w_router,
                  b_router=b_router, w_exp_gate=w_exp_gate, w_exp_up=w_exp_up, w_exp_down=w_exp_down,
                  w_sh_gate=w_sh_gate, w_sh_up=w_sh_up, w_sh_down=w_sh_down, ln2_g=ln2_g, ln2_b=ln2_b)
    depth = w_in.shape[0]
    bp = x_prompt.shape[0]
    bd = x_sample.shape[0]
    n_c = bp + bd
    c_all = jnp.concatenate([c_prompt, c_sample, jnp.zeros((-n_c % 8, D_MODEL), F32)], axis=0)
    yp, ys = x_prompt, x_sample
    outs = [[] for _ in range(8)]
    for l in range(depth):
        wts = _prepare_weights({name: arr[l] for name, arr in params.items()})
        mod = _adaln_mod(c_all, params["w_ada"][l], params["b_ada"][l].reshape(1, -1))
        yp, *rest_p = _layer_prompt(yp, mod[:bp], wts)
        ys, *rest_s = _layer_sample(ys, mod[bp:n_c], state_wkv[l], state_shift[l], page_table, cache_ckv[l],
                                    cache_krope[l], wts)
        for acc, val in zip(outs, rest_p + rest_s):
            acc.append(val)
    return (yp, ys) + tuple(jnp.stack(o) for o in outs)
```

```python
import functools
import math

import numpy as np
import jax
import jax.numpy as jnp
from jax import lax
from jax.experimental import pallas as pl
from jax.experimental.pallas import tpu as pltpu

F32 = jnp.float32
BF16 = jnp.bfloat16

D_MODEL = 1024
PAGE_SIZE = 128
RW_HEADS = 8
RW_HEAD_DIM = 64
RW_WIDTH = RW_HEADS * RW_HEAD_DIM
DECAY_LORA = 64
ICLR_LORA = 64
GATE_LORA = 128
RW_COLS = 3 * RW_WIDTH + DECAY_LORA + ICLR_LORA + GATE_LORA
GN_EPS = 64e-5
MLA_HEADS = 8
QK_NOPE = 64
QK_ROPE = 32
V_HEAD = 64
Q_LORA = 384
KV_LORA = 256
ROPE_THETA = 10000.0
ATTN_SCALE = (QK_NOPE + QK_ROPE) ** -0.5
N_EXPERTS = 64
TOP_K = 8
N_GROUPS = 8
TOPK_GROUPS = 4
EXPERT_FF = 256
SHARED_FF = 256
ROUTED_SCALE = 2.5
DEPTH = 1
DN_ALPHA = (2 * DEPTH) ** 0.25
LN_EPS = 1e-5
RMS_EPS = 1e-6

LANES = 128
HEAD_SLOT = LANES
MLA_LOW_COLS = Q_LORA + KV_LORA + 2 * LANES
SCAN_CHUNK = 64
SCAN_SUBCHUNKS = 2
SAMPLE_ATTN_UNROLL = 8
VMEM_LIMIT = 56 * 1024 * 1024


def _cparams(*sem):
    return pltpu.CompilerParams(dimension_semantics=sem, vmem_limit_bytes=VMEM_LIMIT)


def _mm(a, b):
    return jnp.dot(a.astype(BF16), b.astype(BF16), preferred_element_type=F32)


def _mm_t(a, b):
    return lax.dot_general(a.astype(BF16), b.astype(BF16), (((1,), (1,)), ((), ())), preferred_element_type=F32)


def _mm_tl(a, b):
    return lax.dot_general(a.astype(BF16), b.astype(BF16), (((0,), (0,)), ((), ())), preferred_element_type=F32)


def _split(x):
    hi = x.astype(BF16)
    lo = (x - hi.astype(F32)).astype(BF16)
    return hi, lo


def _three_pass(f, a, b):
    ah, al = _split(a)
    bh, bl = _split(b)
    return f(ah, bh) + (f(ah, bl) + f(al, bh))


def _mm3(a, b):
    return _three_pass(_mm, a, b)


def _mm3_t(a, b):
    return _three_pass(_mm_t, a, b)


def _mm3_tl(a, b):
    return _three_pass(_mm_tl, a, b)


_mm_state = _mm
_mm_merge = _mm


def _mm_exact_rhs(a, b_exact):
    ah, al = _split(a)
    return _mm(ah, b_exact) + _mm(al, b_exact)


def _silu(x):
    return x * jax.nn.sigmoid(x)


def _iota(shape, dim):
    return lax.broadcasted_iota(jnp.int32, shape, dim)


def _mod_kernel(c_ref, w_ref, b_ref, o_ref):
    o_ref[...] = _mm(_silu(c_ref[...]), w_ref[...]) + b_ref[...]


def _adaln_mod(c_all, w_ada, b_ada):
    n = c_all.shape[0]
    d = D_MODEL
    return pl.pallas_call(
        _mod_kernel,
        grid=(6,),
        in_specs=[pl.BlockSpec((n, d), lambda j: (0, 0)),
                  pl.BlockSpec((d, d), lambda j: (0, j)),
                  pl.BlockSpec((1, d), lambda j: (0, j))],
        out_specs=pl.BlockSpec((n, d), lambda j: (0, j)),
        out_shape=jax.ShapeDtypeStruct((n, 6 * d), F32),
        compiler_params=_cparams("arbitrary"),
        name="adaln_mod",
    )(c_all, w_ada, b_ada)


class _Mod:
    def __init__(self, arr, per_token, tile):
        self.arr = arr
        self.per_token = per_token
        self.tile = tile

    def spec(self, j):
        if self.per_token:
            return pl.BlockSpec((self.tile, D_MODEL), lambda b, i, *_: (i, j))
        return pl.BlockSpec((None, 1, D_MODEL), lambda b, i, *_: (b * 6 + j, 0, 0))


def _const_spec(arr):
    nd = arr.ndim
    return pl.BlockSpec(arr.shape, lambda *_: (0,) * nd)


def _inproj_kernel(x_ref, sh_ref, sc_ref, wrw_ref, wmla_ref, wg_ref, prw_ref, mla_ref, gs_ref):
    h = (x_ref[...] * (1 + sc_ref[...]) + sh_ref[...]).astype(BF16)
    prw_ref[...] = jnp.dot(h, wrw_ref[...], preferred_element_type=F32)
    mla_ref[...] = jnp.dot(h, wmla_ref[...], preferred_element_type=F32)
    gs_ref[...] = jax.nn.sigmoid(jnp.dot(h, wg_ref[...], preferred_element_type=F32))


def _inproj(x3, mod, wts, tile):
    bsz, t, d = x3.shape
    row = lambda w: pl.BlockSpec((None, tile, w), lambda b, i: (b, i, 0))
    return pl.pallas_call(
        _inproj_kernel,
        grid=(bsz, t // tile),
        in_specs=[row(d), mod.spec(0), mod.spec(1),
                  _const_spec(wts["w_rw"]), _const_spec(wts["w_mla"]), _const_spec(wts["w_gate"])],
        out_specs=[row(RW_COLS), row(MLA_LOW_COLS), row(2 * d)],
        out_shape=[jax.ShapeDtypeStruct((bsz, t, RW_COLS), F32),
                   jax.ShapeDtypeStruct((bsz, t, MLA_LOW_COLS), F32),
                   jax.ShapeDtypeStruct((bsz, t, 2 * d), F32)],
        compiler_params=_cparams("parallel", "arbitrary"),
        name="inproj",
    )(x3, mod.arr, mod.arr, wts["w_rw"], wts["w_mla"], wts["w_gate"])


RW_OUT_NAMES = ("r", "kmod", "v", "kkn", "bvec", "logw", "bonus", "g")


def _rw_elementwise(p, pp, mu, w_lora, w_gate_up, w_decay0, a0, k_k, k_a, r_k, ones_blk):
    w = RW_WIDTH
    pm = p + (pp - p) * mu
    r, k, v = pm[:, 0:w], pm[:, w:2 * w], pm[:, 2 * w:3 * w]
    xwa = pm[:, 3 * w:3 * w + DECAY_LORA + ICLR_LORA]
    xg = pm[:, 3 * w + DECAY_LORA + ICLR_LORA:]
    lane = _iota((1, DECAY_LORA + ICLR_LORA), 1)
    z = jnp.where(lane < DECAY_LORA, jnp.tanh(xwa), xwa)
    lora = _mm(z, w_lora)
    y = -(w_decay0 + lora[:, :w])
    softplus = jnp.maximum(y, 0.0) + jnp.log1p(jnp.exp(-jnp.abs(y)))
    w_log = -softplus - 0.5
    logw = -jnp.exp(w_log)
    a = jax.nn.sigmoid(a0 + lora[:, w:])
    g = _mm(jax.nn.sigmoid(xg), w_gate_up)
    kk = k * k_k
    kk = kk * lax.rsqrt(_mm_exact_rhs(kk * kk, ones_blk) + 1e-12)
    kmod = k * (1.0 + (a - 1.0) * k_a)
    bonus = _mm_exact_rhs(r * kmod * r_k, ones_blk) * v
    return r, kmod, v, -kk, kk * a, logw, bonus, g


def _rwprep_prompt_kernel(p_ref, prev_ref, s0_ref, mu_ref, wl_ref, wgu_ref, wd0_ref, a0_ref, kk_ref, ka_ref, rk_ref,
                          ones_ref, *out_refs):
    i = pl.program_id(1)
    p = p_ref[...]
    carry = jnp.where(i == 0, s0_ref[...], prev_ref[7:8, :])
    rolled = pltpu.roll(p, 1, 0)
    pp = jnp.where(_iota(p.shape, 0) == 0, carry, rolled)
    outs = _rw_elementwise(p, pp, mu_ref[...], wl_ref[...], wgu_ref[...], wd0_ref[...], a0_ref[...], kk_ref[...],
                           ka_ref[...], rk_ref[...], ones_ref[...])
    for o_ref, val in zip(out_refs, outs):
        o_ref[...] = val


def _rwprep_sample_kernel(p_ref, pp_ref, mu_ref, wl_ref, wgu_ref, wd0_ref, a0_ref, kk_ref, ka_ref, rk_ref,
                          ones_ref, *out_refs):
    outs = _rw_elementwise(p_ref[...], pp_ref[...], mu_ref[...], wl_ref[...], wgu_ref[...], wd0_ref[...], a0_ref[...],
                           kk_ref[...], ka_ref[...], rk_ref[...], ones_ref[...])
    for o_ref, val in zip(out_refs, outs):
        o_ref[...] = val


def _rw_param_list(wts):
    return [wts["mu_shift"], wts["w_lora"], wts["w_gate_up"], wts["w_decay0"], wts["a0"], wts["k_k"], wts["k_a"],
            wts["r_k"], wts["ones_blk"]]


def _rwprep_prompt(p_rw, shift0, wts, tile):
    bsz, t, _ = p_rw.shape
    params = _rw_param_list(wts)
    row = lambda w: pl.BlockSpec((None, tile, w), lambda b, i: (b, i, 0))
    prev = pl.BlockSpec((None, 8, RW_COLS), lambda b, i: (b, jnp.maximum(i * (tile // 8) - 1, 0), 0))
    s0 = pl.BlockSpec((None, 1, RW_COLS), lambda b, i: (b, 0, 0))
    return pl.pallas_call(
        _rwprep_prompt_kernel,
        grid=(bsz, t // tile),
        in_specs=[row(RW_COLS), prev, s0] + [_const_spec(a) for a in params],
        out_specs=[row(RW_WIDTH)] * 8,
        out_shape=[jax.ShapeDtypeStruct((bsz, t, RW_WIDTH), F32)] * 8,
        compiler_params=_cparams("parallel", "arbitrary"),
        name="rwprep_prompt",
    )(p_rw, p_rw, shift0, *params)


def _rwprep_sample(p_rw, p_prev, wts):
    n = p_rw.shape[0]
    params = _rw_param_list(wts)
    return pl.pallas_call(
        _rwprep_sample_kernel,
        grid=(1,),
        in_specs=[_const_spec(p_rw), _const_spec(p_prev)] + [_const_spec(a) for a in params],
        out_specs=[pl.BlockSpec((n, RW_WIDTH), lambda i: (0, 0))] * 8,
        out_shape=[jax.ShapeDtypeStruct((n, RW_WIDTH), F32)] * 8,
        compiler_params=_cparams("arbitrary"),
        name="rwprep_sample",
    )(p_rw, p_prev, *params)


def _group_norm_pair(o, ones_pair):
    mu = _mm_exact_rhs(o, ones_pair) * (1.0 / RW_HEAD_DIM)
    d = o - mu
    var = _mm_exact_rhs(d * d, ones_pair) * (1.0 / RW_HEAD_DIM)
    return d * lax.rsqrt(var + GN_EPS)


def _scan_kernel(r_ref, k_ref, v_ref, a_ref, b_ref, lw_ref, bon_ref, g_ref, lng_ref, lnb_ref, o_ref, st_ref, h_scr):
    c = pl.program_id(1)
    n_steps = pl.num_programs(1)
    n_pairs = h_scr.shape[0]
    cl = SCAN_CHUNK
    n_sub = r_ref.shape[0] // cl

    @pl.when(c == 0)
    def _():
        h_scr[...] = jnp.zeros_like(h_scr)

    row = _iota((cl, cl), 0)
    col = _iota((cl, cl), 1)
    tri_incl = row >= col
    tri_strict = row > col
    tri_ones = tri_incl.astype(BF16)
    eye = (row == col).astype(F32)
    head0 = _iota((1, LANES), 1) < RW_HEAD_DIM
    head_mask = (head0, jnp.logical_not(head0))
    head0_2 = (_iota((1, 2 * LANES), 1) & RW_HEAD_DIM) == 0
    r128 = _iota((LANES, LANES), 0)
    c128 = _iota((LANES, LANES), 1)
    same_head = (r128 < RW_HEAD_DIM) == (c128 < RW_HEAD_DIM)
    diag = r128 == c128
    ones_pair = same_head.astype(BF16)
    units = [(s, p) for s in range(n_sub) for p in range(n_pairs)]
    heads = [(s, p, h) for (s, p) in units for h in range(2)]
    rows = {s: slice(cl * s, cl * (s + 1)) for s in range(n_sub)}
    lanes = {p: slice(LANES * p, LANES * (p + 1)) for p in range(n_pairs)}
    at = lambda ref, u: ref[rows[u[0]], lanes[u[1]]]

    lw = {u: at(lw_ref, u) for u in units}
    lg = {}
    for u in units:
        l1 = lw[u].astype(BF16)
        rem = lw[u] - l1.astype(F32)
        l2 = rem.astype(BF16)
        l3 = (rem - l2.astype(F32)).astype(BF16)
        lg[u] = _mm(tri_ones, l1) + (_mm(tri_ones, l2) + _mm(tri_ones, l3))
    vv = {u: at(v_ref, u) for u in units}
    rh, ah, bh, kh, bt, kt, g_last = {}, {}, {}, {}, {}, {}, {}
    for u in units:
        bb = at(b_ref, u)
        kk = at(k_ref, u)
        lg_last = lg[u][cl - 1:cl, :]
        rh[u] = at(r_ref, u) * jnp.exp(lg[u])
        ah[u] = at(a_ref, u) * jnp.exp(lg[u] - lw[u])
        inv = jnp.exp(-lg[u])
        bh[u] = bb * inv
        kh[u] = kk * inv
        tail = jnp.exp(lg_last - lg[u])
        bt[u] = bb * tail
        kt[u] = kk * tail
        g_last[u] = jnp.exp(lg_last)

    xb, xk = {}, {}
    for (s, p, h) in heads:
        ar_h = jnp.where(head_mask[h], jnp.concatenate([ah[s, p], rh[s, p]], axis=0), 0.0)
        xb[s, p, h] = _mm_t(ar_h, bh[s, p])
        xk[s, p, h] = _mm_t(ar_h, kh[s, p])
    l_ab = {k: jnp.where(tri_strict, x[:cl], 0.0) for k, x in xb.items()}
    m_rb = {k: jnp.where(tri_incl, x[cl:], 0.0) for k, x in xb.items()}
    l_ak = {k: jnp.where(tri_strict, x[:cl], 0.0) for k, x in xk.items()}
    m_rk = {k: jnp.where(tri_incl, x[cl:], 0.0) for k, x in xk.items()}
    lv_h = {k: _mm(l_ak[k], vv[k[:2]]) for k in heads}
    mrkv_h = {k: _mm(m_rk[k], vv[k[:2]]) for k in heads}
    ktv = {u: _mm_tl(kt[u], vv[u]) for u in units}

    base_bits = 3
    blk = lambda bits: (row >> bits) == (col >> bits)
    sum_half = _iota((1, 2 * cl), 1) >= cl
    x = {k: jnp.concatenate([jnp.where(blk(base_bits), l_ab[k], 0.0), eye], axis=1) for k in heads}
    for _ in range(base_bits):
        nxt = {}
        for k in heads:
            xh, xl = _split(x[k])
            ph, plo = xh[:, :cl], xl[:, :cl]
            nxt[k] = (_mm(ph, xh) + (_mm(ph, xl) + _mm(plo, xh))) + jnp.where(sum_half, x[k], 0.0)
        x = nxt
    t_inv = {k: x[k][:, cl:] for k in heads}
    for bits in range(base_bits, int(math.log2(cl))):
        lower_left = jnp.logical_and(blk(bits + 1), jnp.logical_not(blk(bits)))
        y = {k: _mm_merge(jnp.where(lower_left, l_ab[k], 0.0), t_inv[k]) for k in heads}
        t_inv = {k: t_inv[k] + _mm_merge(t_inv[k], y[k]) for k in heads}

    pick = lambda d, u: jnp.where(head0, d[u + (0,)], d[u + (1,)])
    pick2 = lambda d, u: jnp.where(head0_2, d[u + (0,)], d[u + (1,)])
    z = {u: jnp.concatenate([ah[u], pick(lv_h, u)], axis=1) for u in units}
    tz = {k: _mm(t_inv[k], z[k[:2]]) for k in heads}
    w12 = {u: pick2(tz, u) for u in units}
    q12_h = {k: _mm(m_rb[k], w12[k[:2]]) for k in heads}
    g12 = {u: _mm_tl(bt[u], w12[u]) for u in units}
    q1, q2, g1, g2 = {}, {}, {}, {}
    for u in units:
        q12 = pick2(q12_h, u)
        q1[u] = rh[u] + q12[:, :LANES]
        q2[u] = q12[:, LANES:] + pick(mrkv_h, u)
        g1[u] = jnp.where(same_head, g12[u][:, :LANES], 0.0) + jnp.where(diag, g_last[u], 0.0)
        g2[u] = jnp.where(same_head, g12[u][:, LANES:] + ktv[u], 0.0)

    state = {p: h_scr[p] for p in range(n_pairs)}
    outs = {}
    for s in range(n_sub):
        for p in range(n_pairs):
            outs[s, p] = _mm_state(q1[s, p], state[p]) + q2[s, p]
        state = {p: _mm_state(g1[s, p], state[p]) + g2[s, p] for p in range(n_pairs)}
    for p in range(n_pairs):
        h_scr[p] = state[p]

    for u in units:
        y = _group_norm_pair(outs[u], ones_pair) * lng_ref[:, lanes[u[1]]] + lnb_ref[:, lanes[u[1]]]
        o_ref[rows[u[0]], lanes[u[1]]] = (y + at(bon_ref, u)) * at(g_ref, u)

    @pl.when(c == n_steps - 1)
    def _():
        st_ref[...] = h_scr[...]


def _rw_scan(rw, wts):
    bsz, t, w = rw["r"].shape
    n_pairs = w // LANES
    step_rows = SCAN_CHUNK * SCAN_SUBCHUNKS
    assert t % step_rows == 0, (t, step_rows)
    blk = pl.BlockSpec((None, step_rows, w), lambda b, c: (b, c, 0))
    ins = [rw[n] for n in ("r", "kmod", "v", "kkn", "bvec", "logw", "bonus", "g")]
    return pl.pallas_call(
        _scan_kernel,
        grid=(bsz, t // step_rows),
        in_specs=[blk] * 8 + [_const_spec(wts["lnx_g"]), _const_spec(wts["lnx_b"])],
        out_specs=[blk, pl.BlockSpec((None, n_pairs, LANES, LANES), lambda b, c: (b, 0, 0, 0))],
        out_shape=[jax.ShapeDtypeStruct((bsz, t, w), F32),
                   jax.ShapeDtypeStruct((bsz, n_pairs, LANES, LANES), F32)],
        scratch_shapes=[pltpu.VMEM((n_pairs, LANES, LANES), F32)],
        compiler_params=_cparams("parallel", "arbitrary"),
        name="rw_scan",
    )(*ins, wts["lnx_g"], wts["lnx_b"])


def _unpack_state(st):
    hd = RW_HEAD_DIM
    blocks = [st[:, p, j * hd:(j + 1) * hd, j * hd:(j + 1) * hd] for p in range(st.shape[1]) for j in range(2)]
    return jnp.swapaxes(jnp.stack(blocks, axis=1), -1, -2)


def _rwstep_kernel(s_ref, r_ref, k_ref, v_ref, a_ref, b_ref, lw_ref, bon_ref, g_ref, lng_ref, lnb_ref, so_ref, o_ref,
                   acc_scr, vt_scr):
    hd = RW_HEAD_DIM
    vt_scr[...] = v_ref[...].T
    decay2 = jnp.exp(lw_ref[...]).T
    kkn2 = a_ref[...].T
    kka2 = b_ref[...].T
    kmod2 = k_ref[...].T
    r2 = r_ref[...].T
    normed = []
    for h in range(2):
        ch = slice(hd * h, hd * (h + 1))
        decay, kkn, kka, kmod, r = decay2[ch], kkn2[ch], kka2[ch], kmod2[ch], r2[ch]

        def body(i, carry):
            s = s_ref[h, i]
            sa = jnp.sum(s * kkn, axis=0, keepdims=True)
            s_new = s * decay + sa * kka + vt_scr[pl.ds(hd * h + i, 1), :] * kmod
            so_ref[h, i] = s_new
            acc_scr[pl.ds(hd * h + i, 1), :] = jnp.sum(s_new * r, axis=0, keepdims=True)
            return carry

        lax.fori_loop(0, hd, body, 0, unroll=8)
        o = acc_scr[ch, :]
        mu = jnp.mean(o, axis=0, keepdims=True)
        d = o - mu
        var = jnp.mean(d * d, axis=0, keepdims=True)
        normed.append(d * lax.rsqrt(var + GN_EPS))
    y = jnp.concatenate(normed, axis=0).T * lng_ref[...] + lnb_ref[...]
    o_ref[...] = (y + bon_ref[...]) * g_ref[...]


def _rw_step(state_t, rw, wts):
    n = state_t.shape[-1]
    hd = RW_HEAD_DIM
    vec = pl.BlockSpec((n, 2 * hd), lambda p: (0, p))
    par = pl.BlockSpec((1, 2 * hd), lambda p: (0, p))
    st = pl.BlockSpec((2, hd, hd, n), lambda p: (p, 0, 0, 0))
    ins = [rw[nm] for nm in ("r", "kmod", "v", "kkn", "bvec", "logw", "bonus", "g")]
    return pl.pallas_call(
        _rwstep_kernel,
        grid=(RW_HEADS // 2,),
        in_specs=[st] + [vec] * 8 + [par, par],
        out_specs=[st, vec],
        out_shape=[jax.ShapeDtypeStruct(state_t.shape, F32), jax.ShapeDtypeStruct((n, RW_WIDTH), F32)],
        scratch_shapes=[pltpu.VMEM((2 * hd, n), F32), pltpu.VMEM((2 * hd, n), F32)],
        compiler_params=_cparams("parallel"),
        name="rw_step",
    )(state_t, *ins, wts["lnx_g"], wts["lnx_b"])


def _rms(x, g):
    return x * lax.rsqrt(jnp.mean(x * x, axis=-1, keepdims=True) + RMS_EPS) * g


def _mla_common(low, pos, gq, gkv, wqa, wqb, invf):
    cq = _rms(low[:, :Q_LORA], gq)
    ckv = _rms(low[:, Q_LORA:Q_LORA + KV_LORA], gkv)
    kr = low[:, Q_LORA + KV_LORA:Q_LORA + KV_LORA + LANES]
    kr_rot = low[:, Q_LORA + KV_LORA + LANES:]
    ang = pos * invf
    cos = jnp.cos(ang)
    sin = jnp.sin(ang)
    cos8 = jnp.concatenate([cos] * MLA_HEADS, axis=1)
    sin8 = jnp.concatenate([sin] * MLA_HEADS, axis=1)
    q = (_mm(cq, wqa) * cos8 + _mm(cq, wqb) * sin8) * ATTN_SCALE
    k_rope = kr * pltpu.roll(cos, LANES - QK_NOPE, 1) + kr_rot * pltpu.roll(sin, LANES - QK_NOPE, 1)
    return q, ckv, k_rope


def _mlaprep_prompt_kernel(low_ref, gq_ref, gkv_ref, wqa_ref, wqb_ref, wuk_ref, wuv_ref, invf_ref,
                           q_ref, k_ref, v_ref, ckv_ref, kr_ref):
    i = pl.program_id(1)
    rows = low_ref.shape[0]
    pos = (i * rows + _iota((rows, 1), 0)).astype(F32)
    q, ckv, k_rope = _mla_common(low_ref[...], pos, gq_ref[...], gkv_ref[...], wqa_ref[...], wqb_ref[...],
                                 invf_ref[...])
    q_ref[...] = q.astype(BF16)
    ckv_ref[...] = ckv
    kr_ref[...] = k_rope[:, :QK_ROPE]
    k_slot = pltpu.roll(k_rope, QK_NOPE, 1)
    k_ref[...] = (_mm(ckv, wuk_ref[...]) + jnp.concatenate([k_slot] * MLA_HEADS, axis=1)).astype(BF16)
    v_ref[...] = _mm(ckv, wuv_ref[...]).astype(BF16)


def _mlaprep_prompt(low, wts, tile):
    bsz, t, _ = low.shape
    params = [wts["g_qnorm"], wts["g_kvnorm"], wts["w_qa"], wts["w_qb"], wts["w_uk_pad"], wts["w_uv"], wts["invf"]]
    row = lambda w: pl.BlockSpec((None, tile, w), lambda b, i: (b, i, 0))
    slots = MLA_HEADS * HEAD_SLOT
    return pl.pallas_call(
        _mlaprep_prompt_kernel,
        grid=(bsz, t // tile),
        in_specs=[row(MLA_LOW_COLS)] + [_const_spec(a) for a in params],
        out_specs=[row(slots), row(slots), row(MLA_HEADS * V_HEAD), row(KV_LORA), row(QK_ROPE)],
        out_shape=[jax.ShapeDtypeStruct((bsz, t, slots), BF16), jax.ShapeDtypeStruct((bsz, t, slots), BF16),
                   jax.ShapeDtypeStruct((bsz, t, MLA_HEADS * V_HEAD), BF16),
                   jax.ShapeDtypeStruct((bsz, t, KV_LORA), F32), jax.ShapeDtypeStruct((bsz, t, QK_ROPE), F32)],
        compiler_params=_cparams("parallel", "arbitrary"),
        name="mlaprep_prompt",
    )(low, *params)


def _mlaprep_sample_kernel(low_ref, gq_ref, gkv_ref, wqa_ref, wqb_ref, wukt_ref, invf_ref, pos_ref,
                           qlat_ref, qr_ref, ckv_ref, kr_ref):
    q, ckv, k_rope = _mla_common(low_ref[...], pos_ref[...], gq_ref[...], gkv_ref[...], wqa_ref[...], wqb_ref[...],
                                 invf_ref[...])
    ckv_ref[...] = ckv
    kr_ref[...] = k_rope[:, :QK_ROPE]
    rope_lanes = _iota((1, HEAD_SLOT), 1) < QK_ROPE
    for h in range(MLA_HEADS):
        slot = q[:, HEAD_SLOT * h:HEAD_SLOT * (h + 1)]
        qlat_ref[:, KV_LORA * h:KV_LORA * (h + 1)] = _mm(slot, wukt_ref[h])
        qr_ref[:, HEAD_SLOT * h:HEAD_SLOT * (h + 1)] = jnp.where(rope_lanes, pltpu.roll(slot, LANES - QK_NOPE, 1), 0.0)


def _mlaprep_sample(low, pos, wts):
    n = low.shape[0]
    params = [wts["g_qnorm"], wts["g_kvnorm"], wts["w_qa"], wts["w_qb"], wts["w_ukt_pad"], wts["invf"], pos]
    full = lambda shape: pl.BlockSpec(shape, lambda i: (0,) * len(shape))
    return pl.pallas_call(
        _mlaprep_sample_kernel,
        grid=(1,),
        in_specs=[_const_spec(low)] + [_const_spec(a) for a in params],
        out_specs=[full((n, MLA_HEADS * KV_LORA)), full((n, MLA_HEADS * HEAD_SLOT)), full((n, KV_LORA)),
                   full((n, QK_ROPE))],
        out_shape=[jax.ShapeDtypeStruct((n, MLA_HEADS * KV_LORA), F32),
                   jax.ShapeDtypeStruct((n, MLA_HEADS * HEAD_SLOT), F32),
                   jax.ShapeDtypeStruct((n, KV_LORA), F32), jax.ShapeDtypeStruct((n, QK_ROPE), F32)],
        compiler_params=_cparams("arbitrary"),
        name="mlaprep_sample",
    )(low, *params)


def _attn_kernel(q_ref, k_ref, v_ref, o_ref, m_scr, l_scr, acc_scr):
    qi = pl.program_id(2)
    ki = pl.program_id(3)
    bq = q_ref.shape[0]
    bk = k_ref.shape[0]

    @pl.when(ki == 0)
    def _():
        m_scr[...] = jnp.full(m_scr.shape, -jnp.inf, F32)
        l_scr[...] = jnp.zeros_like(l_scr)
        acc_scr[...] = jnp.zeros_like(acc_scr)

    @pl.when(ki * bk <= qi * bq + (bq - 1))
    def _():
        qpos = qi * bq + _iota((bq, bk), 0)
        kpos = ki * bk + _iota((bq, bk), 1)
        visible = kpos <= qpos
        v = v_ref[...]
        for h in range(2):
            q = q_ref[:, HEAD_SLOT * h:HEAD_SLOT * (h + 1)]
            k = k_ref[:, HEAD_SLOT * h:HEAD_SLOT * (h + 1)]
            s = jnp.where(visible, _mm_t(q, k), -jnp.inf)
            m_prev = m_scr[h]
            m_new = jnp.maximum(m_prev, jnp.max(s, axis=1, keepdims=True))
            p = jnp.exp(s - m_new)
            alpha = jnp.exp(m_prev - m_new)
            l_scr[h] = alpha * l_scr[h] + jnp.sum(p, axis=1, keepdims=True)
            acc_scr[h] = alpha * acc_scr[h] + _mm(p, v)
            m_scr[h] = m_new

    @pl.when(ki == pl.num_programs(3) - 1)
    def _():
        head0 = _iota((1, LANES), 1) < V_HEAD
        o_ref[...] = jnp.where(head0, acc_scr[0] / l_scr[0], acc_scr[1] / l_scr[1])


def _attention_prompt(q, k, v, bq, bk):
    bsz, t, _ = q.shape
    n_pairs = MLA_HEADS // 2
    last_k = lambda qi: (qi * bq + (bq - 1)) // bk
    return pl.pallas_call(
        _attn_kernel,
        grid=(bsz, n_pairs, t // bq, t // bk),
        in_specs=[pl.BlockSpec((None, bq, 2 * HEAD_SLOT), lambda b, p, qi, ki: (b, qi, p)),
                  pl.BlockSpec((None, bk, 2 * HEAD_SLOT), lambda b, p, qi, ki: (b, jnp.minimum(ki, last_k(qi)), p)),
                  pl.BlockSpec((None, bk, LANES), lambda b, p, qi, ki: (b, jnp.minimum(ki, last_k(qi)), p))],
        out_specs=pl.BlockSpec((None, bq, LANES), lambda b, p, qi, ki: (b, qi, p)),
        out_shape=jax.ShapeDtypeStruct((bsz, t, MLA_HEADS * V_HEAD), F32),
        scratch_shapes=[pltpu.VMEM((2, bq, 1), F32), pltpu.VMEM((2, bq, 1), F32), pltpu.VMEM((2, bq, LANES), F32)],
        compiler_params=_cparams("parallel", "parallel", "parallel", "arbitrary"),
        name="attn_prompt",
    )(q, k, v)


def _sample_attn_kernel(pt_ref, qlat_ref, qr_ref, cn_ref, kn_ref, ckv_hbm, kr_hbm, o_ref, kbuf, rbuf, s_scr, sem):
    b = pl.program_id(0)
    n_seq = pl.num_programs(0)
    n_pages = kbuf.shape[1]
    slot = lax.rem(b, 2)

    def page_copies(seq, slt):
        out = []
        for pg in range(n_pages):
            page = pt_ref[seq, pg]
            out.append(pltpu.make_async_copy(ckv_hbm.at[page], kbuf.at[slt, pg], sem.at[slt, 0]))
            out.append(pltpu.make_async_copy(kr_hbm.at[page], rbuf.at[slt, pg], sem.at[slt, 1]))
        return out

    @pl.when(b == 0)
    def _():
        for cp in page_copies(0, 0):
            cp.start()

    @pl.when(b + 1 < n_seq)
    def _():
        for cp in page_copies(b + 1, 1 - slot):
            cp.start()

    for cp in page_copies(b, slot):
        cp.wait()

    ql = qlat_ref[...]
    qr = qr_ref[:, :QK_ROPE]
    n_chunks = n_pages // 2
    rows = 2 * PAGE_SIZE

    def score(i, carry):
        kc = kbuf[slot, pl.ds(2 * i, 2)].reshape(rows, KV_LORA)
        rc = jnp.concatenate([rbuf[slot, 2 * i], rbuf[slot, 2 * i + 1]], axis=1)
        s_scr[i] = _mm_t(ql, kc) + jnp.dot(qr, rc, preferred_element_type=F32)
        return carry

    lax.fori_loop(0, n_chunks, score, 0, unroll=SAMPLE_ATTN_UNROLL)
    cn = cn_ref[...]
    kn = kn_ref[...]
    s_new = jnp.sum(ql * cn, axis=-1, keepdims=True) + jnp.sum(qr * kn, axis=-1, keepdims=True)
    s_all = s_scr[...]
    m = jnp.maximum(jnp.max(jnp.max(s_all, axis=0), axis=-1, keepdims=True), s_new)
    p_all = jnp.exp(s_all - m)
    p_new = jnp.exp(s_new - m)
    denom = jnp.sum(jnp.sum(p_all, axis=0), axis=-1, keepdims=True) + p_new
    s_scr[...] = p_all

    def accumulate(i, acc):
        kc = kbuf[slot, pl.ds(2 * i, 2)].reshape(rows, KV_LORA)
        return acc + _mm(s_scr[i], kc)

    acc = lax.fori_loop(0, n_chunks, accumulate, jnp.zeros((ql.shape[0], KV_LORA), F32), unroll=SAMPLE_ATTN_UNROLL)
    o_ref[...] = (acc + p_new * cn) / denom


def _attention_sample(page_table, qlat, qr, ckv_new, kr_new, cache_ckv, cache_kr):
    n, n_pages = page_table.shape
    grid_spec = pltpu.PrefetchScalarGridSpec(
        num_scalar_prefetch=1,
        grid=(n,),
        in_specs=[pl.BlockSpec((None, MLA_HEADS, KV_LORA), lambda b, pt: (b, 0, 0)),
                  pl.BlockSpec((None, MLA_HEADS, HEAD_SLOT), lambda b, pt: (b, 0, 0)),
                  pl.BlockSpec((None, 1, KV_LORA), lambda b, pt: (b, 0, 0)),
                  pl.BlockSpec((None, 1, QK_ROPE), lambda b, pt: (b, 0, 0)),
                  pl.BlockSpec(memory_space=pl.ANY),
                  pl.BlockSpec(memory_space=pl.ANY)],
        out_specs=pl.BlockSpec((None, MLA_HEADS, KV_LORA), lambda b, pt: (b, 0, 0)),
        scratch_shapes=[pltpu.VMEM((2, n_pages, PAGE_SIZE, KV_LORA), F32),
                        pltpu.VMEM((2, n_pages, QK_ROPE, PAGE_SIZE), F32),
                        pltpu.VMEM((n_pages // 2, MLA_HEADS, 2 * PAGE_SIZE), F32),
                        pltpu.SemaphoreType.DMA((2, 2))],
    )
    return pl.pallas_call(
        _sample_attn_kernel,
        grid_spec=grid_spec,
        out_shape=jax.ShapeDtypeStruct((n, MLA_HEADS, KV_LORA), F32),
        compiler_params=_cparams("arbitrary"),
        name="attn_sample",
    )(page_table, qlat.reshape(n, MLA_HEADS, KV_LORA), qr.reshape(n, MLA_HEADS, HEAD_SLOT),
      ckv_new.reshape(n, 1, KV_LORA), kr_new.reshape(n, 1, QK_ROPE), cache_ckv, cache_kr)


def _uv_kernel(ol_ref, wuv_ref, o_ref):
    outs = [_mm(ol_ref[:, KV_LORA * h:KV_LORA * (h + 1)], wuv_ref[:, V_HEAD * h:V_HEAD * (h + 1)])
            for h in range(MLA_HEADS)]
    o_ref[...] = jnp.concatenate(outs, axis=1)


def _latent_to_heads(o_lat, w_uv):
    n = o_lat.shape[0]
    o_lat = o_lat.reshape(n, MLA_HEADS * KV_LORA)
    return pl.pallas_call(
        _uv_kernel,
        grid=(1,),
        in_specs=[_const_spec(o_lat), _const_spec(w_uv)],
        out_specs=pl.BlockSpec((n, MLA_HEADS * V_HEAD), lambda i: (0, 0)),
        out_shape=jax.ShapeDtypeStruct((n, MLA_HEADS * V_HEAD), F32),
        compiler_params=_cparams("arbitrary"),
        name="latent_to_heads",
    )(o_lat, w_uv)


def _layernorm(z, g, b):
    mu = jnp.mean(z, axis=-1, keepdims=True)
    d = z - mu
    var = jnp.mean(d * d, axis=-1, keepdims=True)
    return d * lax.rsqrt(var + LN_EPS) * g + b


def _first_max(x, idx, sentinel):
    mx = jnp.max(x, axis=0, keepdims=True)
    first = jnp.min(jnp.where(x == mx, idx, sentinel), axis=0, keepdims=True)
    return idx == first


def _route(scores, bias):
    n_tok = scores.shape[1]
    per_group = N_EXPERTS // N_GROUPS
    sb = scores + bias
    sb3 = sb.reshape(N_GROUPS, per_group, n_tok)
    member = _iota(sb3.shape, 1)
    m1 = jnp.max(sb3, axis=1, keepdims=True)
    first = jnp.min(jnp.where(sb3 == m1, member, per_group), axis=1, keepdims=True)
    m2 = jnp.max(jnp.where(member == first, -jnp.inf, sb3), axis=1, keepdims=True)
    g_score = (m1 + m2).reshape(N_GROUPS, n_tok)
    g_idx = _iota(g_score.shape, 0)
    g_sel = jnp.zeros(g_score.shape, F32)
    for _ in range(TOPK_GROUPS):
        hit = _first_max(g_score, g_idx, N_GROUPS)
        g_sel = jnp.where(hit, 1.0, g_sel)
        g_score = jnp.where(hit, -jnp.inf, g_score)
    e_mask = jnp.broadcast_to(g_sel.reshape(N_GROUPS, 1, n_tok), sb3.shape).reshape(N_EXPERTS, n_tok) > 0.5
    cur = jnp.where(e_mask, sb, -jnp.inf)
    e_idx = _iota(cur.shape, 0)
    sel = jnp.zeros(cur.shape, F32)
    for _ in range(TOP_K):
        hit = _first_max(cur, e_idx, N_EXPERTS)
        sel = jnp.where(hit, 1.0, sel)
        cur = jnp.where(hit, -jnp.inf, cur)
    picked = jnp.where(sel > 0.5, scores, 0.0)
    return picked / jnp.sum(picked, axis=0, keepdims=True) * ROUTED_SCALE


def _merge_kernel(x_ref, oa_ref, ob_ref, gs_ref, gt1_ref, sh2_ref, sc2_ref, wba_ref, wbb_ref, wout_ref, g1_ref, b1_ref,
                  wrt_ref, br_ref, x1_ref, h2_ref, cw_ref):
    d = D_MODEL
    ya = _mm(oa_ref[...], wba_ref[...])
    yb = _mm(ob_ref[...], wbb_ref[...])
    merged = gs_ref[:, :d] * ya + gs_ref[:, d:] * yb
    z = DN_ALPHA * x_ref[...] + gt1_ref[...] * _mm(merged, wout_ref[...])
    x1 = _layernorm(z, g1_ref[...], b1_ref[...])
    x1_ref[...] = x1
    h2 = x1 * (1 + sc2_ref[...]) + sh2_ref[...]
    h2_ref[...] = h2.astype(BF16)
    scores = jax.nn.sigmoid(_mm3_t(wrt_ref[...], h2))
    cw_ref[...] = _route(scores, br_ref[...]).T


def _merge(x3, oa, ob, gs, mod, wts, tile):
    bsz, t, d = x3.shape
    params = [wts["w_branch_a"], wts["w_branch_b"], wts["w_out"], wts["ln1_g"], wts["ln1_b"], wts["w_router_t"],
              wts["b_router"]]
    row = lambda w: pl.BlockSpec((None, tile, w), lambda b, i: (b, i, 0))
    return pl.pallas_call(
        _merge_kernel,
        grid=(bsz, t // tile),
        in_specs=[row(d), row(RW_WIDTH), row(MLA_HEADS * V_HEAD), row(2 * d), mod.spec(2), mod.spec(3), mod.spec(4)]
        + [_const_spec(a) for a in params],
        out_specs=[row(d), row(d), row(N_EXPERTS)],
        out_shape=[jax.ShapeDtypeStruct((bsz, t, d), F32), jax.ShapeDtypeStruct((bsz, t, d), BF16),
                   jax.ShapeDtypeStruct((bsz, t, N_EXPERTS), F32)],
        compiler_params=_cparams("parallel", "arbitrary"),
        name="merge",
    )(x3, oa, ob, gs, mod.arr, mod.arr, mod.arr, *params)


def _experts_kernel(h_ref, cw_ref, wg_ref, wu_ref, wd_ref, o_ref):
    e = pl.program_id(1)

    @pl.when(e == 0)
    def _():
        o_ref[...] = jnp.zeros_like(o_ref)

    x = h_ref[...]
    act = _silu(jnp.dot(x, wg_ref[...], preferred_element_type=F32)) * jnp.dot(x, wu_ref[...],
                                                                             preferred_element_type=F32)
    lane = _iota((1, N_EXPERTS), 1)
    w_col = jnp.sum(jnp.where(lane == e, cw_ref[...], 0.0), axis=1, keepdims=True)
    o_ref[...] += _mm(act * w_col, wd_ref[...])


def _experts(h2, cw, wts, tile):
    n, d = h2.shape
    return pl.pallas_call(
        _experts_kernel,
        grid=(n // tile, N_EXPERTS),
        in_specs=[pl.BlockSpec((tile, d), lambda i, e: (i, 0)),
                  pl.BlockSpec((tile, N_EXPERTS), lambda i, e: (i, 0)),
                  pl.BlockSpec((None, d, EXPERT_FF), lambda i, e: (e, 0, 0)),
                  pl.BlockSpec((None, d, EXPERT_FF), lambda i, e: (e, 0, 0)),
                  pl.BlockSpec((None, EXPERT_FF, d), lambda i, e: (e, 0, 0))],
        out_specs=pl.BlockSpec((tile, d), lambda i, e: (i, 0)),
        out_shape=jax.ShapeDtypeStruct((n, d), F32),
        compiler_params=_cparams("parallel", "arbitrary"),
        name="experts",
    )(h2, cw, wts["w_exp_gate"], wts["w_exp_up"], wts["w_exp_down"])


def _final_kernel(x1_ref, h2_ref, routed_ref, gt2_ref, wsg_ref, wsu_ref, wsd_ref, g2_ref, b2_ref, y_ref):
    h2 = h2_ref[...]
    act = _silu(jnp.dot(h2, wsg_ref[...], preferred_element_type=F32)) * jnp.dot(h2, wsu_ref[...],
                                                                               preferred_element_type=F32)
    ffn = routed_ref[...] + _mm(act, wsd_ref[...])
    y_ref[...] = _layernorm(DN_ALPHA * x1_ref[...] + gt2_ref[...] * ffn, g2_ref[...], b2_ref[...])


def _final(x1, h2, routed, mod, wts, tile):
    bsz, t, d = x1.shape
    params = [wts["w_sh_gate"], wts["w_sh_up"], wts["w_sh_down"], wts["ln2_g"], wts["ln2_b"]]
    row = lambda w: pl.BlockSpec((None, tile, w), lambda b, i: (b, i, 0))
    return pl.pallas_call(
        _final_kernel,
        grid=(bsz, t // tile),
        in_specs=[row(d), row(d), row(d), mod.spec(5)] + [_const_spec(a) for a in params],
        out_specs=row(d),
        out_shape=jax.ShapeDtypeStruct((bsz, t, d), F32),
        compiler_params=_cparams("parallel", "arbitrary"),
        name="final",
    )(x1, h2, routed, mod.arr, *params)


def _rope_rotation_columns(w):
    half = QK_ROPE // 2
    return jnp.concatenate([-w[..., half:], w[..., :half]], axis=-1)


def _prepare_weights(P):
    d = D_MODEL
    w = {}
    w_in = P["w_in"]
    o1 = RW_COLS
    o2 = o1 + Q_LORA
    o3 = o2 + KV_LORA
    o4 = o3 + QK_ROPE
    w_kr = w_in[:, o3:o4]
    pad = jnp.zeros((d, LANES - QK_ROPE), F32)
    w["w_rw"] = w_in[:, :o1].astype(BF16)
    w["w_mla"] = jnp.concatenate([w_in[:, o1:o3], w_kr, pad, _rope_rotation_columns(w_kr), pad], axis=1).astype(BF16)
    w["w_gate"] = w_in[:, o4:].astype(BF16)

    row = lambda v: v.reshape(1, -1)
    w["mu_shift"] = row(P["mu_shift"])
    zl = jnp.zeros((DECAY_LORA, RW_WIDTH), F32)
    w["w_lora"] = jnp.concatenate([jnp.concatenate([P["w_decay_up"], zl], axis=1),
                                   jnp.concatenate([zl, P["w_iclr_up"]], axis=1)], axis=0).astype(BF16)
    w["w_gate_up"] = P["w_gate_up"].astype(BF16)
    for nm in ("w_decay0", "a0", "k_k", "k_a", "r_k", "lnx_g", "lnx_b", "g_qnorm", "g_kvnorm", "ln1_g", "ln1_b",
               "ln2_g", "ln2_b"):
        w[nm] = row(P[nm])
    head_of = np.arange(RW_WIDTH) // RW_HEAD_DIM
    w["ones_blk"] = jnp.asarray(head_of[:, None] == head_of[None, :], BF16)

    wq = P["w_uq"].reshape(Q_LORA, MLA_HEADS, QK_NOPE + QK_ROPE)
    zq = jnp.zeros((Q_LORA, MLA_HEADS, HEAD_SLOT - QK_NOPE - QK_ROPE), F32)
    w["w_qa"] = jnp.concatenate([wq, zq], axis=-1).reshape(Q_LORA, MLA_HEADS * HEAD_SLOT).astype(BF16)
    w["w_qb"] = jnp.concatenate([jnp.zeros((Q_LORA, MLA_HEADS, QK_NOPE), F32), _rope_rotation_columns(wq[..., QK_NOPE:]), zq],
                                axis=-1).reshape(Q_LORA, MLA_HEADS * HEAD_SLOT).astype(BF16)
    wuk = P["w_uk"].reshape(KV_LORA, MLA_HEADS, QK_NOPE)
    zk = jnp.zeros((KV_LORA, MLA_HEADS, HEAD_SLOT - QK_NOPE), F32)
    w["w_uk_pad"] = jnp.concatenate([wuk, zk], axis=-1).reshape(KV_LORA, MLA_HEADS * HEAD_SLOT).astype(BF16)
    w["w_ukt_pad"] = jnp.transpose(jnp.concatenate([wuk, zk], axis=-1), (1, 2, 0)).astype(BF16)
    w["w_uv"] = P["w_uv"].astype(BF16)
    half = QK_ROPE // 2
    inv = ROPE_THETA ** (-jnp.arange(half, dtype=F32) / half)
    w["invf"] = jnp.concatenate([jnp.zeros((QK_NOPE,), F32), inv, inv,
                                 jnp.zeros((HEAD_SLOT - QK_NOPE - QK_ROPE,), F32)]).reshape(1, HEAD_SLOT)

    for nm in ("w_branch_a", "w_branch_b", "w_out", "w_exp_gate", "w_exp_up", "w_exp_down", "w_sh_gate", "w_sh_up",
               "w_sh_down"):
        w[nm] = P[nm].astype(BF16)
    w["w_router_t"] = P["w_router"].T
    w["b_router"] = P["b_router"].reshape(N_EXPERTS, 1)
    return w


def _row_tile(t, cap):
    tile = min(t, cap)
    assert t % tile == 0 and tile % 16 == 0, (t, tile)
    return tile


def _layer_tail(x3, oa, ob, gs, mod, wts, tile, moe_tile):
    bsz, t, d = x3.shape
    x1, h2, cw = _merge(x3, oa, ob, gs, mod, wts, tile)
    routed = _experts(h2.reshape(bsz * t, d), cw.reshape(bsz * t, N_EXPERTS), wts, moe_tile).reshape(bsz, t, d)
    return _final(x1, h2, routed, mod, wts, tile)


def _layer_prompt(x, mod_rows, wts):
    bsz, t, d = x.shape
    tile = _row_tile(t, 256)
    mod = _Mod(mod_rows.reshape(bsz * 6, 1, d), per_token=False, tile=tile)
    p_rw, low, gs = _inproj(x, mod, wts, tile)
    rw = dict(zip(RW_OUT_NAMES, _rwprep_prompt(p_rw, jnp.zeros((bsz, 1, RW_COLS), F32), wts, tile)))
    oa, state = _rw_scan(rw, wts)
    q, k, v, ckv, k_rope = _mlaprep_prompt(low, wts, tile)
    ob = _attention_prompt(q, k, v, _row_tile(t, 512), _row_tile(t, 1024))
    y = _layer_tail(x, oa, ob, gs, mod, wts, tile, _row_tile(bsz * t, 1024))
    return y, ckv, k_rope, _unpack_state(state), p_rw[:, -1]


def _layer_sample(x, mod_rows, state, shift, page_table, cache_ckv, cache_kr, wts):
    n, s_new, d = x.shape
    assert s_new == 1
    past = page_table.shape[1] * PAGE_SIZE
    x3 = x.reshape(1, n, d)
    mod = _Mod(mod_rows, per_token=True, tile=n)
    p_rw, low, gs = _inproj(x3, mod, wts, n)
    p_rw2 = p_rw.reshape(n, RW_COLS)
    rw = dict(zip(RW_OUT_NAMES, _rwprep_sample(p_rw2, shift, wts)))
    new_state_t, oa = _rw_step(jnp.transpose(state, (1, 2, 3, 0)), rw, wts)
    new_state = jnp.transpose(new_state_t, (3, 0, 1, 2))
    pos = jnp.full((n, 1), past, F32)
    qlat, qr, ckv, k_rope = _mlaprep_sample(low.reshape(n, MLA_LOW_COLS), pos, wts)
    o_lat = _attention_sample(page_table, qlat, qr, ckv, k_rope, cache_ckv, jnp.swapaxes(cache_kr, 1, 2))
    ob = _latent_to_heads(o_lat, wts["w_uv"])
    y = _layer_tail(x3, oa.reshape(1, n, RW_WIDTH), ob.reshape(1, n, MLA_HEADS * V_HEAD), gs, mod, wts, n, n)
    return (y.reshape(n, 1, d), ckv.reshape(n, 1, KV_LORA), k_rope.reshape(n, 1, QK_ROPE), new_state, p_rw2)


def kernel(x_prompt, x_sample, c_prompt, c_sample, cache_ckv, cache_krope, state_wkv, state_shift, page_table, w_ada, b_ada, w_in, mu_shift, w_decay0, w_decay_up, a0, w_iclr_up, w_gate_up, k_k, k_a, r_k, lnx_g, lnx_b, w_branch_a, g_qnorm, w_uq, g_kvnorm, w_uk, w_uv, w_branch_b, w_out, ln1_g, ln1_b, w_router, b_router, w_exp_gate, w_exp_up, w_exp_down, w_sh_gate, w_sh_up, w_sh_down, ln2_g, ln2_b):
    params = dict(w_ada=w_ada, b_ada=b_ada, w_in=w_in, mu_shift=mu_shift, w_decay0=w_decay0, w_decay_up=w_decay_up,
                  a0=a0, w_iclr_up=w_iclr_up, w_gate_up=w_gate_up, k_k=k_k, k_a=k_a, r_k=r_k, lnx_g=lnx_g,
                  lnx_b=lnx_b, w_branch_a=w_branch_a, g_qnorm=g_qnorm, w_uq=w_uq, g_kvnorm=g_kvnorm, w_uk=w_uk,
                  w_uv=w_uv, w_branch_b=w_branch_b, w_out=w_out, ln1_g=ln1_g, ln1_b=ln1_b, w_router=w_router,
                  b_router=b_router, w_exp_gate=w_exp_gate, w_exp_up=w_exp_up, w_exp_down=w_exp_down,
                  w_sh_gate=w_sh_gate, w_sh_up=w_sh_up, w_sh_down=w_sh_down, ln2_g=ln2_g, ln2_b=ln2_b)
    depth = w_in.shape[0]
    bp = x_prompt.shape[0]
    bd = x_sample.shape[0]
    n_c = bp + bd
    c_all = jnp.concatenate([c_prompt, c_sample, jnp.zeros((-n_c % 8, D_MODEL), F32)], axis=0)
    yp, ys = x_prompt, x_sample
    outs = [[] for _ in range(8)]
    for l in range(depth):
        wts = _prepare_weights({name: arr[l] for name, arr in params.items()})
        mod = _adaln_mod(c_all, params["w_ada"][l], params["b_ada"][l].reshape(1, -1))
        yp, *rest_p = _layer_prompt(yp, mod[:bp], wts)
        ys, *rest_s = _layer_sample(ys, mod[bp:n_c], state_wkv[l], state_shift[l], page_table, cache_ckv[l],
                                    cache_krope[l], wts)
        for acc, val in zip(outs, rest_p + rest_s):
            acc.append(val)
    return (yp, ys) + tuple(jnp.stack(o) for o in outs)
```

```python
import functools
import math

import numpy as np
import jax
import jax.numpy as jnp
from jax import lax
from jax.experimental import pallas as pl
from jax.experimental.pallas import tpu as pltpu

F32 = jnp.float32
BF16 = jnp.bfloat16

D_MODEL = 1024
PAGE_SIZE = 128
RW_HEADS = 8
RW_HEAD_DIM = 64
RW_WIDTH = RW_HEADS * RW_HEAD_DIM
DECAY_LORA = 64
ICLR_LORA = 64
GATE_LORA = 128
RW_COLS = 3 * RW_WIDTH + DECAY_LORA + ICLR_LORA + GATE_LORA
GN_EPS = 64e-5
MLA_HEADS = 8
QK_NOPE = 64
QK_ROPE = 32
V_HEAD = 64
Q_LORA = 384
KV_LORA = 256
ROPE_THETA = 10000.0
ATTN_SCALE = (QK_NOPE + QK_ROPE) ** -0.5
N_EXPERTS = 64
TOP_K = 8
N_GROUPS = 8
TOPK_GROUPS = 4
EXPERT_FF = 256
SHARED_FF = 256
ROUTED_SCALE = 2.5
DEPTH = 1
DN_ALPHA = (2 * DEPTH) ** 0.25
LN_EPS = 1e-5
RMS_EPS = 1e-6

LANES = 128
HEAD_SLOT = LANES
MLA_LOW_COLS = Q_LORA + KV_LORA + 2 * LANES
SCAN_CHUNK = 64
SCAN_SUBCHUNKS = 2
SAMPLE_ATTN_UNROLL = 8
MOE_ROWS = 256
VMEM_LIMIT = 56 * 1024 * 1024


def _cparams(*sem):
    return pltpu.CompilerParams(dimension_semantics=sem, vmem_limit_bytes=VMEM_LIMIT)


def _mm(a, b):
    return jnp.dot(a.astype(BF16), b.astype(BF16), preferred_element_type=F32)


def _mm_t(a, b):
    return lax.dot_general(a.astype(BF16), b.astype(BF16), (((1,), (1,)), ((), ())), preferred_element_type=F32)


def _mm_tl(a, b):
    return lax.dot_general(a.astype(BF16), b.astype(BF16), (((0,), (0,)), ((), ())), preferred_element_type=F32)


def _split(x):
    hi = x.astype(BF16)
    lo = (x - hi.astype(F32)).astype(BF16)
    return hi, lo


def _three_pass(f, a, b):
    ah, al = _split(a)
    bh, bl = _split(b)
    return f(ah, bh) + (f(ah, bl) + f(al, bh))


def _mm3(a, b):
    return _three_pass(_mm, a, b)


def _mm3_t(a, b):
    return _three_pass(_mm_t, a, b)


def _mm3_tl(a, b):
    return _three_pass(_mm_tl, a, b)


_mm_state = _mm
_mm_merge = _mm


def _mm_exact_rhs(a, b_exact):
    ah, al = _split(a)
    return _mm(ah, b_exact) + _mm(al, b_exact)


def _silu(x):
    return x * jax.nn.sigmoid(x)


def _iota(shape, dim):
    return lax.broadcasted_iota(jnp.int32, shape, dim)


def _mod_kernel(c_ref, w_ref, b_ref, o_ref):
    o_ref[...] = _mm(_silu(c_ref[...]), w_ref[...]) + b_ref[...]


def _adaln_mod(c_all, w_ada, b_ada):
    n = c_all.shape[0]
    d = D_MODEL
    return pl.pallas_call(
        _mod_kernel,
        grid=(6,),
        in_specs=[pl.BlockSpec((n, d), lambda j: (0, 0)),
                  pl.BlockSpec((d, d), lambda j: (0, j)),
                  pl.BlockSpec((1, d), lambda j: (0, j))],
        out_specs=pl.BlockSpec((n, d), lambda j: (0, j)),
        out_shape=jax.ShapeDtypeStruct((n, 6 * d), F32),
        compiler_params=_cparams("arbitrary"),
        name="adaln_mod",
    )(c_all, w_ada, b_ada)


class _Mod:
    def __init__(self, arr, per_token, tile):
        self.arr = arr
        self.per_token = per_token
        self.tile = tile

    def spec(self, j):
        if self.per_token:
            return pl.BlockSpec((self.tile, D_MODEL), lambda b, i, *_: (i, j))
        return pl.BlockSpec((None, 1, D_MODEL), lambda b, i, *_: (b * 6 + j, 0, 0))


def _const_spec(arr):
    nd = arr.ndim
    return pl.BlockSpec(arr.shape, lambda *_: (0,) * nd)


def _inproj_kernel(x_ref, sh_ref, sc_ref, wrw_ref, wmla_ref, wg_ref, prw_ref, mla_ref, gs_ref):
    h = (x_ref[...] * (1 + sc_ref[...]) + sh_ref[...]).astype(BF16)
    prw_ref[...] = jnp.dot(h, wrw_ref[...], preferred_element_type=F32)
    mla_ref[...] = jnp.dot(h, wmla_ref[...], preferred_element_type=F32)
    gs_ref[...] = jax.nn.sigmoid(jnp.dot(h, wg_ref[...], preferred_element_type=F32))


def _inproj(x3, mod, wts, tile):
    bsz, t, d = x3.shape
    row = lambda w: pl.BlockSpec((None, tile, w), lambda b, i: (b, i, 0))
    return pl.pallas_call(
        _inproj_kernel,
        grid=(bsz, t // tile),
        in_specs=[row(d), mod.spec(0), mod.spec(1),
                  _const_spec(wts["w_rw"]), _const_spec(wts["w_mla"]), _const_spec(wts["w_gate"])],
        out_specs=[row(RW_COLS), row(MLA_LOW_COLS), row(2 * d)],
        out_shape=[jax.ShapeDtypeStruct((bsz, t, RW_COLS), F32),
                   jax.ShapeDtypeStruct((bsz, t, MLA_LOW_COLS), F32),
                   jax.ShapeDtypeStruct((bsz, t, 2 * d), F32)],
        compiler_params=_cparams("parallel", "arbitrary"),
        name="inproj",
    )(x3, mod.arr, mod.arr, wts["w_rw"], wts["w_mla"], wts["w_gate"])


RW_OUT_NAMES = ("r", "kmod", "v", "kkn", "bvec", "logw", "bonus", "g")


def _rw_elementwise(p, pp, mu, w_lora, w_gate_up, w_decay0, a0, k_k, k_a, r_k, ones_blk):
    w = RW_WIDTH
    pm = p + (pp - p) * mu
    r, k, v = pm[:, 0:w], pm[:, w:2 * w], pm[:, 2 * w:3 * w]
    xwa = pm[:, 3 * w:3 * w + DECAY_LORA + ICLR_LORA]
    xg = pm[:, 3 * w + DECAY_LORA + ICLR_LORA:]
    lane = _iota((1, DECAY_LORA + ICLR_LORA), 1)
    z = jnp.where(lane < DECAY_LORA, jnp.tanh(xwa), xwa)
    lora = _mm(z, w_lora)
    y = -(w_decay0 + lora[:, :w])
    softplus = jnp.maximum(y, 0.0) + jnp.log1p(jnp.exp(-jnp.abs(y)))
    w_log = -softplus - 0.5
    logw = -jnp.exp(w_log)
    a = jax.nn.sigmoid(a0 + lora[:, w:])
    g = _mm(jax.nn.sigmoid(xg), w_gate_up)
    kk = k * k_k
    kk = kk * lax.rsqrt(_mm_exact_rhs(kk * kk, ones_blk) + 1e-12)
    kmod = k * (1.0 + (a - 1.0) * k_a)
    bonus = _mm_exact_rhs(r * kmod * r_k, ones_blk) * v
    return r, kmod, v, -kk, kk * a, logw, bonus, g


def _rwprep_prompt_kernel(p_ref, prev_ref, s0_ref, mu_ref, wl_ref, wgu_ref, wd0_ref, a0_ref, kk_ref, ka_ref, rk_ref,
                          ones_ref, *out_refs):
    i = pl.program_id(1)
    p = p_ref[...]
    carry = jnp.where(i == 0, s0_ref[...], prev_ref[7:8, :])
    rolled = pltpu.roll(p, 1, 0)
    pp = jnp.where(_iota(p.shape, 0) == 0, carry, rolled)
    outs = _rw_elementwise(p, pp, mu_ref[...], wl_ref[...], wgu_ref[...], wd0_ref[...], a0_ref[...], kk_ref[...],
                           ka_ref[...], rk_ref[...], ones_ref[...])
    for o_ref, val in zip(out_refs, outs):
        o_ref[...] = val


def _rwprep_sample_kernel(p_ref, pp_ref, mu_ref, wl_ref, wgu_ref, wd0_ref, a0_ref, kk_ref, ka_ref, rk_ref,
                          ones_ref, *out_refs):
    outs = _rw_elementwise(p_ref[...], pp_ref[...], mu_ref[...], wl_ref[...], wgu_ref[...], wd0_ref[...], a0_ref[...],
                           kk_ref[...], ka_ref[...], rk_ref[...], ones_ref[...])
    for o_ref, val in zip(out_refs, outs):
        o_ref[...] = val


def _rw_param_list(wts):
    return [wts["mu_shift"], wts["w_lora"], wts["w_gate_up"], wts["w_decay0"], wts["a0"], wts["k_k"], wts["k_a"],
            wts["r_k"], wts["ones_blk"]]


def _rwprep_prompt(p_rw, shift0, wts, tile):
    bsz, t, _ = p_rw.shape
    params = _rw_param_list(wts)
    row = lambda w: pl.BlockSpec((None, tile, w), lambda b, i: (b, i, 0))
    prev = pl.BlockSpec((None, 8, RW_COLS), lambda b, i: (b, jnp.maximum(i * (tile // 8) - 1, 0), 0))
    s0 = pl.BlockSpec((None, 1, RW_COLS), lambda b, i: (b, 0, 0))
    return pl.pallas_call(
        _rwprep_prompt_kernel,
        grid=(bsz, t // tile),
        in_specs=[row(RW_COLS), prev, s0] + [_const_spec(a) for a in params],
        out_specs=[row(RW_WIDTH)] * 8,
        out_shape=[jax.ShapeDtypeStruct((bsz, t, RW_WIDTH), F32)] * 8,
        compiler_params=_cparams("parallel", "arbitrary"),
        name="rwprep_prompt",
    )(p_rw, p_rw, shift0, *params)


def _rwprep_sample(p_rw, p_prev, wts):
    n = p_rw.shape[0]
    params = _rw_param_list(wts)
    return pl.pallas_call(
        _rwprep_sample_kernel,
        grid=(1,),
        in_specs=[_const_spec(p_rw), _const_spec(p_prev)] + [_const_spec(a) for a in params],
        out_specs=[pl.BlockSpec((n, RW_WIDTH), lambda i: (0, 0))] * 8,
        out_shape=[jax.ShapeDtypeStruct((n, RW_WIDTH), F32)] * 8,
        compiler_params=_cparams("arbitrary"),
        name="rwprep_sample",
    )(p_rw, p_prev, *params)


def _group_norm_pair(o, ones_pair):
    mu = _mm_exact_rhs(o, ones_pair) * (1.0 / RW_HEAD_DIM)
    d = o - mu
    var = _mm_exact_rhs(d * d, ones_pair) * (1.0 / RW_HEAD_DIM)
    return d * lax.rsqrt(var + GN_EPS)


def _scan_kernel(r_ref, k_ref, v_ref, a_ref, b_ref, lw_ref, bon_ref, g_ref, lng_ref, lnb_ref, o_ref, st_ref, h_scr):
    c = pl.program_id(1)
    n_steps = pl.num_programs(1)
    n_pairs = h_scr.shape[0]
    cl = SCAN_CHUNK
    n_sub = r_ref.shape[0] // cl

    @pl.when(c == 0)
    def _():
        h_scr[...] = jnp.zeros_like(h_scr)

    row = _iota((cl, cl), 0)
    col = _iota((cl, cl), 1)
    tri_incl = row >= col
    tri_strict = row > col
    tri_ones = tri_incl.astype(BF16)
    eye = (row == col).astype(F32)
    head0 = _iota((1, LANES), 1) < RW_HEAD_DIM
    head_mask = (head0, jnp.logical_not(head0))
    head0_2 = (_iota((1, 2 * LANES), 1) & RW_HEAD_DIM) == 0
    r128 = _iota((LANES, LANES), 0)
    c128 = _iota((LANES, LANES), 1)
    same_head = (r128 < RW_HEAD_DIM) == (c128 < RW_HEAD_DIM)
    diag = r128 == c128
    ones_pair = same_head.astype(BF16)
    units = [(s, p) for s in range(n_sub) for p in range(n_pairs)]
    heads = [(s, p, h) for (s, p) in units for h in range(2)]
    rows = {s: slice(cl * s, cl * (s + 1)) for s in range(n_sub)}
    lanes = {p: slice(LANES * p, LANES * (p + 1)) for p in range(n_pairs)}
    at = lambda ref, u: ref[rows[u[0]], lanes[u[1]]]

    lw = {u: at(lw_ref, u) for u in units}
    lg = {}
    for u in units:
        l1 = lw[u].astype(BF16)
        rem = lw[u] - l1.astype(F32)
        l2 = rem.astype(BF16)
        l3 = (rem - l2.astype(F32)).astype(BF16)
        lg[u] = _mm(tri_ones, l1) + (_mm(tri_ones, l2) + _mm(tri_ones, l3))
    vv = {u: at(v_ref, u) for u in units}
    rh, ah, bh, kh, bt, kt, g_last = {}, {}, {}, {}, {}, {}, {}
    for u in units:
        bb = at(b_ref, u)
        kk = at(k_ref, u)
        lg_last = lg[u][cl - 1:cl, :]
        rh[u] = at(r_ref, u) * jnp.exp(lg[u])
        ah[u] = at(a_ref, u) * jnp.exp(lg[u] - lw[u])
        inv = jnp.exp(-lg[u])
        bh[u] = bb * inv
        kh[u] = kk * inv
        tail = jnp.exp(lg_last - lg[u])
        bt[u] = bb * tail
        kt[u] = kk * tail
        g_last[u] = jnp.exp(lg_last)

    xb, xk = {}, {}
    for (s, p, h) in heads:
        ar_h = jnp.where(head_mask[h], jnp.concatenate([ah[s, p], rh[s, p]], axis=0), 0.0)
        xb[s, p, h] = _mm_t(ar_h, bh[s, p])
        xk[s, p, h] = _mm_t(ar_h, kh[s, p])
    l_ab = {k: jnp.where(tri_strict, x[:cl], 0.0) for k, x in xb.items()}
    m_rb = {k: jnp.where(tri_incl, x[cl:], 0.0) for k, x in xb.items()}
    l_ak = {k: jnp.where(tri_strict, x[:cl], 0.0) for k, x in xk.items()}
    m_rk = {k: jnp.where(tri_incl, x[cl:], 0.0) for k, x in xk.items()}
    lv_h = {k: _mm(l_ak[k], vv[k[:2]]) for k in heads}
    mrkv_h = {k: _mm(m_rk[k], vv[k[:2]]) for k in heads}
    ktv = {u: _mm_tl(kt[u], vv[u]) for u in units}

    base_bits = 3
    blk = lambda bits: (row >> bits) == (col >> bits)
    sum_half = _iota((1, 2 * cl), 1) >= cl
    x = {k: jnp.concatenate([jnp.where(blk(base_bits), l_ab[k], 0.0), eye], axis=1) for k in heads}
    for _ in range(base_bits):
        nxt = {}
        for k in heads:
            xh, xl = _split(x[k])
            ph, plo = xh[:, :cl], xl[:, :cl]
            nxt[k] = (_mm(ph, xh) + (_mm(ph, xl) + _mm(plo, xh))) + jnp.where(sum_half, x[k], 0.0)
        x = nxt
    t_inv = {k: x[k][:, cl:] for k in heads}
    for bits in range(base_bits, int(math.log2(cl))):
        lower_left = jnp.logical_and(blk(bits + 1), jnp.logical_not(blk(bits)))
        y = {k: _mm_merge(jnp.where(lower_left, l_ab[k], 0.0), t_inv[k]) for k in heads}
        t_inv = {k: t_inv[k] + _mm_merge(t_inv[k], y[k]) for k in heads}

    pick = lambda d, u: jnp.where(head0, d[u + (0,)], d[u + (1,)])
    pick2 = lambda d, u: jnp.where(head0_2, d[u + (0,)], d[u + (1,)])
    z = {u: jnp.concatenate([ah[u], pick(lv_h, u)], axis=1) for u in units}
    tz = {k: _mm(t_inv[k], z[k[:2]]) for k in heads}
    w12 = {u: pick2(tz, u) for u in units}
    q12_h = {k: _mm(m_rb[k], w12[k[:2]]) for k in heads}
    g12 = {u: _mm_tl(bt[u], w12[u]) for u in units}
    q1, q2, g1, g2 = {}, {}, {}, {}
    for u in units:
        q12 = pick2(q12_h, u)
        q1[u] = rh[u] + q12[:, :LANES]
        q2[u] = q12[:, LANES:] + pick(mrkv_h, u)
        g1[u] = jnp.where(same_head, g12[u][:, :LANES], 0.0) + jnp.where(diag, g_last[u], 0.0)
        g2[u] = jnp.where(same_head, g12[u][:, LANES:] + ktv[u], 0.0)

    state = {p: h_scr[p] for p in range(n_pairs)}
    outs = {}
    for s in range(n_sub):
        for p in range(n_pairs):
            outs[s, p] = _mm_state(q1[s, p], state[p]) + q2[s, p]
        state = {p: _mm_state(g1[s, p], state[p]) + g2[s, p] for p in range(n_pairs)}
    for p in range(n_pairs):
        h_scr[p] = state[p]

    for u in units:
        y = _group_norm_pair(outs[u], ones_pair) * lng_ref[:, lanes[u[1]]] + lnb_ref[:, lanes[u[1]]]
        o_ref[rows[u[0]], lanes[u[1]]] = (y + at(bon_ref, u)) * at(g_ref, u)

    @pl.when(c == n_steps - 1)
    def _():
        st_ref[...] = h_scr[...]


def _rw_scan(rw, wts):
    bsz, t, w = rw["r"].shape
    n_pairs = w // LANES
    step_rows = SCAN_CHUNK * SCAN_SUBCHUNKS
    assert t % step_rows == 0, (t, step_rows)
    blk = pl.BlockSpec((None, step_rows, w), lambda b, c: (b, c, 0))
    ins = [rw[n] for n in ("r", "kmod", "v", "kkn", "bvec", "logw", "bonus", "g")]
    return pl.pallas_call(
        _scan_kernel,
        grid=(bsz, t // step_rows),
        in_specs=[blk] * 8 + [_const_spec(wts["lnx_g"]), _const_spec(wts["lnx_b"])],
        out_specs=[blk, pl.BlockSpec((None, n_pairs, LANES, LANES), lambda b, c: (b, 0, 0, 0))],
        out_shape=[jax.ShapeDtypeStruct((bsz, t, w), F32),
                   jax.ShapeDtypeStruct((bsz, n_pairs, LANES, LANES), F32)],
        scratch_shapes=[pltpu.VMEM((n_pairs, LANES, LANES), F32)],
        compiler_params=_cparams("parallel", "arbitrary"),
        name="rw_scan",
    )(*ins, wts["lnx_g"], wts["lnx_b"])


def _unpack_state(st):
    hd = RW_HEAD_DIM
    blocks = [st[:, p, j * hd:(j + 1) * hd, j * hd:(j + 1) * hd] for p in range(st.shape[1]) for j in range(2)]
    return jnp.swapaxes(jnp.stack(blocks, axis=1), -1, -2)


def _rwstep_kernel(s_ref, r_ref, k_ref, v_ref, a_ref, b_ref, lw_ref, bon_ref, g_ref, lng_ref, lnb_ref, so_ref, o_ref,
                   acc_scr, vt_scr):
    hd = RW_HEAD_DIM
    vt_scr[...] = v_ref[...].T
    decay2 = jnp.exp(lw_ref[...]).T
    kkn2 = a_ref[...].T
    kka2 = b_ref[...].T
    kmod2 = k_ref[...].T
    r2 = r_ref[...].T
    normed = []
    for h in range(2):
        ch = slice(hd * h, hd * (h + 1))
        decay, kkn, kka, kmod, r = decay2[ch], kkn2[ch], kka2[ch], kmod2[ch], r2[ch]

        def body(i, carry):
            s = s_ref[h, i]
            sa = jnp.sum(s * kkn, axis=0, keepdims=True)
            s_new = s * decay + sa * kka + vt_scr[pl.ds(hd * h + i, 1), :] * kmod
            so_ref[h, i] = s_new
            acc_scr[pl.ds(hd * h + i, 1), :] = jnp.sum(s_new * r, axis=0, keepdims=True)
            return carry

        lax.fori_loop(0, hd, body, 0, unroll=8)
        o = acc_scr[ch, :]
        mu = jnp.mean(o, axis=0, keepdims=True)
        d = o - mu
        var = jnp.mean(d * d, axis=0, keepdims=True)
        normed.append(d * lax.rsqrt(var + GN_EPS))
    y = jnp.concatenate(normed, axis=0).T * lng_ref[...] + lnb_ref[...]
    o_ref[...] = (y + bon_ref[...]) * g_ref[...]


def _rw_step(state_t, rw, wts):
    n = state_t.shape[-1]
    hd = RW_HEAD_DIM
    vec = pl.BlockSpec((n, 2 * hd), lambda p: (0, p))
    par = pl.BlockSpec((1, 2 * hd), lambda p: (0, p))
    st = pl.BlockSpec((2, hd, hd, n), lambda p: (p, 0, 0, 0))
    ins = [rw[nm] for nm in ("r", "kmod", "v", "kkn", "bvec", "logw", "bonus", "g")]
    return pl.pallas_call(
        _rwstep_kernel,
        grid=(RW_HEADS // 2,),
        in_specs=[st] + [vec] * 8 + [par, par],
        out_specs=[st, vec],
        out_shape=[jax.ShapeDtypeStruct(state_t.shape, F32), jax.ShapeDtypeStruct((n, RW_WIDTH), F32)],
        scratch_shapes=[pltpu.VMEM((2 * hd, n), F32), pltpu.VMEM((2 * hd, n), F32)],
        compiler_params=_cparams("parallel"),
        name="rw_step",
    )(state_t, *ins, wts["lnx_g"], wts["lnx_b"])


def _rms(x, g):
    return x * lax.rsqrt(jnp.mean(x * x, axis=-1, keepdims=True) + RMS_EPS) * g


def _mla_common(low, pos, gq, gkv, wqa, wqb, invf, q_scale):
    cq = _rms(low[:, :Q_LORA], gq)
    ckv = _rms(low[:, Q_LORA:Q_LORA + KV_LORA], gkv)
    kr = low[:, Q_LORA + KV_LORA:Q_LORA + KV_LORA + LANES]
    kr_rot = low[:, Q_LORA + KV_LORA + LANES:]
    ang = pos * invf
    cos = jnp.cos(ang)
    sin = jnp.sin(ang)
    cos8 = jnp.concatenate([cos] * MLA_HEADS, axis=1)
    sin8 = jnp.concatenate([sin] * MLA_HEADS, axis=1)
    q = (_mm(cq, wqa) * cos8 + _mm(cq, wqb) * sin8) * q_scale
    k_rope = kr * pltpu.roll(cos, LANES - QK_NOPE, 1) + kr_rot * pltpu.roll(sin, LANES - QK_NOPE, 1)
    return q, ckv, k_rope


def _mlaprep_prompt_kernel(low_ref, gq_ref, gkv_ref, wqa_ref, wqb_ref, wuk_ref, wuv_ref, invf_ref,
                           q_ref, k_ref, v_ref, ckv_ref, kr_ref):
    i = pl.program_id(1)
    rows = low_ref.shape[0]
    pos = (i * rows + _iota((rows, 1), 0)).astype(F32)
    q, ckv, k_rope = _mla_common(low_ref[...], pos, gq_ref[...], gkv_ref[...], wqa_ref[...], wqb_ref[...],
                                 invf_ref[...], ATTN_SCALE * math.log2(math.e))
    q_ref[...] = q.astype(BF16)
    ckv_ref[...] = ckv
    kr_ref[...] = k_rope[:, :QK_ROPE]
    k_slot = pltpu.roll(k_rope, QK_NOPE, 1)
    k_ref[...] = (_mm(ckv, wuk_ref[...]) + jnp.concatenate([k_slot] * MLA_HEADS, axis=1)).astype(BF16)
    v_ref[...] = _mm(ckv, wuv_ref[...]).astype(BF16)


def _mlaprep_prompt(low, wts, tile):
    bsz, t, _ = low.shape
    params = [wts["g_qnorm"], wts["g_kvnorm"], wts["w_qa"], wts["w_qb"], wts["w_uk_pad"], wts["w_uv"], wts["invf"]]
    row = lambda w: pl.BlockSpec((None, tile, w), lambda b, i: (b, i, 0))
    slots = MLA_HEADS * HEAD_SLOT
    return pl.pallas_call(
        _mlaprep_prompt_kernel,
        grid=(bsz, t // tile),
        in_specs=[row(MLA_LOW_COLS)] + [_const_spec(a) for a in params],
        out_specs=[row(slots), row(slots), row(MLA_HEADS * V_HEAD), row(KV_LORA), row(QK_ROPE)],
        out_shape=[jax.ShapeDtypeStruct((bsz, t, slots), BF16), jax.ShapeDtypeStruct((bsz, t, slots), BF16),
                   jax.ShapeDtypeStruct((bsz, t, MLA_HEADS * V_HEAD), BF16),
                   jax.ShapeDtypeStruct((bsz, t, KV_LORA), F32), jax.ShapeDtypeStruct((bsz, t, QK_ROPE), F32)],
        compiler_params=_cparams("parallel", "arbitrary"),
        name="mlaprep_prompt",
    )(low, *params)


def _mlaprep_sample_kernel(low_ref, gq_ref, gkv_ref, wqa_ref, wqb_ref, wukt_ref, invf_ref, pos_ref,
                           qlat_ref, qr_ref, ckv_ref, kr_ref):
    q, ckv, k_rope = _mla_common(low_ref[...], pos_ref[...], gq_ref[...], gkv_ref[...], wqa_ref[...], wqb_ref[...],
                                 invf_ref[...], ATTN_SCALE)
    ckv_ref[...] = ckv
    kr_ref[...] = k_rope[:, :QK_ROPE]
    rope_lanes = _iota((1, HEAD_SLOT), 1) < QK_ROPE
    for h in range(MLA_HEADS):
        slot = q[:, HEAD_SLOT * h:HEAD_SLOT * (h + 1)]
        qlat_ref[:, KV_LORA * h:KV_LORA * (h + 1)] = _mm(slot, wukt_ref[h])
        qr_ref[:, HEAD_SLOT * h:HEAD_SLOT * (h + 1)] = jnp.where(rope_lanes, pltpu.roll(slot, LANES - QK_NOPE, 1), 0.0)


def _mlaprep_sample(low, pos, wts):
    n = low.shape[0]
    params = [wts["g_qnorm"], wts["g_kvnorm"], wts["w_qa"], wts["w_qb"], wts["w_ukt_pad"], wts["invf"], pos]
    full = lambda shape: pl.BlockSpec(shape, lambda i: (0,) * len(shape))
    return pl.pallas_call(
        _mlaprep_sample_kernel,
        grid=(1,),
        in_specs=[_const_spec(low)] + [_const_spec(a) for a in params],
        out_specs=[full((n, MLA_HEADS * KV_LORA)), full((n, MLA_HEADS * HEAD_SLOT)), full((n, KV_LORA)),
                   full((n, QK_ROPE))],
        out_shape=[jax.ShapeDtypeStruct((n, MLA_HEADS * KV_LORA), F32),
                   jax.ShapeDtypeStruct((n, MLA_HEADS * HEAD_SLOT), F32),
                   jax.ShapeDtypeStruct((n, KV_LORA), F32), jax.ShapeDtypeStruct((n, QK_ROPE), F32)],
        compiler_params=_cparams("arbitrary"),
        name="mlaprep_sample",
    )(low, *params)


def _attn_kernel(qi_ref, ki_ref, q_ref, k_ref, v_ref, o_ref, m_scr, l_scr, acc_scr):
    step = pl.program_id(2)
    qi = qi_ref[step]
    ki = ki_ref[step]
    bq = q_ref.shape[0]
    bk = k_ref.shape[0]
    on_diagonal = ki == (qi * bq) // bk

    @pl.when(ki == 0)
    def _():
        m_scr[...] = jnp.full(m_scr.shape, -jnp.inf, F32)
        l_scr[...] = jnp.zeros_like(l_scr)
        acc_scr[...] = jnp.zeros_like(acc_scr)

    def accumulate(masked):
        v = v_ref[...]
        if masked:
            visible = ki * bk + _iota((bq, bk), 1) <= qi * bq + _iota((bq, bk), 0)
        for h in range(2):
            q = q_ref[:, HEAD_SLOT * h:HEAD_SLOT * (h + 1)]
            k = k_ref[:, HEAD_SLOT * h:HEAD_SLOT * (h + 1)]
            s = _mm_t(q, k)
            if masked:
                s = jnp.where(visible, s, -jnp.inf)
            m_prev = m_scr[h]
            m_new = jnp.maximum(m_prev, jnp.max(s, axis=1, keepdims=True))
            p = jnp.exp2(s - m_new)
            alpha = jnp.exp2(m_prev - m_new)
            l_scr[h] = alpha * l_scr[h] + jnp.sum(p, axis=1, keepdims=True)
            acc_scr[h] = alpha * acc_scr[h] + _mm(p, v)
            m_scr[h] = m_new

    @pl.when(jnp.logical_not(on_diagonal))
    def _():
        accumulate(False)

    @pl.when(on_diagonal)
    def _():
        accumulate(True)
        head0 = _iota((1, LANES), 1) < V_HEAD
        o_ref[...] = jnp.where(head0, acc_scr[0] / l_scr[0], acc_scr[1] / l_scr[1])


def _attention_prompt(q, k, v, bq, bk):
    bsz, t, _ = q.shape
    assert bk % bq == 0
    n_pairs = MLA_HEADS // 2
    pairs = [(qi, ki) for qi in range(t // bq) for ki in range((qi * bq) // bk + 1)]
    qi_of = jnp.asarray([p[0] for p in pairs], jnp.int32)
    ki_of = jnp.asarray([p[1] for p in pairs], jnp.int32)
    grid_spec = pltpu.PrefetchScalarGridSpec(
        num_scalar_prefetch=2,
        grid=(bsz, n_pairs, len(pairs)),
        in_specs=[pl.BlockSpec((None, bq, 2 * HEAD_SLOT), lambda b, p, s, qi, ki: (b, qi[s], p)),
                  pl.BlockSpec((None, bk, 2 * HEAD_SLOT), lambda b, p, s, qi, ki: (b, ki[s], p)),
                  pl.BlockSpec((None, bk, LANES), lambda b, p, s, qi, ki: (b, ki[s], p))],
        out_specs=pl.BlockSpec((None, bq, LANES), lambda b, p, s, qi, ki: (b, qi[s], p)),
        scratch_shapes=[pltpu.VMEM((2, bq, 1), F32), pltpu.VMEM((2, bq, 1), F32), pltpu.VMEM((2, bq, LANES), F32)],
    )
    return pl.pallas_call(
        _attn_kernel,
        grid_spec=grid_spec,
        out_shape=jax.ShapeDtypeStruct((bsz, t, MLA_HEADS * V_HEAD), F32),
        compiler_params=_cparams("parallel", "parallel", "arbitrary"),
        name="attn_prompt",
    )(qi_of, ki_of, q, k, v)


def _sample_attn_kernel(pt_ref, qlat_ref, qr_ref, cn_ref, kn_ref, ckv_hbm, kr_hbm, o_ref, kbuf, rbuf, s_scr, sem):
    b = pl.program_id(0)
    n_seq = pl.num_programs(0)
    n_pages = kbuf.shape[1]
    slot = lax.rem(b, 2)

    def page_copies(seq, slt):
        out = []
        for pg in range(n_pages):
            page = pt_ref[seq, pg]
            out.append(pltpu.make_async_copy(ckv_hbm.at[page], kbuf.at[slt, pg], sem.at[slt, 0]))
            out.append(pltpu.make_async_copy(kr_hbm.at[page], rbuf.at[slt, pg], sem.at[slt, 1]))
        return out

    @pl.when(b == 0)
    def _():
        for cp in page_copies(0, 0):
            cp.start()

    @pl.when(b + 1 < n_seq)
    def _():
        for cp in page_copies(b + 1, 1 - slot):
            cp.start()

    for cp in page_copies(b, slot):
        cp.wait()

    ql = qlat_ref[...]
    qr = qr_ref[:, :QK_ROPE]
    n_chunks = n_pages // 2
    rows = 2 * PAGE_SIZE

    def score(i, carry):
        kc = kbuf[slot, pl.ds(2 * i, 2)].reshape(rows, KV_LORA)
        rc = jnp.concatenate([rbuf[slot, 2 * i], rbuf[slot, 2 * i + 1]], axis=1)
        s_scr[i] = _mm_t(ql, kc) + jnp.dot(qr, rc, preferred_element_type=F32)
        return carry

    lax.fori_loop(0, n_chunks, score, 0, unroll=SAMPLE_ATTN_UNROLL)
    cn = cn_ref[...]
    kn = kn_ref[...]
    s_new = jnp.sum(ql * cn, axis=-1, keepdims=True) + jnp.sum(qr * kn, axis=-1, keepdims=True)
    s_all = s_scr[...]
    m = jnp.maximum(jnp.max(jnp.max(s_all, axis=0), axis=-1, keepdims=True), s_new)
    p_all = jnp.exp(s_all - m)
    p_new = jnp.exp(s_new - m)
    denom = jnp.sum(jnp.sum(p_all, axis=0), axis=-1, keepdims=True) + p_new
    s_scr[...] = p_all

    def accumulate(i, acc):
        kc = kbuf[slot, pl.ds(2 * i, 2)].reshape(rows, KV_LORA)
        return acc + _mm(s_scr[i], kc)

    acc = lax.fori_loop(0, n_chunks, accumulate, jnp.zeros((ql.shape[0], KV_LORA), F32), unroll=SAMPLE_ATTN_UNROLL)
    o_ref[...] = (acc + p_new * cn) / denom


def _attention_sample(page_table, qlat, qr, ckv_new, kr_new, cache_ckv, cache_kr):
    n, n_pages = page_table.shape
    grid_spec = pltpu.PrefetchScalarGridSpec(
        num_scalar_prefetch=1,
        grid=(n,),
        in_specs=[pl.BlockSpec((None, MLA_HEADS, KV_LORA), lambda b, pt: (b, 0, 0)),
                  pl.BlockSpec((None, MLA_HEADS, HEAD_SLOT), lambda b, pt: (b, 0, 0)),
                  pl.BlockSpec((None, 1, KV_LORA), lambda b, pt: (b, 0, 0)),
                  pl.BlockSpec((None, 1, QK_ROPE), lambda b, pt: (b, 0, 0)),
                  pl.BlockSpec(memory_space=pl.ANY),
                  pl.BlockSpec(memory_space=pl.ANY)],
        out_specs=pl.BlockSpec((None, MLA_HEADS, KV_LORA), lambda b, pt: (b, 0, 0)),
        scratch_shapes=[pltpu.VMEM((2, n_pages, PAGE_SIZE, KV_LORA), F32),
                        pltpu.VMEM((2, n_pages, QK_ROPE, PAGE_SIZE), F32),
                        pltpu.VMEM((n_pages // 2, MLA_HEADS, 2 * PAGE_SIZE), F32),
                        pltpu.SemaphoreType.DMA((2, 2))],
    )
    return pl.pallas_call(
        _sample_attn_kernel,
        grid_spec=grid_spec,
        out_shape=jax.ShapeDtypeStruct((n, MLA_HEADS, KV_LORA), F32),
        compiler_params=_cparams("arbitrary"),
        name="attn_sample",
    )(page_table, qlat.reshape(n, MLA_HEADS, KV_LORA), qr.reshape(n, MLA_HEADS, HEAD_SLOT),
      ckv_new.reshape(n, 1, KV_LORA), kr_new.reshape(n, 1, QK_ROPE), cache_ckv, cache_kr)


def _uv_kernel(ol_ref, wuv_ref, o_ref):
    outs = [_mm(ol_ref[:, KV_LORA * h:KV_LORA * (h + 1)], wuv_ref[:, V_HEAD * h:V_HEAD * (h + 1)])
            for h in range(MLA_HEADS)]
    o_ref[...] = jnp.concatenate(outs, axis=1)


def _latent_to_heads(o_lat, w_uv):
    n = o_lat.shape[0]
    o_lat = o_lat.reshape(n, MLA_HEADS * KV_LORA)
    return pl.pallas_call(
        _uv_kernel,
        grid=(1,),
        in_specs=[_const_spec(o_lat), _const_spec(w_uv)],
        out_specs=pl.BlockSpec((n, MLA_HEADS * V_HEAD), lambda i: (0, 0)),
        out_shape=jax.ShapeDtypeStruct((n, MLA_HEADS * V_HEAD), F32),
        compiler_params=_cparams("arbitrary"),
        name="latent_to_heads",
    )(o_lat, w_uv)


def _layernorm(z, g, b):
    mu = jnp.mean(z, axis=-1, keepdims=True)
    d = z - mu
    var = jnp.mean(d * d, axis=-1, keepdims=True)
    return d * lax.rsqrt(var + LN_EPS) * g + b


def _first_max(x, idx, sentinel):
    mx = jnp.max(x, axis=0, keepdims=True)
    return jnp.min(jnp.where(x == mx, idx, sentinel), axis=0, keepdims=True)


def _route(scores, bias):
    n_tok = scores.shape[1]
    per_group = N_EXPERTS // N_GROUPS
    sb = scores + bias
    sb3 = sb.reshape(N_GROUPS, per_group, n_tok)
    member = _iota(sb3.shape, 1)
    m1 = jnp.max(sb3, axis=1, keepdims=True)
    first = jnp.min(jnp.where(sb3 == m1, member, per_group), axis=1, keepdims=True)
    m2 = jnp.max(jnp.where(member == first, -jnp.inf, sb3), axis=1, keepdims=True)
    g_score = (m1 + m2).reshape(N_GROUPS, n_tok)
    g_idx = _iota(g_score.shape, 0)
    g_sel = jnp.zeros(g_score.shape, F32)
    for _ in range(TOPK_GROUPS):
        hit = g_idx == _first_max(g_score, g_idx, N_GROUPS)
        g_sel = jnp.where(hit, 1.0, g_sel)
        g_score = jnp.where(hit, -jnp.inf, g_score)
    e_mask = jnp.broadcast_to(g_sel.reshape(N_GROUPS, 1, n_tok), sb3.shape).reshape(N_EXPERTS, n_tok) > 0.5
    cur = jnp.where(e_mask, sb, -jnp.inf)
    e_idx = _iota(cur.shape, 0)
    chosen, picked = [], []
    for _ in range(TOP_K):
        first = _first_max(cur, e_idx, N_EXPERTS)
        hit = e_idx == first
        chosen.append(first)
        picked.append(jnp.sum(jnp.where(hit, scores, 0.0), axis=0, keepdims=True))
        cur = jnp.where(hit, -jnp.inf, cur)
    picked = jnp.concatenate(picked, axis=0)
    weights = picked / jnp.sum(picked, axis=0, keepdims=True) * ROUTED_SCALE
    return jnp.concatenate(chosen, axis=0), weights


def _pack_bf16_pairs(x):
    w = x.shape[1] // 2
    bits = lax.bitcast_convert_type(x.astype(BF16).astype(F32), jnp.uint32)
    return (bits[:, :w] >> 16) | (bits[:, w:] & jnp.uint32(0xFFFF0000))


def _unpack_bf16_pairs(p):
    lo = lax.bitcast_convert_type(p << 16, F32)
    hi = lax.bitcast_convert_type(p & jnp.uint32(0xFFFF0000), F32)
    return jnp.concatenate([lo, hi], axis=1).astype(BF16)


def _merge_kernel(x_ref, oa_ref, ob_ref, gs_ref, gt1_ref, sh2_ref, sc2_ref, wba_ref, wbb_ref, wout_ref, g1_ref, b1_ref,
                  wrt_ref, br_ref, x1_ref, h2p_ref, eidx_ref, ew_ref):
    d = D_MODEL
    ya = _mm(oa_ref[...], wba_ref[...])
    yb = _mm(ob_ref[...], wbb_ref[...])
    merged = gs_ref[:, :d] * ya + gs_ref[:, d:] * yb
    z = DN_ALPHA * x_ref[...] + gt1_ref[...] * _mm(merged, wout_ref[...])
    x1 = _layernorm(z, g1_ref[...], b1_ref[...])
    x1_ref[...] = x1
    h2 = x1 * (1 + sc2_ref[...]) + sh2_ref[...]
    h2p_ref[...] = _pack_bf16_pairs(h2)
    scores = jax.nn.sigmoid(_mm3_t(wrt_ref[...], h2))
    eidx_ref[...], ew_ref[...] = _route(scores, br_ref[...])


def _merge(x3, oa, ob, gs, mod, wts, tile):
    bsz, t, d = x3.shape
    params = [wts["w_branch_a"], wts["w_branch_b"], wts["w_out"], wts["ln1_g"], wts["ln1_b"], wts["w_router_t"],
              wts["b_router"]]
    row = lambda w: pl.BlockSpec((None, tile, w), lambda b, i: (b, i, 0))
    per_tok = pl.BlockSpec((None, TOP_K, tile), lambda b, i: (b, 0, i))
    return pl.pallas_call(
        _merge_kernel,
        grid=(bsz, t // tile),
        in_specs=[row(d), row(RW_WIDTH), row(MLA_HEADS * V_HEAD), row(2 * d), mod.spec(2), mod.spec(3), mod.spec(4)]
        + [_const_spec(a) for a in params],
        out_specs=[row(d), row(d // 2), per_tok, per_tok],
        out_shape=[jax.ShapeDtypeStruct((bsz, t, d), F32), jax.ShapeDtypeStruct((bsz, t, d // 2), jnp.uint32),
                   jax.ShapeDtypeStruct((bsz, TOP_K, t), jnp.int32), jax.ShapeDtypeStruct((bsz, TOP_K, t), F32)],
        compiler_params=_cparams("parallel", "arbitrary"),
        name="merge",
    )(x3, oa, ob, gs, mod.arr, mod.arr, mod.arr, *params)


def _rank_kernel(e_ref, rank_ref, count_ref, carry_scr):
    i = pl.program_id(0)
    tt = e_ref.shape[1]

    @pl.when(i == 0)
    def _():
        carry_scr[...] = jnp.zeros_like(carry_scr)

    e = e_ref[...]
    expert = _iota((N_EXPERTS, tt), 0)
    hits = [expert == e[j:j + 1, :] for j in range(TOP_K)]
    onehot = functools.reduce(lambda a, b: a + b, [h.astype(F32) for h in hits])
    earlier = (_iota((tt, tt), 0) < _iota((tt, tt), 1)).astype(BF16)
    before = _mm(onehot, earlier) + carry_scr[:, :1]
    rank_ref[...] = jnp.concatenate(
        [jnp.sum(jnp.where(h, before, 0.0), axis=0, keepdims=True) for h in hits], axis=0).astype(jnp.int32)
    carry_scr[...] = carry_scr[...] + jnp.sum(onehot, axis=1, keepdims=True)
    count_ref[...] = carry_scr[...]


def _assignment_ranks(eidx, tile):
    n = eidx.shape[1]
    rank, counts = pl.pallas_call(
        _rank_kernel,
        grid=(n // tile,),
        in_specs=[pl.BlockSpec((TOP_K, tile), lambda i: (0, i))],
        out_specs=[pl.BlockSpec((TOP_K, tile), lambda i: (0, i)), pl.BlockSpec((N_EXPERTS, LANES), lambda i: (0, 0))],
        out_shape=[jax.ShapeDtypeStruct((TOP_K, n), jnp.int32), jax.ShapeDtypeStruct((N_EXPERTS, LANES), F32)],
        scratch_shapes=[pltpu.VMEM((N_EXPERTS, LANES), F32)],
        compiler_params=_cparams("arbitrary"),
        name="moe_rank",
    )(eidx)
    return rank, counts[:, 0].astype(jnp.int32)


def _row_copy(src, src_row, dst, dst_row, sem):
    return pltpu.make_async_copy(src.at[pl.ds(src_row, 1)], dst.at[pl.ds(dst_row, 1)], sem)


def _dispatch_kernel(dest_ref, h_ref, slots_in, slots_out, sem):
    del slots_in
    n_tok = h_ref.shape[0]

    def start(t, carry):
        for j in range(TOP_K):
            _row_copy(h_ref, t, slots_out, dest_ref[j, t], sem).start()
        return carry

    def wait(t, carry):
        for j in range(TOP_K):
            _row_copy(h_ref, 0, slots_out, 0, sem).wait()
        return carry

    lax.fori_loop(0, n_tok, start, 0, unroll=4)
    lax.fori_loop(0, n_tok, wait, 0, unroll=4)


def _dispatch(h2p, dest, n_rows, tile):
    n, w = h2p.shape
    return pl.pallas_call(
        _dispatch_kernel,
        grid=(n // tile,),
        in_specs=[pl.BlockSpec((TOP_K, tile), lambda i: (0, i), memory_space=pltpu.SMEM),
                  pl.BlockSpec((tile, w), lambda i: (i, 0)),
                  pl.BlockSpec(memory_space=pl.ANY)],
        out_specs=pl.BlockSpec(memory_space=pl.ANY),
        out_shape=jax.ShapeDtypeStruct((n_rows, w), jnp.uint32),
        scratch_shapes=[pltpu.SemaphoreType.DMA],
        input_output_aliases={2: 0},
        compiler_params=_cparams("arbitrary"),
        name="moe_dispatch",
    )(dest, h2p, jnp.zeros((n_rows, w), jnp.uint32))


def _expert_ffn_kernel(be_ref, nu_ref, x_ref, wg_ref, wu_ref, wd_ref, o_ref):
    @pl.when(pl.program_id(0) < nu_ref[0])
    def _():
        x = _unpack_bf16_pairs(x_ref[...])
        act = _silu(jnp.dot(x, wg_ref[...], preferred_element_type=F32)) * jnp.dot(x, wu_ref[...],
                                                                                 preferred_element_type=F32)
        o_ref[...] = _mm(act, wd_ref[...])


def _expert_ffn(slots, block_expert, n_used, wts):
    n_rows, w = slots.shape
    d = 2 * w
    used = lambda i, be, nu: jnp.minimum(i, nu[0] - 1)
    grid_spec = pltpu.PrefetchScalarGridSpec(
        num_scalar_prefetch=2,
        grid=(n_rows // MOE_ROWS,),
        in_specs=[pl.BlockSpec((MOE_ROWS, w), lambda i, be, nu: (used(i, be, nu), 0)),
                  pl.BlockSpec((None, d, EXPERT_FF), lambda i, be, nu: (be[i], 0, 0)),
                  pl.BlockSpec((None, d, EXPERT_FF), lambda i, be, nu: (be[i], 0, 0)),
                  pl.BlockSpec((None, EXPERT_FF, d), lambda i, be, nu: (be[i], 0, 0))],
        out_specs=pl.BlockSpec((MOE_ROWS, d), lambda i, be, nu: (used(i, be, nu), 0)),
    )
    return pl.pallas_call(
        _expert_ffn_kernel,
        grid_spec=grid_spec,
        out_shape=jax.ShapeDtypeStruct((n_rows, d), F32),
        compiler_params=_cparams("arbitrary"),
        name="moe_ffn",
    )(block_expert, n_used, slots, wts["w_exp_gate"], wts["w_exp_up"], wts["w_exp_down"])


def _routed_experts(h2p, eidx, wts):
    n = h2p.shape[0]
    tile = _lane_tile(n, 512)
    rank, counts = _assignment_ranks(eidx, tile)
    padded = (counts + (MOE_ROWS - 1)) // MOE_ROWS * MOE_ROWS
    pad_end = jnp.cumsum(padded)
    dest = (pad_end - padded)[eidx] + rank
    n_blocks = -(-(n * TOP_K) // MOE_ROWS) + N_EXPERTS
    first_row = jnp.arange(n_blocks, dtype=jnp.int32) * MOE_ROWS
    block_expert = jnp.minimum(jnp.searchsorted(pad_end, first_row, side="right"), N_EXPERTS - 1).astype(jnp.int32)
    n_used = (pad_end[-1:] // MOE_ROWS).astype(jnp.int32)
    slots = _dispatch(h2p, dest, n_blocks * MOE_ROWS, tile)
    return _expert_ffn(slots, block_expert, n_used, wts), dest


def _final_kernel(dcur_ref, dnext_ref, ew_ref, x1_ref, h2p_ref, gt2_ref, wsg_ref, wsu_ref, wsd_ref, g2_ref, b2_ref,
                  rows_hbm, y_ref, buf, sem):
    step = pl.program_id(0) * pl.num_programs(1) + pl.program_id(1)
    n_steps = pl.num_programs(0) * pl.num_programs(1)
    n_tok = x1_ref.shape[0]
    slot = lax.rem(step, 2)

    def gather(d_ref, slt):
        def body(t, carry):
            for j in range(TOP_K):
                _row_copy(rows_hbm, d_ref[j, t], buf.at[slt, j], t, sem.at[slt]).start()
            return carry
        lax.fori_loop(0, n_tok, body, 0, unroll=4)

    @pl.when(step == 0)
    def _():
        gather(dcur_ref, 0)

    @pl.when(step + 1 < n_steps)
    def _():
        gather(dnext_ref, 1 - slot)

    def wait(t, carry):
        for j in range(TOP_K):
            _row_copy(rows_hbm, 0, buf.at[slot, j], 0, sem.at[slot]).wait()
        return carry

    lax.fori_loop(0, n_tok, wait, 0, unroll=4)

    w_tok = ew_ref[...].T
    routed = w_tok[:, 0:1] * buf[slot, 0]
    for j in range(1, TOP_K):
        routed = routed + w_tok[:, j:j + 1] * buf[slot, j]
    h2 = _unpack_bf16_pairs(h2p_ref[...])
    act = _silu(jnp.dot(h2, wsg_ref[...], preferred_element_type=F32)) * jnp.dot(h2, wsu_ref[...],
                                                                               preferred_element_type=F32)
    ffn = routed + _mm(act, wsd_ref[...])
    y_ref[...] = _layernorm(DN_ALPHA * x1_ref[...] + gt2_ref[...] * ffn, g2_ref[...], b2_ref[...])


def _final(x1, h2p, dest, ew, expert_rows, mod, wts, tile):
    bsz, t, d = x1.shape
    n_i = t // tile
    last = bsz * n_i - 1
    params = [wts["w_sh_gate"], wts["w_sh_up"], wts["w_sh_down"], wts["ln2_g"], wts["ln2_b"]]
    row = lambda w: pl.BlockSpec((None, tile, w), lambda b, i: (b, i, 0))
    idx = lambda off: pl.BlockSpec((TOP_K, tile), lambda b, i: (0, jnp.minimum(b * n_i + i + off, last)),
                                   memory_space=pltpu.SMEM)
    return pl.pallas_call(
        _final_kernel,
        grid=(bsz, n_i),
        in_specs=[idx(0), idx(1), pl.BlockSpec((TOP_K, tile), lambda b, i: (0, b * n_i + i)),
                  row(d), row(d // 2), mod.spec(5)] + [_const_spec(a) for a in params]
        + [pl.BlockSpec(memory_space=pl.ANY)],
        out_specs=row(d),
        out_shape=jax.ShapeDtypeStruct((bsz, t, d), F32),
        scratch_shapes=[pltpu.VMEM((2, TOP_K, tile, d), F32), pltpu.SemaphoreType.DMA((2,))],
        compiler_params=_cparams("arbitrary", "arbitrary"),
        name="final",
    )(dest, dest, ew, x1, h2p, mod.arr, *params, expert_rows)


def _rope_rotation_columns(w):
    half = QK_ROPE // 2
    return jnp.concatenate([-w[..., half:], w[..., :half]], axis=-1)


def _prepare_weights(P):
    d = D_MODEL
    w = {}
    w_in = P["w_in"]
    o1 = RW_COLS
    o2 = o1 + Q_LORA
    o3 = o2 + KV_LORA
    o4 = o3 + QK_ROPE
    w_kr = w_in[:, o3:o4]
    pad = jnp.zeros((d, LANES - QK_ROPE), F32)
    w["w_rw"] = w_in[:, :o1].astype(BF16)
    w["w_mla"] = jnp.concatenate([w_in[:, o1:o3], w_kr, pad, _rope_rotation_columns(w_kr), pad], axis=1).astype(BF16)
    w["w_gate"] = w_in[:, o4:].astype(BF16)

    row = lambda v: v.reshape(1, -1)
    w["mu_shift"] = row(P["mu_shift"])
    zl = jnp.zeros((DECAY_LORA, RW_WIDTH), F32)
    w["w_lora"] = jnp.concatenate([jnp.concatenate([P["w_decay_up"], zl], axis=1),
                                   jnp.concatenate([zl, P["w_iclr_up"]], axis=1)], axis=0).astype(BF16)
    w["w_gate_up"] = P["w_gate_up"].astype(BF16)
    for nm in ("w_decay0", "a0", "k_k", "k_a", "r_k", "lnx_g", "lnx_b", "g_qnorm", "g_kvnorm", "ln1_g", "ln1_b",
               "ln2_g", "ln2_b"):
        w[nm] = row(P[nm])
    head_of = np.arange(RW_WIDTH) // RW_HEAD_DIM
    w["ones_blk"] = jnp.asarray(head_of[:, None] == head_of[None, :], BF16)

    wq = P["w_uq"].reshape(Q_LORA, MLA_HEADS, QK_NOPE + QK_ROPE)
    zq = jnp.zeros((Q_LORA, MLA_HEADS, HEAD_SLOT - QK_NOPE - QK_ROPE), F32)
    w["w_qa"] = jnp.concatenate([wq, zq], axis=-1).reshape(Q_LORA, MLA_HEADS * HEAD_SLOT).astype(BF16)
    w["w_qb"] = jnp.concatenate([jnp.zeros((Q_LORA, MLA_HEADS, QK_NOPE), F32), _rope_rotation_columns(wq[..., QK_NOPE:]), zq],
                                axis=-1).reshape(Q_LORA, MLA_HEADS * HEAD_SLOT).astype(BF16)
    wuk = P["w_uk"].reshape(KV_LORA, MLA_HEADS, QK_NOPE)
    zk = jnp.zeros((KV_LORA, MLA_HEADS, HEAD_SLOT - QK_NOPE), F32)
    w["w_uk_pad"] = jnp.concatenate([wuk, zk], axis=-1).reshape(KV_LORA, MLA_HEADS * HEAD_SLOT).astype(BF16)
    w["w_ukt_pad"] = jnp.transpose(jnp.concatenate([wuk, zk], axis=-1), (1, 2, 0)).astype(BF16)
    w["w_uv"] = P["w_uv"].astype(BF16)
    half = QK_ROPE // 2
    inv = ROPE_THETA ** (-jnp.arange(half, dtype=F32) / half)
    w["invf"] = jnp.concatenate([jnp.zeros((QK_NOPE,), F32), inv, inv,
                                 jnp.zeros((HEAD_SLOT - QK_NOPE - QK_ROPE,), F32)]).reshape(1, HEAD_SLOT)

    for nm in ("w_branch_a", "w_branch_b", "w_out", "w_exp_gate", "w_exp_up", "w_exp_down", "w_sh_gate", "w_sh_up",
               "w_sh_down"):
        w[nm] = P[nm].astype(BF16)
    w["w_router_t"] = P["w_router"].T
    w["b_router"] = P["b_router"].reshape(N_EXPERTS, 1)
    return w


def _row_tile(t, cap):
    tile = min(t, cap)
    assert t % tile == 0 and tile % 16 == 0, (t, tile)
    return tile


def _lane_tile(n, cap):
    tiles = [m for m in range(LANES, cap + 1, LANES) if n % m == 0]
    assert tiles, (n, cap)
    return tiles[-1]


def _mixers_prompt(x, mod_rows, wts):
    bsz, t, d = x.shape
    tile = _row_tile(t, 256)
    mod = _Mod(mod_rows.reshape(bsz * 6, 1, d), per_token=False, tile=tile)
    p_rw, low, gs = _inproj(x, mod, wts, tile)
    rw = dict(zip(RW_OUT_NAMES, _rwprep_prompt(p_rw, jnp.zeros((bsz, 1, RW_COLS), F32), wts, tile)))
    oa, state = _rw_scan(rw, wts)
    q, k, v, ckv, k_rope = _mlaprep_prompt(low, wts, tile)
    ob = _attention_prompt(q, k, v, _row_tile(t, 512), _row_tile(t, 1024))
    merged = _merge(x, oa, ob, gs, mod, wts, tile)
    return merged, mod, (ckv, k_rope, _unpack_state(state), p_rw[:, -1])


def _mixers_sample(x, mod_rows, state, shift, page_table, cache_ckv, cache_kr, wts):
    n, s_new, d = x.shape
    assert s_new == 1
    past = page_table.shape[1] * PAGE_SIZE
    x3 = x.reshape(1, n, d)
    mod = _Mod(mod_rows, per_token=True, tile=n)
    p_rw, low, gs = _inproj(x3, mod, wts, n)
    p_rw2 = p_rw.reshape(n, RW_COLS)
    rw = dict(zip(RW_OUT_NAMES, _rwprep_sample(p_rw2, shift, wts)))
    new_state_t, oa = _rw_step(jnp.transpose(state, (1, 2, 3, 0)), rw, wts)
    new_state = jnp.transpose(new_state_t, (3, 0, 1, 2))
    pos = jnp.full((n, 1), past, F32)
    qlat, qr, ckv, k_rope = _mlaprep_sample(low.reshape(n, MLA_LOW_COLS), pos, wts)
    o_lat = _attention_sample(page_table, qlat, qr, ckv, k_rope, cache_ckv, jnp.swapaxes(cache_kr, 1, 2))
    ob = _latent_to_heads(o_lat, wts["w_uv"])
    merged = _merge(x3, oa.reshape(1, n, RW_WIDTH), ob.reshape(1, n, MLA_HEADS * V_HEAD), gs, mod, wts, n)
    return merged, mod, (ckv.reshape(n, 1, KV_LORA), k_rope.reshape(n, 1, QK_ROPE), new_state, p_rw2)


def _per_token_rows(a):
    return jnp.swapaxes(a, 0, 1).reshape(a.shape[1], -1)


def _moe_and_norm(merged_p, mod_p, merged_s, mod_s, wts):
    x1_p, h2p_p, eidx_p, ew_p = merged_p
    x1_s, h2p_s, eidx_s, ew_s = merged_s
    half = h2p_p.shape[-1]
    n_p = x1_p.shape[0] * x1_p.shape[1]
    h2p = jnp.concatenate([h2p_p.reshape(-1, half), h2p_s.reshape(-1, half)], axis=0)
    eidx = jnp.concatenate([_per_token_rows(eidx_p), _per_token_rows(eidx_s)], axis=1)
    expert_rows, dest = _routed_experts(h2p, eidx, wts)
    y_p = _final(x1_p, h2p_p, dest[:, :n_p], _per_token_rows(ew_p), expert_rows, mod_p, wts,
                 _row_tile(x1_p.shape[1], LANES))
    y_s = _final(x1_s, h2p_s, dest[:, n_p:], _per_token_rows(ew_s), expert_rows, mod_s, wts, x1_s.shape[1])
    return y_p, y_s


def kernel(x_prompt, x_sample, c_prompt, c_sample, cache_ckv, cache_krope, state_wkv, state_shift, page_table, w_ada, b_ada, w_in, mu_shift, w_decay0, w_decay_up, a0, w_iclr_up, w_gate_up, k_k, k_a, r_k, lnx_g, lnx_b, w_branch_a, g_qnorm, w_uq, g_kvnorm, w_uk, w_uv, w_branch_b, w_out, ln1_g, ln1_b, w_router, b_router, w_exp_gate, w_exp_up, w_exp_down, w_sh_gate, w_sh_up, w_sh_down, ln2_g, ln2_b):
    params = dict(w_ada=w_ada, b_ada=b_ada, w_in=w_in, mu_shift=mu_shift, w_decay0=w_decay0, w_decay_up=w_decay_up,
                  a0=a0, w_iclr_up=w_iclr_up, w_gate_up=w_gate_up, k_k=k_k, k_a=k_a, r_k=r_k, lnx_g=lnx_g,
                  lnx_b=lnx_b, w_branch_a=w_branch_a, g_qnorm=g_qnorm, w_uq=w_uq, g_kvnorm=g_kvnorm, w_uk=w_uk,
                  w_uv=w_uv, w_branch_b=w_branch_b, w_out=w_out, ln1_g=ln1_g, ln1_b=ln1_b, w_router=w_router,
                  b_router=b_router, w_exp_gate=w_exp_gate, w_exp_up=w_exp_up, w_exp_down=w_exp_down,
                  w_sh_gate=w_sh_gate, w_sh_up=w_sh_up, w_sh_down=w_sh_down, ln2_g=ln2_g, ln2_b=ln2_b)
    depth = w_in.shape[0]
    bp = x_prompt.shape[0]
    bd = x_sample.shape[0]
    n_c = bp + bd
    c_all = jnp.concatenate([c_prompt, c_sample, jnp.zeros((-n_c % 8, D_MODEL), F32)], axis=0)
    yp, ys = x_prompt, x_sample
    outs = [[] for _ in range(8)]
    for l in range(depth):
        wts = _prepare_weights({name: arr[l] for name, arr in params.items()})
        mod = _adaln_mod(c_all, params["w_ada"][l], params["b_ada"][l].reshape(1, -1))
        merged_p, mod_p, rest_p = _mixers_prompt(yp, mod[:bp], wts)
        merged_s, mod_s, rest_s = _mixers_sample(ys, mod[bp:n_c], state_wkv[l], state_shift[l], page_table,
                                                 cache_ckv[l], cache_krope[l], wts)
        yp, ys = _moe_and_norm(merged_p, mod_p, merged_s, mod_s, wts)
        ys = ys.reshape(x_sample.shape)
        for acc, val in zip(outs, rest_p + rest_s):
            acc.append(val)
    return (yp, ys) + tuple(jnp.stack(o) for o in outs)
```

```python
import math

import numpy as np
import jax
import jax.numpy as jnp
from jax import lax
from jax.experimental import pallas as pl
from jax.experimental.pallas import tpu as pltpu

F32 = jnp.float32
BF16 = jnp.bfloat16

D_MODEL = 1024
PAGE_SIZE = 128
RW_HEADS = 8
RW_HEAD_DIM = 64
RW_WIDTH = RW_HEADS * RW_HEAD_DIM
DECAY_LORA = 64
ICLR_LORA = 64
GATE_LORA = 128
RW_COLS = 3 * RW_WIDTH + DECAY_LORA + ICLR_LORA + GATE_LORA
GN_EPS = 64e-5
MLA_HEADS = 8
QK_NOPE = 64
QK_ROPE = 32
V_HEAD = 64
Q_LORA = 384
KV_LORA = 256
ROPE_THETA = 10000.0
ATTN_SCALE = (QK_NOPE + QK_ROPE) ** -0.5
N_EXPERTS = 64
TOP_K = 8
N_GROUPS = 8
TOPK_GROUPS = 4
EXPERT_FF = 256
SHARED_FF = 256
ROUTED_SCALE = 2.5
DEPTH = 1
DN_ALPHA = (2 * DEPTH) ** 0.25
LN_EPS = 1e-5
RMS_EPS = 1e-6

LANES = 128
HEAD_SLOT = LANES
MLA_LOW_COLS = Q_LORA + KV_LORA + 2 * LANES
SCAN_CHUNK = 64
SCAN_SUBCHUNKS = 2
SAMPLE_ATTN_UNROLL = 8
EXPERTS_PER_STEP = 4
VMEM_LIMIT = 56 * 1024 * 1024


def _cparams(*sem):
    return pltpu.CompilerParams(dimension_semantics=sem, vmem_limit_bytes=VMEM_LIMIT)


def _mm(a, b):
    return jnp.dot(a.astype(BF16), b.astype(BF16), preferred_element_type=F32)


def _mm_t(a, b):
    return lax.dot_general(a.astype(BF16), b.astype(BF16), (((1,), (1,)), ((), ())), preferred_element_type=F32)


def _mm_tl(a, b):
    return lax.dot_general(a.astype(BF16), b.astype(BF16), (((0,), (0,)), ((), ())), preferred_element_type=F32)


def _split(x):
    hi = x.astype(BF16)
    lo = (x - hi.astype(F32)).astype(BF16)
    return hi, lo


def _three_pass(f, a, b):
    ah, al = _split(a)
    bh, bl = _split(b)
    return f(ah, bh) + (f(ah, bl) + f(al, bh))


def _mm3(a, b):
    return _three_pass(_mm, a, b)


def _mm3_t(a, b):
    return _three_pass(_mm_t, a, b)


def _mm3_tl(a, b):
    return _three_pass(_mm_tl, a, b)


_mm_state = _mm
_mm_merge = _mm


def _mm_exact_rhs(a, b_exact):
    ah, al = _split(a)
    return _mm(ah, b_exact) + _mm(al, b_exact)


def _silu(x):
    return x * jax.nn.sigmoid(x)


def _iota(shape, dim):
    return lax.broadcasted_iota(jnp.int32, shape, dim)


def _mod_kernel(c_ref, w_ref, b_ref, o_ref):
    o_ref[...] = _mm(_silu(c_ref[...]), w_ref[...]) + b_ref[...]


def _adaln_mod(c_all, w_ada, b_ada):
    n = c_all.shape[0]
    d = D_MODEL
    return pl.pallas_call(
        _mod_kernel,
        grid=(6,),
        in_specs=[pl.BlockSpec((n, d), lambda j: (0, 0)),
                  pl.BlockSpec((d, d), lambda j: (0, j)),
                  pl.BlockSpec((1, d), lambda j: (0, j))],
        out_specs=pl.BlockSpec((n, d), lambda j: (0, j)),
        out_shape=jax.ShapeDtypeStruct((n, 6 * d), F32),
        compiler_params=_cparams("arbitrary"),
        name="adaln_mod",
    )(c_all, w_ada, b_ada)


class _Mod:
    def __init__(self, arr, per_token, tile):
        self.arr = arr
        self.per_token = per_token
        self.tile = tile

    def spec(self, j):
        if self.per_token:
            return pl.BlockSpec((self.tile, D_MODEL), lambda b, i, *_: (i, j))
        return pl.BlockSpec((None, 1, D_MODEL), lambda b, i, *_: (b * 6 + j, 0, 0))


def _const_spec(arr):
    nd = arr.ndim
    return pl.BlockSpec(arr.shape, lambda *_: (0,) * nd)


def _inproj_kernel(x_ref, sh_ref, sc_ref, wrw_ref, wmla_ref, wg_ref, prw_ref, mla_ref, gs_ref):
    h = (x_ref[...] * (1 + sc_ref[...]) + sh_ref[...]).astype(BF16)
    prw_ref[...] = jnp.dot(h, wrw_ref[...], preferred_element_type=F32)
    mla_ref[...] = jnp.dot(h, wmla_ref[...], preferred_element_type=F32)
    gs_ref[...] = jax.nn.sigmoid(jnp.dot(h, wg_ref[...], preferred_element_type=F32))


def _inproj(x3, mod, wts, tile):
    bsz, t, d = x3.shape
    row = lambda w: pl.BlockSpec((None, tile, w), lambda b, i: (b, i, 0))
    return pl.pallas_call(
        _inproj_kernel,
        grid=(bsz, t // tile),
        in_specs=[row(d), mod.spec(0), mod.spec(1),
                  _const_spec(wts["w_rw"]), _const_spec(wts["w_mla"]), _const_spec(wts["w_gate"])],
        out_specs=[row(RW_COLS), row(MLA_LOW_COLS), row(2 * d)],
        out_shape=[jax.ShapeDtypeStruct((bsz, t, RW_COLS), F32),
                   jax.ShapeDtypeStruct((bsz, t, MLA_LOW_COLS), F32),
                   jax.ShapeDtypeStruct((bsz, t, 2 * d), F32)],
        compiler_params=_cparams("parallel", "arbitrary"),
        name="inproj",
    )(x3, mod.arr, mod.arr, wts["w_rw"], wts["w_mla"], wts["w_gate"])


RW_OUT_NAMES = ("r", "kmod", "v", "kkn", "bvec", "logw", "bonus", "g")


def _rw_elementwise(p, pp, mu, w_lora, w_gate_up, w_decay0, a0, k_k, k_a, r_k, ones_blk):
    w = RW_WIDTH
    pm = p + (pp - p) * mu
    r, k, v = pm[:, 0:w], pm[:, w:2 * w], pm[:, 2 * w:3 * w]
    xwa = pm[:, 3 * w:3 * w + DECAY_LORA + ICLR_LORA]
    xg = pm[:, 3 * w + DECAY_LORA + ICLR_LORA:]
    lane = _iota((1, DECAY_LORA + ICLR_LORA), 1)
    z = jnp.where(lane < DECAY_LORA, jnp.tanh(xwa), xwa)
    lora = _mm(z, w_lora)
    y = -(w_decay0 + lora[:, :w])
    softplus = jnp.maximum(y, 0.0) + jnp.log1p(jnp.exp(-jnp.abs(y)))
    w_log = -softplus - 0.5
    logw = -jnp.exp(w_log)
    a = jax.nn.sigmoid(a0 + lora[:, w:])
    g = _mm(jax.nn.sigmoid(xg), w_gate_up)
    kk = k * k_k
    kk = kk * lax.rsqrt(_mm_exact_rhs(kk * kk, ones_blk) + 1e-12)
    kmod = k * (1.0 + (a - 1.0) * k_a)
    bonus = _mm_exact_rhs(r * kmod * r_k, ones_blk) * v
    return r, kmod, v, -kk, kk * a, logw, bonus, g


def _rwprep_sample_kernel(p_ref, pp_ref, mu_ref, wl_ref, wgu_ref, wd0_ref, a0_ref, kk_ref, ka_ref, rk_ref,
                          ones_ref, *out_refs):
    outs = _rw_elementwise(p_ref[...], pp_ref[...], mu_ref[...], wl_ref[...], wgu_ref[...], wd0_ref[...], a0_ref[...],
                           kk_ref[...], ka_ref[...], rk_ref[...], ones_ref[...])
    for o_ref, val in zip(out_refs, outs):
        o_ref[...] = val


def _rw_param_list(wts):
    return [wts["mu_shift"], wts["w_lora"], wts["w_gate_up"], wts["w_decay0"], wts["a0"], wts["k_k"], wts["k_a"],
            wts["r_k"], wts["ones_blk"]]


def _rwprep_sample(p_rw, p_prev, wts):
    n = p_rw.shape[0]
    params = _rw_param_list(wts)
    return pl.pallas_call(
        _rwprep_sample_kernel,
        grid=(1,),
        in_specs=[_const_spec(p_rw), _const_spec(p_prev)] + [_const_spec(a) for a in params],
        out_specs=[pl.BlockSpec((n, RW_WIDTH), lambda i: (0, 0))] * 8,
        out_shape=[jax.ShapeDtypeStruct((n, RW_WIDTH), F32)] * 8,
        compiler_params=_cparams("arbitrary"),
        name="rwprep_sample",
    )(p_rw, p_prev, *params)


def _group_norm_pair(o, ones_pair):
    mu = _mm_exact_rhs(o, ones_pair) * (1.0 / RW_HEAD_DIM)
    d = o - mu
    var = _mm_exact_rhs(d * d, ones_pair) * (1.0 / RW_HEAD_DIM)
    return d * lax.rsqrt(var + GN_EPS)


def _scan_kernel(p_ref, s0_ref, mu_ref, wl_ref, wgu_ref, wd0_ref, a0_ref, kk_ref, ka_ref, rk_ref, ones_ref,
                 lng_ref, lnb_ref, o_ref, st_ref, h_scr, prev_scr):
    c = pl.program_id(1)
    n_steps = pl.num_programs(1)
    n_pairs = h_scr.shape[0]
    cl = SCAN_CHUNK
    n_rows = p_ref.shape[0]
    n_sub = n_rows // cl

    @pl.when(c == 0)
    def _():
        h_scr[...] = jnp.zeros_like(h_scr)
        prev_scr[...] = s0_ref[...]

    p = p_ref[...]
    pp = jnp.where(_iota(p.shape, 0) == 0, prev_scr[...], pltpu.roll(p, 1, 0))
    prev_scr[...] = p[n_rows - 1:n_rows, :]
    r_all, k_all, v_all, a_all, b_all, lw_all, bon_all, g_all = _rw_elementwise(
        p, pp, mu_ref[...], wl_ref[...], wgu_ref[...], wd0_ref[...], a0_ref[...], kk_ref[...], ka_ref[...],
        rk_ref[...], ones_ref[...])

    row = _iota((cl, cl), 0)
    col = _iota((cl, cl), 1)
    tri_incl = row >= col
    tri_strict = row > col
    tri_ones = tri_incl.astype(BF16)
    eye = (row == col).astype(F32)
    head0 = _iota((1, LANES), 1) < RW_HEAD_DIM
    head_mask = (head0, jnp.logical_not(head0))
    head0_2 = (_iota((1, 2 * LANES), 1) & RW_HEAD_DIM) == 0
    r128 = _iota((LANES, LANES), 0)
    c128 = _iota((LANES, LANES), 1)
    same_head = (r128 < RW_HEAD_DIM) == (c128 < RW_HEAD_DIM)
    diag = r128 == c128
    ones_pair = same_head.astype(BF16)
    units = [(s, p) for s in range(n_sub) for p in range(n_pairs)]
    heads = [(s, p, h) for (s, p) in units for h in range(2)]
    rows = {s: slice(cl * s, cl * (s + 1)) for s in range(n_sub)}
    lanes = {p: slice(LANES * p, LANES * (p + 1)) for p in range(n_pairs)}
    at = lambda ref, u: ref[rows[u[0]], lanes[u[1]]]

    lw = {u: at(lw_all, u) for u in units}
    lg = {}
    for u in units:
        l1 = lw[u].astype(BF16)
        rem = lw[u] - l1.astype(F32)
        l2 = rem.astype(BF16)
        l3 = (rem - l2.astype(F32)).astype(BF16)
        lg[u] = _mm(tri_ones, l1) + (_mm(tri_ones, l2) + _mm(tri_ones, l3))
    vv = {u: at(v_all, u) for u in units}
    rh, ah, bh, kh, bt, kt, g_last = {}, {}, {}, {}, {}, {}, {}
    for u in units:
        bb = at(b_all, u)
        kk = at(k_all, u)
        lg_last = lg[u][cl - 1:cl, :]
        rh[u] = at(r_all, u) * jnp.exp(lg[u])
        ah[u] = at(a_all, u) * jnp.exp(lg[u] - lw[u])
        inv = jnp.exp(-lg[u])
        bh[u] = bb * inv
        kh[u] = kk * inv
        tail = jnp.exp(lg_last - lg[u])
        bt[u] = bb * tail
        kt[u] = kk * tail
        g_last[u] = jnp.exp(lg_last)

    xb, xk = {}, {}
    for (s, p, h) in heads:
        ar_h = jnp.where(head_mask[h], jnp.concatenate([ah[s, p], rh[s, p]], axis=0), 0.0)
        xb[s, p, h] = _mm_t(ar_h, bh[s, p])
        xk[s, p, h] = _mm_t(ar_h, kh[s, p])
    l_ab = {k: jnp.where(tri_strict, x[:cl], 0.0) for k, x in xb.items()}
    m_rb = {k: jnp.where(tri_incl, x[cl:], 0.0) for k, x in xb.items()}
    l_ak = {k: jnp.where(tri_strict, x[:cl], 0.0) for k, x in xk.items()}
    m_rk = {k: jnp.where(tri_incl, x[cl:], 0.0) for k, x in xk.items()}
    lv_h = {k: _mm(l_ak[k], vv[k[:2]]) for k in heads}
    mrkv_h = {k: _mm(m_rk[k], vv[k[:2]]) for k in heads}
    ktv = {u: _mm_tl(kt[u], vv[u]) for u in units}

    base_bits = 3
    blk = lambda bits: (row >> bits) == (col >> bits)
    sum_half = _iota((1, 2 * cl), 1) >= cl
    x = {k: jnp.concatenate([jnp.where(blk(base_bits), l_ab[k], 0.0), eye], axis=1) for k in heads}
    for _ in range(base_bits):
        nxt = {}
        for k in heads:
            xh, xl = _split(x[k])
            ph, plo = xh[:, :cl], xl[:, :cl]
            nxt[k] = (_mm(ph, xh) + (_mm(ph, xl) + _mm(plo, xh))) + jnp.where(sum_half, x[k], 0.0)
        x = nxt
    t_inv = {k: x[k][:, cl:] for k in heads}
    for bits in range(base_bits, int(math.log2(cl))):
        lower_left = jnp.logical_and(blk(bits + 1), jnp.logical_not(blk(bits)))
        y = {k: _mm_merge(jnp.where(lower_left, l_ab[k], 0.0), t_inv[k]) for k in heads}
        t_inv = {k: t_inv[k] + _mm_merge(t_inv[k], y[k]) for k in heads}

    pick = lambda d, u: jnp.where(head0, d[u + (0,)], d[u + (1,)])
    pick2 = lambda d, u: jnp.where(head0_2, d[u + (0,)], d[u + (1,)])
    z = {u: jnp.concatenate([ah[u], pick(lv_h, u)], axis=1) for u in units}
    tz = {k: _mm(t_inv[k], z[k[:2]]) for k in heads}
    w12 = {u: pick2(tz, u) for u in units}
    q12_h = {k: _mm(m_rb[k], w12[k[:2]]) for k in heads}
    g12 = {u: _mm_tl(bt[u], w12[u]) for u in units}
    q1, q2, g1, g2 = {}, {}, {}, {}
    for u in units:
        q12 = pick2(q12_h, u)
        q1[u] = rh[u] + q12[:, :LANES]
        q2[u] = q12[:, LANES:] + pick(mrkv_h, u)
        g1[u] = jnp.where(same_head, g12[u][:, :LANES], 0.0) + jnp.where(diag, g_last[u], 0.0)
        g2[u] = jnp.where(same_head, g12[u][:, LANES:] + ktv[u], 0.0)

    state = {p: h_scr[p] for p in range(n_pairs)}
    outs = {}
    for s in range(n_sub):
        for p in range(n_pairs):
            outs[s, p] = _mm_state(q1[s, p], state[p]) + q2[s, p]
        state = {p: _mm_state(g1[s, p], state[p]) + g2[s, p] for p in range(n_pairs)}
    for p in range(n_pairs):
        h_scr[p] = state[p]

    for u in units:
        y = _group_norm_pair(outs[u], ones_pair) * lng_ref[:, lanes[u[1]]] + lnb_ref[:, lanes[u[1]]]
        o_ref[rows[u[0]], lanes[u[1]]] = (y + at(bon_all, u)) * at(g_all, u)

    @pl.when(c == n_steps - 1)
    def _():
        st_ref[...] = h_scr[...]


def _rw_scan(p_rw, shift0, wts):
    bsz, t, _ = p_rw.shape
    w = RW_WIDTH
    n_pairs = w // LANES
    step_rows = SCAN_CHUNK * SCAN_SUBCHUNKS
    assert t % step_rows == 0, (t, step_rows)
    blk = lambda width: pl.BlockSpec((None, step_rows, width), lambda b, c: (b, c, 0))
    params = _rw_param_list(wts) + [wts["lnx_g"], wts["lnx_b"]]
    return pl.pallas_call(
        _scan_kernel,
        grid=(bsz, t // step_rows),
        in_specs=[blk(RW_COLS), pl.BlockSpec((None, 1, RW_COLS), lambda b, c: (b, 0, 0))]
        + [_const_spec(a) for a in params],
        out_specs=[blk(w), pl.BlockSpec((None, n_pairs, LANES, LANES), lambda b, c: (b, 0, 0, 0))],
        out_shape=[jax.ShapeDtypeStruct((bsz, t, w), F32),
                   jax.ShapeDtypeStruct((bsz, n_pairs, LANES, LANES), F32)],
        scratch_shapes=[pltpu.VMEM((n_pairs, LANES, LANES), F32), pltpu.VMEM((1, RW_COLS), F32)],
        compiler_params=_cparams("parallel", "arbitrary"),
        name="rw_scan",
    )(p_rw, shift0, *params)


def _unpack_state(st):
    hd = RW_HEAD_DIM
    blocks = [st[:, p, j * hd:(j + 1) * hd, j * hd:(j + 1) * hd] for p in range(st.shape[1]) for j in range(2)]
    return jnp.swapaxes(jnp.stack(blocks, axis=1), -1, -2)


def _rwstep_kernel(s_ref, r_ref, k_ref, v_ref, a_ref, b_ref, lw_ref, bon_ref, g_ref, lng_ref, lnb_ref, so_ref, o_ref,
                   acc_scr, vt_scr):
    hd = RW_HEAD_DIM
    vt_scr[...] = v_ref[...].T
    decay2 = jnp.exp(lw_ref[...]).T
    kkn2 = a_ref[...].T
    kka2 = b_ref[...].T
    kmod2 = k_ref[...].T
    r2 = r_ref[...].T
    normed = []
    for h in range(2):
        ch = slice(hd * h, hd * (h + 1))
        decay, kkn, kka, kmod, r = decay2[ch], kkn2[ch], kka2[ch], kmod2[ch], r2[ch]

        def body(i, carry):
            s = s_ref[h, i]
            sa = jnp.sum(s * kkn, axis=0, keepdims=True)
            s_new = s * decay + sa * kka + vt_scr[pl.ds(hd * h + i, 1), :] * kmod
            so_ref[h, i] = s_new
            acc_scr[pl.ds(hd * h + i, 1), :] = jnp.sum(s_new * r, axis=0, keepdims=True)
            return carry

        lax.fori_loop(0, hd, body, 0, unroll=8)
        o = acc_scr[ch, :]
        mu = jnp.mean(o, axis=0, keepdims=True)
        d = o - mu
        var = jnp.mean(d * d, axis=0, keepdims=True)
        normed.append(d * lax.rsqrt(var + GN_EPS))
    y = jnp.concatenate(normed, axis=0).T * lng_ref[...] + lnb_ref[...]
    o_ref[...] = (y + bon_ref[...]) * g_ref[...]


def _rw_step(state_t, rw, wts):
    n = state_t.shape[-1]
    hd = RW_HEAD_DIM
    vec = pl.BlockSpec((n, 2 * hd), lambda p: (0, p))
    par = pl.BlockSpec((1, 2 * hd), lambda p: (0, p))
    st = pl.BlockSpec((2, hd, hd, n), lambda p: (p, 0, 0, 0))
    ins = [rw[nm] for nm in ("r", "kmod", "v", "kkn", "bvec", "logw", "bonus", "g")]
    return pl.pallas_call(
        _rwstep_kernel,
        grid=(RW_HEADS // 2,),
        in_specs=[st] + [vec] * 8 + [par, par],
        out_specs=[st, vec],
        out_shape=[jax.ShapeDtypeStruct(state_t.shape, F32), jax.ShapeDtypeStruct((n, RW_WIDTH), F32)],
        scratch_shapes=[pltpu.VMEM((2 * hd, n), F32), pltpu.VMEM((2 * hd, n), F32)],
        compiler_params=_cparams("parallel"),
        name="rw_step",
    )(state_t, *ins, wts["lnx_g"], wts["lnx_b"])


def _rms(x, g):
    return x * lax.rsqrt(jnp.mean(x * x, axis=-1, keepdims=True) + RMS_EPS) * g


def _mla_common(low, pos, gq, gkv, wqa, wqb, invf, q_scale):
    cq = _rms(low[:, :Q_LORA], gq)
    ckv = _rms(low[:, Q_LORA:Q_LORA + KV_LORA], gkv)
    kr = low[:, Q_LORA + KV_LORA:Q_LORA + KV_LORA + LANES]
    kr_rot = low[:, Q_LORA + KV_LORA + LANES:]
    ang = pos * invf
    cos = jnp.cos(ang)
    sin = jnp.sin(ang)
    cos8 = jnp.concatenate([cos] * MLA_HEADS, axis=1)
    sin8 = jnp.concatenate([sin] * MLA_HEADS, axis=1)
    q = (_mm(cq, wqa) * cos8 + _mm(cq, wqb) * sin8) * q_scale
    k_rope = kr * pltpu.roll(cos, LANES - QK_NOPE, 1) + kr_rot * pltpu.roll(sin, LANES - QK_NOPE, 1)
    return q, ckv, k_rope


def _mlaprep_prompt_kernel(low_ref, gq_ref, gkv_ref, wqa_ref, wqb_ref, wuk_ref, wuv_ref, invf_ref,
                           q_ref, k_ref, v_ref, ckv_ref, kr_ref):
    i = pl.program_id(1)
    rows = low_ref.shape[0]
    pos = (i * rows + _iota((rows, 1), 0)).astype(F32)
    q, ckv, k_rope = _mla_common(low_ref[...], pos, gq_ref[...], gkv_ref[...], wqa_ref[...], wqb_ref[...],
                                 invf_ref[...], ATTN_SCALE * math.log2(math.e))
    q_ref[...] = q.astype(BF16)
    ckv_ref[...] = ckv
    kr_ref[...] = k_rope[:, :QK_ROPE]
    k_slot = pltpu.roll(k_rope, QK_NOPE, 1)
    k_ref[...] = (_mm(ckv, wuk_ref[...]) + jnp.concatenate([k_slot] * MLA_HEADS, axis=1)).astype(BF16)
    v_ref[...] = _mm(ckv, wuv_ref[...]).astype(BF16)


def _mlaprep_prompt(low, wts, tile):
    bsz, t, _ = low.shape
    params = [wts["g_qnorm"], wts["g_kvnorm"], wts["w_qa"], wts["w_qb"], wts["w_uk_pad"], wts["w_uv"], wts["invf"]]
    row = lambda w: pl.BlockSpec((None, tile, w), lambda b, i: (b, i, 0))
    slots = MLA_HEADS * HEAD_SLOT
    return pl.pallas_call(
        _mlaprep_prompt_kernel,
        grid=(bsz, t // tile),
        in_specs=[row(MLA_LOW_COLS)] + [_const_spec(a) for a in params],
        out_specs=[row(slots), row(slots), row(MLA_HEADS * V_HEAD), row(KV_LORA), row(QK_ROPE)],
        out_shape=[jax.ShapeDtypeStruct((bsz, t, slots), BF16), jax.ShapeDtypeStruct((bsz, t, slots), BF16),
                   jax.ShapeDtypeStruct((bsz, t, MLA_HEADS * V_HEAD), BF16),
                   jax.ShapeDtypeStruct((bsz, t, KV_LORA), F32), jax.ShapeDtypeStruct((bsz, t, QK_ROPE), F32)],
        compiler_params=_cparams("parallel", "arbitrary"),
        name="mlaprep_prompt",
    )(low, *params)


def _mlaprep_sample_kernel(low_ref, gq_ref, gkv_ref, wqa_ref, wqb_ref, wukt_ref, invf_ref, pos_ref,
                           qlat_ref, qr_ref, ckv_ref, kr_ref):
    q, ckv, k_rope = _mla_common(low_ref[...], pos_ref[...], gq_ref[...], gkv_ref[...], wqa_ref[...], wqb_ref[...],
                                 invf_ref[...], ATTN_SCALE)
    ckv_ref[...] = ckv
    kr_ref[...] = k_rope[:, :QK_ROPE]
    rope_lanes = _iota((1, HEAD_SLOT), 1) < QK_ROPE
    for h in range(MLA_HEADS):
        slot = q[:, HEAD_SLOT * h:HEAD_SLOT * (h + 1)]
        qlat_ref[:, KV_LORA * h:KV_LORA * (h + 1)] = _mm(slot, wukt_ref[h])
        qr_ref[:, HEAD_SLOT * h:HEAD_SLOT * (h + 1)] = jnp.where(rope_lanes, pltpu.roll(slot, LANES - QK_NOPE, 1), 0.0)


def _mlaprep_sample(low, pos, wts):
    n = low.shape[0]
    params = [wts["g_qnorm"], wts["g_kvnorm"], wts["w_qa"], wts["w_qb"], wts["w_ukt_pad"], wts["invf"], pos]
    full = lambda shape: pl.BlockSpec(shape, lambda i: (0,) * len(shape))
    return pl.pallas_call(
        _mlaprep_sample_kernel,
        grid=(1,),
        in_specs=[_const_spec(low)] + [_const_spec(a) for a in params],
        out_specs=[full((n, MLA_HEADS * KV_LORA)), full((n, MLA_HEADS * HEAD_SLOT)), full((n, KV_LORA)),
                   full((n, QK_ROPE))],
        out_shape=[jax.ShapeDtypeStruct((n, MLA_HEADS * KV_LORA), F32),
                   jax.ShapeDtypeStruct((n, MLA_HEADS * HEAD_SLOT), F32),
                   jax.ShapeDtypeStruct((n, KV_LORA), F32), jax.ShapeDtypeStruct((n, QK_ROPE), F32)],
        compiler_params=_cparams("arbitrary"),
        name="mlaprep_sample",
    )(low, *params)


def _attn_kernel(qi_ref, ki_ref, q_ref, k_ref, v_ref, o_ref, m_scr, l_scr, acc_scr):
    step = pl.program_id(2)
    qi = qi_ref[step]
    ki = ki_ref[step]
    bq = q_ref.shape[0]
    bk = k_ref.shape[0]
    on_diagonal = ki == (qi * bq) // bk

    @pl.when(ki == 0)
    def _():
        m_scr[...] = jnp.full(m_scr.shape, -jnp.inf, F32)
        l_scr[...] = jnp.zeros_like(l_scr)
        acc_scr[...] = jnp.zeros_like(acc_scr)

    def accumulate(masked):
        v = v_ref[...]
        if masked:
            visible = ki * bk + _iota((bq, bk), 1) <= qi * bq + _iota((bq, bk), 0)
        for h in range(2):
            q = q_ref[:, HEAD_SLOT * h:HEAD_SLOT * (h + 1)]
            k = k_ref[:, HEAD_SLOT * h:HEAD_SLOT * (h + 1)]
            s = _mm_t(q, k)
            if masked:
                s = jnp.where(visible, s, -jnp.inf)
            m_prev = m_scr[h]
            m_new = jnp.maximum(m_prev, jnp.max(s, axis=1, keepdims=True))
            p = jnp.exp2(s - m_new)
            alpha = jnp.exp2(m_prev - m_new)
            l_scr[h] = alpha * l_scr[h] + jnp.sum(p, axis=1, keepdims=True)
            acc_scr[h] = alpha * acc_scr[h] + _mm(p, v)
            m_scr[h] = m_new

    @pl.when(jnp.logical_not(on_diagonal))
    def _():
        accumulate(False)

    @pl.when(on_diagonal)
    def _():
        accumulate(True)
        head0 = _iota((1, LANES), 1) < V_HEAD
        o_ref[...] = jnp.where(head0, acc_scr[0] / l_scr[0], acc_scr[1] / l_scr[1])


def _attention_prompt(q, k, v, bq, bk):
    bsz, t, _ = q.shape
    assert bk % bq == 0
    n_pairs = MLA_HEADS // 2
    pairs = [(qi, ki) for qi in range(t // bq) for ki in range((qi * bq) // bk + 1)]
    qi_of = jnp.asarray([p[0] for p in pairs], jnp.int32)
    ki_of = jnp.asarray([p[1] for p in pairs], jnp.int32)
    grid_spec = pltpu.PrefetchScalarGridSpec(
        num_scalar_prefetch=2,
        grid=(bsz, n_pairs, len(pairs)),
        in_specs=[pl.BlockSpec((None, bq, 2 * HEAD_SLOT), lambda b, p, s, qi, ki: (b, qi[s], p)),
                  pl.BlockSpec((None, bk, 2 * HEAD_SLOT), lambda b, p, s, qi, ki: (b, ki[s], p)),
                  pl.BlockSpec((None, bk, LANES), lambda b, p, s, qi, ki: (b, ki[s], p))],
        out_specs=pl.BlockSpec((None, bq, LANES), lambda b, p, s, qi, ki: (b, qi[s], p)),
        scratch_shapes=[pltpu.VMEM((2, bq, 1), F32), pltpu.VMEM((2, bq, 1), F32), pltpu.VMEM((2, bq, LANES), F32)],
    )
    return pl.pallas_call(
        _attn_kernel,
        grid_spec=grid_spec,
        out_shape=jax.ShapeDtypeStruct((bsz, t, MLA_HEADS * V_HEAD), F32),
        compiler_params=_cparams("parallel", "parallel", "arbitrary"),
        name="attn_prompt",
    )(qi_of, ki_of, q, k, v)


def _sample_attn_kernel(pt_ref, qlat_ref, qr_ref, cn_ref, kn_ref, ckv_hbm, kr_hbm, o_ref, kbuf, rbuf, s_scr, sem):
    b = pl.program_id(0)
    n_seq = pl.num_programs(0)
    n_pages = kbuf.shape[1]
    slot = lax.rem(b, 2)

    def page_copies(seq, slt):
        out = []
        for pg in range(n_pages):
            page = pt_ref[seq, pg]
            out.append(pltpu.make_async_copy(ckv_hbm.at[page], kbuf.at[slt, pg], sem.at[slt, 0]))
            out.append(pltpu.make_async_copy(kr_hbm.at[page], rbuf.at[slt, pg], sem.at[slt, 1]))
        return out

    @pl.when(b == 0)
    def _():
        for cp in page_copies(0, 0):
            cp.start()

    @pl.when(b + 1 < n_seq)
    def _():
        for cp in page_copies(b + 1, 1 - slot):
            cp.start()

    for cp in page_copies(b, slot):
        cp.wait()

    ql = qlat_ref[...]
    qr = qr_ref[:, :QK_ROPE]
    n_chunks = n_pages // 2
    rows = 2 * PAGE_SIZE

    def score(i, carry):
        kc = kbuf[slot, pl.ds(2 * i, 2)].reshape(rows, KV_LORA)
        rc = jnp.concatenate([rbuf[slot, 2 * i], rbuf[slot, 2 * i + 1]], axis=1)
        s_scr[i] = _mm_t(ql, kc) + jnp.dot(qr, rc, preferred_element_type=F32)
        return carry

    lax.fori_loop(0, n_chunks, score, 0, unroll=SAMPLE_ATTN_UNROLL)
    cn = cn_ref[...]
    kn = kn_ref[...]
    s_new = jnp.sum(ql * cn, axis=-1, keepdims=True) + jnp.sum(qr * kn, axis=-1, keepdims=True)
    s_all = s_scr[...]
    m = jnp.maximum(jnp.max(jnp.max(s_all, axis=0), axis=-1, keepdims=True), s_new)
    p_all = jnp.exp(s_all - m)
    p_new = jnp.exp(s_new - m)
    denom = jnp.sum(jnp.sum(p_all, axis=0), axis=-1, keepdims=True) + p_new
    s_scr[...] = p_all

    def accumulate(i, acc):
        kc = kbuf[slot, pl.ds(2 * i, 2)].reshape(rows, KV_LORA)
        return acc + _mm(s_scr[i], kc)

    acc = lax.fori_loop(0, n_chunks, accumulate, jnp.zeros((ql.shape[0], KV_LORA), F32), unroll=SAMPLE_ATTN_UNROLL)
    o_ref[...] = (acc + p_new * cn) / denom


def _attention_sample(page_table, qlat, qr, ckv_new, kr_new, cache_ckv, cache_kr):
    n, n_pages = page_table.shape
    grid_spec = pltpu.PrefetchScalarGridSpec(
        num_scalar_prefetch=1,
        grid=(n,),
        in_specs=[pl.BlockSpec((None, MLA_HEADS, KV_LORA), lambda b, pt: (b, 0, 0)),
                  pl.BlockSpec((None, MLA_HEADS, HEAD_SLOT), lambda b, pt: (b, 0, 0)),
                  pl.BlockSpec((None, 1, KV_LORA), lambda b, pt: (b, 0, 0)),
                  pl.BlockSpec((None, 1, QK_ROPE), lambda b, pt: (b, 0, 0)),
                  pl.BlockSpec(memory_space=pl.ANY),
                  pl.BlockSpec(memory_space=pl.ANY)],
        out_specs=pl.BlockSpec((None, MLA_HEADS, KV_LORA), lambda b, pt: (b, 0, 0)),
        scratch_shapes=[pltpu.VMEM((2, n_pages, PAGE_SIZE, KV_LORA), F32),
                        pltpu.VMEM((2, n_pages, QK_ROPE, PAGE_SIZE), F32),
                        pltpu.VMEM((n_pages // 2, MLA_HEADS, 2 * PAGE_SIZE), F32),
                        pltpu.SemaphoreType.DMA((2, 2))],
    )
    return pl.pallas_call(
        _sample_attn_kernel,
        grid_spec=grid_spec,
        out_shape=jax.ShapeDtypeStruct((n, MLA_HEADS, KV_LORA), F32),
        compiler_params=_cparams("arbitrary"),
        name="attn_sample",
    )(page_table, qlat.reshape(n, MLA_HEADS, KV_LORA), qr.reshape(n, MLA_HEADS, HEAD_SLOT),
      ckv_new.reshape(n, 1, KV_LORA), kr_new.reshape(n, 1, QK_ROPE), cache_ckv, cache_kr)


def _uv_kernel(ol_ref, wuv_ref, o_ref):
    outs = [_mm(ol_ref[:, KV_LORA * h:KV_LORA * (h + 1)], wuv_ref[:, V_HEAD * h:V_HEAD * (h + 1)])
            for h in range(MLA_HEADS)]
    o_ref[...] = jnp.concatenate(outs, axis=1)


def _latent_to_heads(o_lat, w_uv):
    n = o_lat.shape[0]
    o_lat = o_lat.reshape(n, MLA_HEADS * KV_LORA)
    return pl.pallas_call(
        _uv_kernel,
        grid=(1,),
        in_specs=[_const_spec(o_lat), _const_spec(w_uv)],
        out_specs=pl.BlockSpec((n, MLA_HEADS * V_HEAD), lambda i: (0, 0)),
        out_shape=jax.ShapeDtypeStruct((n, MLA_HEADS * V_HEAD), F32),
        compiler_params=_cparams("arbitrary"),
        name="latent_to_heads",
    )(o_lat, w_uv)


def _layernorm(z, g, b):
    mu = jnp.mean(z, axis=-1, keepdims=True)
    d = z - mu
    var = jnp.mean(d * d, axis=-1, keepdims=True)
    return d * lax.rsqrt(var + LN_EPS) * g + b


def _first_max(x, idx, sentinel):
    mx = jnp.max(x, axis=0, keepdims=True)
    return jnp.min(jnp.where(x == mx, idx, sentinel), axis=0, keepdims=True)


def _route(scores, bias):
    n_tok = scores.shape[1]
    per_group = N_EXPERTS // N_GROUPS
    sb = scores + bias
    sb3 = sb.reshape(N_GROUPS, per_group, n_tok)
    member = _iota(sb3.shape, 1)
    m1 = jnp.max(sb3, axis=1, keepdims=True)
    first = jnp.min(jnp.where(sb3 == m1, member, per_group), axis=1, keepdims=True)
    m2 = jnp.max(jnp.where(member == first, -jnp.inf, sb3), axis=1, keepdims=True)
    g_score = (m1 + m2).reshape(N_GROUPS, n_tok)
    g_idx = _iota(g_score.shape, 0)
    g_sel = jnp.zeros(g_score.shape, F32)
    for _ in range(TOPK_GROUPS):
        hit = g_idx == _first_max(g_score, g_idx, N_GROUPS)
        g_sel = jnp.where(hit, 1.0, g_sel)
        g_score = jnp.where(hit, -jnp.inf, g_score)
    e_mask = jnp.broadcast_to(g_sel.reshape(N_GROUPS, 1, n_tok), sb3.shape).reshape(N_EXPERTS, n_tok) > 0.5
    cur = jnp.where(e_mask, sb, -jnp.inf)
    e_idx = _iota(cur.shape, 0)
    sel = jnp.zeros(cur.shape, F32)
    for _ in range(TOP_K):
        hit = e_idx == _first_max(cur, e_idx, N_EXPERTS)
        sel = jnp.where(hit, 1.0, sel)
        cur = jnp.where(hit, -jnp.inf, cur)
    picked = jnp.where(sel > 0.5, scores, 0.0)
    return picked / jnp.sum(picked, axis=0, keepdims=True) * ROUTED_SCALE


def _merge_kernel(x_ref, oa_ref, ob_ref, gs_ref, gt1_ref, sh2_ref, sc2_ref, wba_ref, wbb_ref, wout_ref, g1_ref, b1_ref,
                  wrt_ref, br_ref, x1_ref, h2_ref, cw_ref):
    d = D_MODEL
    ya = _mm(oa_ref[...], wba_ref[...])
    yb = _mm(ob_ref[...], wbb_ref[...])
    merged = gs_ref[:, :d] * ya + gs_ref[:, d:] * yb
    z = DN_ALPHA * x_ref[...] + gt1_ref[...] * _mm(merged, wout_ref[...])
    x1 = _layernorm(z, g1_ref[...], b1_ref[...])
    x1_ref[...] = x1
    h2 = x1 * (1 + sc2_ref[...]) + sh2_ref[...]
    h2_ref[...] = h2.astype(BF16)
    scores = jax.nn.sigmoid(_mm3_t(wrt_ref[...], h2))
    cw_ref[...] = _route(scores, br_ref[...]).T


def _merge(x3, oa, ob, gs, mod, wts, tile):
    bsz, t, d = x3.shape
    params = [wts["w_branch_a"], wts["w_branch_b"], wts["w_out"], wts["ln1_g"], wts["ln1_b"], wts["w_router_t"],
              wts["b_router"]]
    row = lambda w: pl.BlockSpec((None, tile, w), lambda b, i: (b, i, 0))
    return pl.pallas_call(
        _merge_kernel,
        grid=(bsz, t // tile),
        in_specs=[row(d), row(RW_WIDTH), row(MLA_HEADS * V_HEAD), row(2 * d), mod.spec(2), mod.spec(3), mod.spec(4)]
        + [_const_spec(a) for a in params],
        out_specs=[row(d), row(d), row(N_EXPERTS)],
        out_shape=[jax.ShapeDtypeStruct((bsz, t, d), F32), jax.ShapeDtypeStruct((bsz, t, d), BF16),
                   jax.ShapeDtypeStruct((bsz, t, N_EXPERTS), F32)],
        compiler_params=_cparams("parallel", "arbitrary"),
        name="merge",
    )(x3, oa, ob, gs, mod.arr, mod.arr, mod.arr, *params)


def _experts_kernel(h_ref, cw_ref, wg_ref, wu_ref, wd_ref, o_ref):
    step = pl.program_id(1)
    n_e = wg_ref.shape[0]

    @pl.when(step == 0)
    def _():
        o_ref[...] = jnp.zeros_like(o_ref)

    x = h_ref[...]
    cw = cw_ref[...]
    lane = _iota((1, N_EXPERTS), 1)
    acts = []
    for e in range(n_e):
        act = _silu(_mm(x, wg_ref[e])) * _mm(x, wu_ref[e])
        w_col = jnp.sum(jnp.where(lane == step * n_e + e, cw, 0.0), axis=1, keepdims=True)
        acts.append((act * w_col).astype(BF16))
    o_ref[...] += _mm(jnp.concatenate(acts, axis=1), wd_ref[...].reshape(n_e * EXPERT_FF, -1))


def _experts(h2, cw, wts, tile):
    n, d = h2.shape
    n_e = EXPERTS_PER_STEP
    return pl.pallas_call(
        _experts_kernel,
        grid=(n // tile, N_EXPERTS // n_e),
        in_specs=[pl.BlockSpec((tile, d), lambda i, e: (i, 0)),
                  pl.BlockSpec((tile, N_EXPERTS), lambda i, e: (i, 0)),
                  pl.BlockSpec((n_e, d, EXPERT_FF), lambda i, e: (e, 0, 0)),
                  pl.BlockSpec((n_e, d, EXPERT_FF), lambda i, e: (e, 0, 0)),
                  pl.BlockSpec((n_e, EXPERT_FF, d), lambda i, e: (e, 0, 0))],
        out_specs=pl.BlockSpec((tile, d), lambda i, e: (i, 0)),
        out_shape=jax.ShapeDtypeStruct((n, d), F32),
        compiler_params=_cparams("parallel", "arbitrary"),
        name="experts",
    )(h2, cw, wts["w_exp_gate"], wts["w_exp_up"], wts["w_exp_down"])


def _final_kernel(x1_ref, h2_ref, routed_ref, gt2_ref, wsg_ref, wsu_ref, wsd_ref, g2_ref, b2_ref, y_ref):
    h2 = h2_ref[...]
    act = _silu(jnp.dot(h2, wsg_ref[...], preferred_element_type=F32)) * jnp.dot(h2, wsu_ref[...],
                                                                               preferred_element_type=F32)
    ffn = routed_ref[...] + _mm(act, wsd_ref[...])
    y_ref[...] = _layernorm(DN_ALPHA * x1_ref[...] + gt2_ref[...] * ffn, g2_ref[...], b2_ref[...])


def _final(x1, h2, routed, mod, wts, tile):
    bsz, t, d = x1.shape
    params = [wts["w_sh_gate"], wts["w_sh_up"], wts["w_sh_down"], wts["ln2_g"], wts["ln2_b"]]
    row = lambda w: pl.BlockSpec((None, tile, w), lambda b, i: (b, i, 0))
    return pl.pallas_call(
        _final_kernel,
        grid=(bsz, t // tile),
        in_specs=[row(d), row(d), row(d), mod.spec(5)] + [_const_spec(a) for a in params],
        out_specs=row(d),
        out_shape=jax.ShapeDtypeStruct((bsz, t, d), F32),
        compiler_params=_cparams("parallel", "arbitrary"),
        name="final",
    )(x1, h2, routed, mod.arr, *params)


def _rope_rotation_columns(w):
    half = QK_ROPE // 2
    return jnp.concatenate([-w[..., half:], w[..., :half]], axis=-1)


def _prepare_weights(P):
    d = D_MODEL
    w = {}
    w_in = P["w_in"]
    o1 = RW_COLS
    o2 = o1 + Q_LORA
    o3 = o2 + KV_LORA
    o4 = o3 + QK_ROPE
    w_kr = w_in[:, o3:o4]
    pad = jnp.zeros((d, LANES - QK_ROPE), F32)
    w["w_rw"] = w_in[:, :o1].astype(BF16)
    w["w_mla"] = jnp.concatenate([w_in[:, o1:o3], w_kr, pad, _rope_rotation_columns(w_kr), pad], axis=1).astype(BF16)
    w["w_gate"] = w_in[:, o4:].astype(BF16)

    row = lambda v: v.reshape(1, -1)
    w["mu_shift"] = row(P["mu_shift"])
    zl = jnp.zeros((DECAY_LORA, RW_WIDTH), F32)
    w["w_lora"] = jnp.concatenate([jnp.concatenate([P["w_decay_up"], zl], axis=1),
                                   jnp.concatenate([zl, P["w_iclr_up"]], axis=1)], axis=0).astype(BF16)
    w["w_gate_up"] = P["w_gate_up"].astype(BF16)
    for nm in ("w_decay0", "a0", "k_k", "k_a", "r_k", "lnx_g", "lnx_b", "g_qnorm", "g_kvnorm", "ln1_g", "ln1_b",
               "ln2_g", "ln2_b"):
        w[nm] = row(P[nm])
    head_of = np.arange(RW_WIDTH) // RW_HEAD_DIM
    w["ones_blk"] = jnp.asarray(head_of[:, None] == head_of[None, :], BF16)

    wq = P["w_uq"].reshape(Q_LORA, MLA_HEADS, QK_NOPE + QK_ROPE)
    zq = jnp.zeros((Q_LORA, MLA_HEADS, HEAD_SLOT - QK_NOPE - QK_ROPE), F32)
    w["w_qa"] = jnp.concatenate([wq, zq], axis=-1).reshape(Q_LORA, MLA_HEADS * HEAD_SLOT).astype(BF16)
    w["w_qb"] = jnp.concatenate([jnp.zeros((Q_LORA, MLA_HEADS, QK_NOPE), F32), _rope_rotation_columns(wq[..., QK_NOPE:]), zq],
                                axis=-1).reshape(Q_LORA, MLA_HEADS * HEAD_SLOT).astype(BF16)
    wuk = P["w_uk"].reshape(KV_LORA, MLA_HEADS, QK_NOPE)
    zk = jnp.zeros((KV_LORA, MLA_HEADS, HEAD_SLOT - QK_NOPE), F32)
    w["w_uk_pad"] = jnp.concatenate([wuk, zk], axis=-1).reshape(KV_LORA, MLA_HEADS * HEAD_SLOT).astype(BF16)
    w["w_ukt_pad"] = jnp.transpose(jnp.concatenate([wuk, zk], axis=-1), (1, 2, 0)).astype(BF16)
    w["w_uv"] = P["w_uv"].astype(BF16)
    half = QK_ROPE // 2
    inv = ROPE_THETA ** (-jnp.arange(half, dtype=F32) / half)
    w["invf"] = jnp.concatenate([jnp.zeros((QK_NOPE,), F32), inv, inv,
                                 jnp.zeros((HEAD_SLOT - QK_NOPE - QK_ROPE,), F32)]).reshape(1, HEAD_SLOT)

    for nm in ("w_branch_a", "w_branch_b", "w_out", "w_sh_gate", "w_sh_up", "w_sh_down"):
        w[nm] = P[nm].astype(BF16)
    for nm in ("w_exp_gate", "w_exp_up", "w_exp_down"):
        w[nm] = P[nm]
    w["w_router_t"] = P["w_router"].T
    w["b_router"] = P["b_router"].reshape(N_EXPERTS, 1)
    return w


def _row_tile(t, cap):
    tile = min(t, cap)
    assert t % tile == 0 and tile % 16 == 0, (t, tile)
    return tile


def _layer_tail(x3, oa, ob, gs, mod, wts, tile, moe_tile):
    bsz, t, d = x3.shape
    x1, h2, cw = _merge(x3, oa, ob, gs, mod, wts, tile)
    routed = _experts(h2.reshape(bsz * t, d), cw.reshape(bsz * t, N_EXPERTS), wts, moe_tile).reshape(bsz, t, d)
    return _final(x1, h2, routed, mod, wts, tile)


def _layer_prompt(x, mod_rows, wts):
    bsz, t, d = x.shape
    tile = _row_tile(t, 256)
    mod = _Mod(mod_rows.reshape(bsz * 6, 1, d), per_token=False, tile=tile)
    p_rw, low, gs = _inproj(x, mod, wts, tile)
    oa, state = _rw_scan(p_rw, jnp.zeros((bsz, 1, RW_COLS), F32), wts)
    q, k, v, ckv, k_rope = _mlaprep_prompt(low, wts, tile)
    ob = _attention_prompt(q, k, v, _row_tile(t, 512), _row_tile(t, 1024))
    y = _layer_tail(x, oa, ob, gs, mod, wts, _row_tile(t, 512), _row_tile(bsz * t, 1024))
    return y, ckv, k_rope, _unpack_state(state), p_rw[:, -1]


def _layer_sample(x, mod_rows, state, shift, page_table, cache_ckv, cache_kr, wts):
    n, s_new, d = x.shape
    assert s_new == 1
    past = page_table.shape[1] * PAGE_SIZE
    x3 = x.reshape(1, n, d)
    mod = _Mod(mod_rows, per_token=True, tile=n)
    p_rw, low, gs = _inproj(x3, mod, wts, n)
    p_rw2 = p_rw.reshape(n, RW_COLS)
    rw = dict(zip(RW_OUT_NAMES, _rwprep_sample(p_rw2, shift, wts)))
    new_state_t, oa = _rw_step(jnp.transpose(state, (1, 2, 3, 0)), rw, wts)
    new_state = jnp.transpose(new_state_t, (3, 0, 1, 2))
    pos = jnp.full((n, 1), past, F32)
    qlat, qr, ckv, k_rope = _mlaprep_sample(low.reshape(n, MLA_LOW_COLS), pos, wts)
    o_lat = _attention_sample(page_table, qlat, qr, ckv, k_rope, cache_ckv, jnp.swapaxes(cache_kr, 1, 2))
    ob = _latent_to_heads(o_lat, wts["w_uv"])
    y = _layer_tail(x3, oa.reshape(1, n, RW_WIDTH), ob.reshape(1, n, MLA_HEADS * V_HEAD), gs, mod, wts, n, n)
    return (y.reshape(n, 1, d), ckv.reshape(n, 1, KV_LORA), k_rope.reshape(n, 1, QK_ROPE), new_state, p_rw2)


def kernel(x_prompt, x_sample, c_prompt, c_sample, cache_ckv, cache_krope, state_wkv, state_shift, page_table, w_ada, b_ada, w_in, mu_shift, w_decay0, w_decay_up, a0, w_iclr_up, w_gate_up, k_k, k_a, r_k, lnx_g, lnx_b, w_branch_a, g_qnorm, w_uq, g_kvnorm, w_uk, w_uv, w_branch_b, w_out, ln1_g, ln1_b, w_router, b_router, w_exp_gate, w_exp_up, w_exp_down, w_sh_gate, w_sh_up, w_sh_down, ln2_g, ln2_b):
    params = dict(w_ada=w_ada, b_ada=b_ada, w_in=w_in, mu_shift=mu_shift, w_decay0=w_decay0, w_decay_up=w_decay_up,
                  a0=a0, w_iclr_up=w_iclr_up, w_gate_up=w_gate_up, k_k=k_k, k_a=k_a, r_k=r_k, lnx_g=lnx_g,
                  lnx_b=lnx_b, w_branch_a=w_branch_a, g_qnorm=g_qnorm, w_uq=w_uq, g_kvnorm=g_kvnorm, w_uk=w_uk,
                  w_uv=w_uv, w_branch_b=w_branch_b, w_out=w_out, ln1_g=ln1_g, ln1_b=ln1_b, w_router=w_router,
                  b_router=b_router, w_exp_gate=w_exp_gate, w_exp_up=w_exp_up, w_exp_down=w_exp_down,
                  w_sh_gate=w_sh_gate, w_sh_up=w_sh_up, w_sh_down=w_sh_down, ln2_g=ln2_g, ln2_b=ln2_b)
    depth = w_in.shape[0]
    bp = x_prompt.shape[0]
    bd = x_sample.shape[0]
    n_c = bp + bd
    c_all = jnp.concatenate([c_prompt, c_sample, jnp.zeros((-n_c % 8, D_MODEL), F32)], axis=0)
    yp, ys = x_prompt, x_sample
    outs = [[] for _ in range(8)]
    for l in range(depth):
        wts = _prepare_weights({name: arr[l] for name, arr in params.items()})
        mod = _adaln_mod(c_all, params["w_ada"][l], params["b_ada"][l].reshape(1, -1))
        yp, *rest_p = _layer_prompt(yp, mod[:bp], wts)
        ys, *rest_s = _layer_sample(ys, mod[bp:n_c], state_wkv[l], state_shift[l], page_table, cache_ckv[l],
                                    cache_krope[l], wts)
        for acc, val in zip(outs, rest_p + rest_s):
            acc.append(val)
    return (yp, ys) + tuple(jnp.stack(o) for o in outs)
```

```python
import math

import numpy as np
import jax
import jax.numpy as jnp
from jax import lax
from jax.experimental import pallas as pl
from jax.experimental.pallas import tpu as pltpu

F32 = jnp.float32
BF16 = jnp.bfloat16

D_MODEL = 1024
PAGE_SIZE = 128
RW_HEADS = 8
RW_HEAD_DIM = 64
RW_WIDTH = RW_HEADS * RW_HEAD_DIM
DECAY_LORA = 64
ICLR_LORA = 64
GATE_LORA = 128
RW_COLS = 3 * RW_WIDTH + DECAY_LORA + ICLR_LORA + GATE_LORA
GN_EPS = 64e-5
MLA_HEADS = 8
QK_NOPE = 64
QK_ROPE = 32
V_HEAD = 64
Q_LORA = 384
KV_LORA = 256
ROPE_THETA = 10000.0
ATTN_SCALE = (QK_NOPE + QK_ROPE) ** -0.5
N_EXPERTS = 64
TOP_K = 8
N_GROUPS = 8
TOPK_GROUPS = 4
EXPERT_FF = 256
SHARED_FF = 256
ROUTED_SCALE = 2.5
DEPTH = 1
DN_ALPHA = (2 * DEPTH) ** 0.25
LN_EPS = 1e-5
RMS_EPS = 1e-6

LANES = 128
HEAD_SLOT = LANES
MLA_LOW_COLS = Q_LORA + KV_LORA + 2 * LANES
SCAN_CHUNK = 64
SCAN_SUBCHUNKS = 4
SAMPLE_ATTN_UNROLL = 8
ATTN_HEADS_PER_STEP = 4
EXPERTS_PER_STEP = 4
VMEM_LIMIT = 56 * 1024 * 1024


def _cparams(*sem):
    return pltpu.CompilerParams(dimension_semantics=sem, vmem_limit_bytes=VMEM_LIMIT)


def _mm(a, b):
    return jnp.dot(a.astype(BF16), b.astype(BF16), preferred_element_type=F32)


def _mm_t(a, b):
    return lax.dot_general(a.astype(BF16), b.astype(BF16), (((1,), (1,)), ((), ())), preferred_element_type=F32)


def _mm_tl(a, b):
    return lax.dot_general(a.astype(BF16), b.astype(BF16), (((0,), (0,)), ((), ())), preferred_element_type=F32)


def _split(x):
    hi = x.astype(BF16)
    lo = (x - hi.astype(F32)).astype(BF16)
    return hi, lo


def _three_pass(f, a, b):
    ah, al = _split(a)
    bh, bl = _split(b)
    return f(ah, bh) + (f(ah, bl) + f(al, bh))


def _mm3(a, b):
    return _three_pass(_mm, a, b)


def _mm3_t(a, b):
    return _three_pass(_mm_t, a, b)


def _mm3_tl(a, b):
    return _three_pass(_mm_tl, a, b)


_mm_state = _mm
_mm_merge = _mm


def _mm_exact_rhs(a, b_exact):
    ah, al = _split(a)
    return _mm(ah, b_exact) + _mm(al, b_exact)


def _silu(x):
    return x * jax.nn.sigmoid(x)


def _iota(shape, dim):
    return lax.broadcasted_iota(jnp.int32, shape, dim)


def _mod_kernel(c_ref, w_ref, b_ref, o_ref):
    o_ref[...] = _mm(_silu(c_ref[...]), w_ref[...]) + b_ref[...]


def _adaln_mod(c_all, w_ada, b_ada):
    n = c_all.shape[0]
    d = D_MODEL
    return pl.pallas_call(
        _mod_kernel,
        grid=(6,),
        in_specs=[pl.BlockSpec((n, d), lambda j: (0, 0)),
                  pl.BlockSpec((d, d), lambda j: (0, j)),
                  pl.BlockSpec((1, d), lambda j: (0, j))],
        out_specs=pl.BlockSpec((n, d), lambda j: (0, j)),
        out_shape=jax.ShapeDtypeStruct((n, 6 * d), F32),
        compiler_params=_cparams("arbitrary"),
        name="adaln_mod",
    )(c_all, w_ada, b_ada)


class _Mod:
    def __init__(self, arr, per_token, tile):
        self.arr = arr
        self.per_token = per_token
        self.tile = tile

    def spec(self, j):
        if self.per_token:
            return pl.BlockSpec((self.tile, D_MODEL), lambda b, i, *_: (i, j))
        return pl.BlockSpec((None, 1, D_MODEL), lambda b, i, *_: (b * 6 + j, 0, 0))


def _const_spec(arr):
    nd = arr.ndim
    return pl.BlockSpec(arr.shape, lambda *_: (0,) * nd)


def _inproj_kernel(x_ref, sh_ref, sc_ref, wrw_ref, wmla_ref, wg_ref, prw_ref, mla_ref, gs_ref):
    h = (x_ref[...] * (1 + sc_ref[...]) + sh_ref[...]).astype(BF16)
    prw_ref[...] = jnp.dot(h, wrw_ref[...], preferred_element_type=F32)
    mla_ref[...] = jnp.dot(h, wmla_ref[...], preferred_element_type=F32)
    gs_ref[...] = jax.nn.sigmoid(jnp.dot(h, wg_ref[...], preferred_element_type=F32))


def _inproj(x3, mod, wts, tile):
    bsz, t, d = x3.shape
    row = lambda w: pl.BlockSpec((None, tile, w), lambda b, i: (b, i, 0))
    return pl.pallas_call(
        _inproj_kernel,
        grid=(bsz, t // tile),
        in_specs=[row(d), mod.spec(0), mod.spec(1),
                  _const_spec(wts["w_rw"]), _const_spec(wts["w_mla"]), _const_spec(wts["w_gate"])],
        out_specs=[row(RW_COLS), row(MLA_LOW_COLS), row(2 * d)],
        out_shape=[jax.ShapeDtypeStruct((bsz, t, RW_COLS), F32),
                   jax.ShapeDtypeStruct((bsz, t, MLA_LOW_COLS), F32),
                   jax.ShapeDtypeStruct((bsz, t, 2 * d), F32)],
        compiler_params=_cparams("parallel", "arbitrary"),
        name="inproj",
    )(x3, mod.arr, mod.arr, wts["w_rw"], wts["w_mla"], wts["w_gate"])


RW_OUT_NAMES = ("r", "kmod", "v", "kkn", "bvec", "logw", "bonus", "g")


def _rw_elementwise(p, pp, mu, w_lora, w_gate_up, w_decay0, a0, k_k, k_a, r_k, ones_blk):
    w = RW_WIDTH
    pm = p + (pp - p) * mu
    r, k, v = pm[:, 0:w], pm[:, w:2 * w], pm[:, 2 * w:3 * w]
    xwa = pm[:, 3 * w:3 * w + DECAY_LORA + ICLR_LORA]
    xg = pm[:, 3 * w + DECAY_LORA + ICLR_LORA:]
    lane = _iota((1, DECAY_LORA + ICLR_LORA), 1)
    z = jnp.where(lane < DECAY_LORA, jnp.tanh(xwa), xwa)
    lora = _mm(z, w_lora)
    y = -(w_decay0 + lora[:, :w])
    softplus = jnp.maximum(y, 0.0) + jnp.log1p(jnp.exp(-jnp.abs(y)))
    w_log = -softplus - 0.5
    logw = -jnp.exp(w_log)
    a = jax.nn.sigmoid(a0 + lora[:, w:])
    g = _mm(jax.nn.sigmoid(xg), w_gate_up)
    kk = k * k_k
    kk = kk * lax.rsqrt(_mm_exact_rhs(kk * kk, ones_blk) + 1e-12)
    kmod = k * (1.0 + (a - 1.0) * k_a)
    bonus = _mm_exact_rhs(r * kmod * r_k, ones_blk) * v
    return r, kmod, v, -kk, kk * a, logw, bonus, g


def _rwprep_sample_kernel(p_ref, pp_ref, mu_ref, wl_ref, wgu_ref, wd0_ref, a0_ref, kk_ref, ka_ref, rk_ref,
                          ones_ref, *out_refs):
    outs = _rw_elementwise(p_ref[...], pp_ref[...], mu_ref[...], wl_ref[...], wgu_ref[...], wd0_ref[...], a0_ref[...],
                           kk_ref[...], ka_ref[...], rk_ref[...], ones_ref[...])
    for o_ref, val in zip(out_refs, outs):
        o_ref[...] = val


def _rw_param_list(wts):
    return [wts["mu_shift"], wts["w_lora"], wts["w_gate_up"], wts["w_decay0"], wts["a0"], wts["k_k"], wts["k_a"],
            wts["r_k"], wts["ones_blk"]]


def _rwprep_sample(p_rw, p_prev, wts):
    n = p_rw.shape[0]
    params = _rw_param_list(wts)
    return pl.pallas_call(
        _rwprep_sample_kernel,
        grid=(1,),
        in_specs=[_const_spec(p_rw), _const_spec(p_prev)] + [_const_spec(a) for a in params],
        out_specs=[pl.BlockSpec((n, RW_WIDTH), lambda i: (0, 0))] * 8,
        out_shape=[jax.ShapeDtypeStruct((n, RW_WIDTH), F32)] * 8,
        compiler_params=_cparams("arbitrary"),
        name="rwprep_sample",
    )(p_rw, p_prev, *params)


def _group_norm_pair(o, ones_pair):
    mu = _mm_exact_rhs(o, ones_pair) * (1.0 / RW_HEAD_DIM)
    d = o - mu
    var = _mm_exact_rhs(d * d, ones_pair) * (1.0 / RW_HEAD_DIM)
    return d * lax.rsqrt(var + GN_EPS)


def _scan_kernel(p_ref, s0_ref, mu_ref, wl_ref, wgu_ref, wd0_ref, a0_ref, kk_ref, ka_ref, rk_ref, ones_ref,
                 lng_ref, lnb_ref, o_ref, st_ref, h_scr, prev_scr):
    c = pl.program_id(1)
    n_steps = pl.num_programs(1)
    n_pairs = h_scr.shape[0]
    cl = SCAN_CHUNK
    n_rows = p_ref.shape[0]
    n_sub = n_rows // cl

    @pl.when(c == 0)
    def _():
        h_scr[...] = jnp.zeros_like(h_scr)
        prev_scr[...] = s0_ref[...]

    p = p_ref[...]
    pp = jnp.where(_iota(p.shape, 0) == 0, prev_scr[...], pltpu.roll(p, 1, 0))
    prev_scr[...] = p[n_rows - 1:n_rows, :]
    r_all, k_all, v_all, a_all, b_all, lw_all, bon_all, g_all = _rw_elementwise(
        p, pp, mu_ref[...], wl_ref[...], wgu_ref[...], wd0_ref[...], a0_ref[...], kk_ref[...], ka_ref[...],
        rk_ref[...], ones_ref[...])

    row = _iota((cl, cl), 0)
    col = _iota((cl, cl), 1)
    tri_incl = row >= col
    tri_strict = row > col
    tri_ones = tri_incl.astype(BF16)
    eye = (row == col).astype(F32)
    head0 = _iota((1, LANES), 1) < RW_HEAD_DIM
    head_mask = (head0, jnp.logical_not(head0))
    head0_2 = (_iota((1, 2 * LANES), 1) & RW_HEAD_DIM) == 0
    r128 = _iota((LANES, LANES), 0)
    c128 = _iota((LANES, LANES), 1)
    same_head = (r128 < RW_HEAD_DIM) == (c128 < RW_HEAD_DIM)
    diag = r128 == c128
    ones_pair = same_head.astype(BF16)
    units = [(s, p) for s in range(n_sub) for p in range(n_pairs)]
    heads = [(s, p, h) for (s, p) in units for h in range(2)]
    rows = {s: slice(cl * s, cl * (s + 1)) for s in range(n_sub)}
    lanes = {p: slice(LANES * p, LANES * (p + 1)) for p in range(n_pairs)}
    at = lambda ref, u: ref[rows[u[0]], lanes[u[1]]]

    lw = {u: at(lw_all, u) for u in units}
    lg = {}
    for u in units:
        l1 = lw[u].astype(BF16)
        rem = lw[u] - l1.astype(F32)
        l2 = rem.astype(BF16)
        l3 = (rem - l2.astype(F32)).astype(BF16)
        lg[u] = _mm(tri_ones, l1) + (_mm(tri_ones, l2) + _mm(tri_ones, l3))
    vv = {u: at(v_all, u) for u in units}
    rh, ah, bh, kh, bt, kt, g_last = {}, {}, {}, {}, {}, {}, {}
    for u in units:
        bb = at(b_all, u)
        kk = at(k_all, u)
        lg_last = lg[u][cl - 1:cl, :]
        rh[u] = at(r_all, u) * jnp.exp(lg[u])
        ah[u] = at(a_all, u) * jnp.exp(lg[u] - lw[u])
        inv = jnp.exp(-lg[u])
        bh[u] = bb * inv
        kh[u] = kk * inv
        tail = jnp.exp(lg_last - lg[u])
        bt[u] = bb * tail
        kt[u] = kk * tail
        g_last[u] = jnp.exp(lg_last)

    xb, xk = {}, {}
    for (s, p, h) in heads:
        ar_h = jnp.where(head_mask[h], jnp.concatenate([ah[s, p], rh[s, p]], axis=0), 0.0)
        xb[s, p, h] = _mm_t(ar_h, bh[s, p])
        xk[s, p, h] = _mm_t(ar_h, kh[s, p])
    l_ab = {k: jnp.where(tri_strict, x[:cl], 0.0) for k, x in xb.items()}
    m_rb = {k: jnp.where(tri_incl, x[cl:], 0.0) for k, x in xb.items()}
    l_ak = {k: jnp.where(tri_strict, x[:cl], 0.0) for k, x in xk.items()}
    m_rk = {k: jnp.where(tri_incl, x[cl:], 0.0) for k, x in xk.items()}
    lv_h = {k: _mm(l_ak[k], vv[k[:2]]) for k in heads}
    mrkv_h = {k: _mm(m_rk[k], vv[k[:2]]) for k in heads}
    ktv = {u: _mm_tl(kt[u], vv[u]) for u in units}

    base_bits = 3
    blk = lambda bits: (row >> bits) == (col >> bits)
    sum_half = _iota((1, 2 * cl), 1) >= cl
    x = {k: jnp.concatenate([jnp.where(blk(base_bits), l_ab[k], 0.0), eye], axis=1) for k in heads}
    for _ in range(base_bits):
        nxt = {}
        for k in heads:
            xh, xl = _split(x[k])
            ph, plo = xh[:, :cl], xl[:, :cl]
            nxt[k] = (_mm(ph, xh) + (_mm(ph, xl) + _mm(plo, xh))) + jnp.where(sum_half, x[k], 0.0)
        x = nxt
    t_inv = {k: x[k][:, cl:] for k in heads}
    for bits in range(base_bits, int(math.log2(cl))):
        lower_left = jnp.logical_and(blk(bits + 1), jnp.logical_not(blk(bits)))
        y = {k: _mm_merge(jnp.where(lower_left, l_ab[k], 0.0), t_inv[k]) for k in heads}
        t_inv = {k: t_inv[k] + _mm_merge(t_inv[k], y[k]) for k in heads}

    pick = lambda d, u: jnp.where(head0, d[u + (0,)], d[u + (1,)])
    pick2 = lambda d, u: jnp.where(head0_2, d[u + (0,)], d[u + (1,)])
    z = {u: jnp.concatenate([ah[u], pick(lv_h, u)], axis=1) for u in units}
    tz = {k: _mm(t_inv[k], z[k[:2]]) for k in heads}
    w12 = {u: pick2(tz, u) for u in units}
    q12_h = {k: _mm(m_rb[k], w12[k[:2]]) for k in heads}
    g12 = {u: _mm_tl(bt[u], w12[u]) for u in units}
    q1, q2, g1, g2 = {}, {}, {}, {}
    for u in units:
        q12 = pick2(q12_h, u)
        q1[u] = rh[u] + q12[:, :LANES]
        q2[u] = q12[:, LANES:] + pick(mrkv_h, u)
        g1[u] = jnp.where(same_head, g12[u][:, :LANES], 0.0) + jnp.where(diag, g_last[u], 0.0)
        g2[u] = jnp.where(same_head, g12[u][:, LANES:] + ktv[u], 0.0)

    state = {p: h_scr[p] for p in range(n_pairs)}
    outs = {}
    for s in range(n_sub):
        for p in range(n_pairs):
            outs[s, p] = _mm_state(q1[s, p], state[p]) + q2[s, p]
        state = {p: _mm_state(g1[s, p], state[p]) + g2[s, p] for p in range(n_pairs)}
    for p in range(n_pairs):
        h_scr[p] = state[p]

    for u in units:
        y = _group_norm_pair(outs[u], ones_pair) * lng_ref[:, lanes[u[1]]] + lnb_ref[:, lanes[u[1]]]
        o_ref[rows[u[0]], lanes[u[1]]] = (y + at(bon_all, u)) * at(g_all, u)

    @pl.when(c == n_steps - 1)
    def _():
        st_ref[...] = h_scr[...]


def _rw_scan(p_rw, shift0, wts):
    bsz, t, _ = p_rw.shape
    w = RW_WIDTH
    n_pairs = w // LANES
    step_rows = SCAN_CHUNK * SCAN_SUBCHUNKS
    assert t % step_rows == 0, (t, step_rows)
    blk = lambda width: pl.BlockSpec((None, step_rows, width), lambda b, c: (b, c, 0))
    params = _rw_param_list(wts) + [wts["lnx_g"], wts["lnx_b"]]
    return pl.pallas_call(
        _scan_kernel,
        grid=(bsz, t // step_rows),
        in_specs=[blk(RW_COLS), pl.BlockSpec((None, 1, RW_COLS), lambda b, c: (b, 0, 0))]
        + [_const_spec(a) for a in params],
        out_specs=[blk(w), pl.BlockSpec((None, n_pairs, LANES, LANES), lambda b, c: (b, 0, 0, 0))],
        out_shape=[jax.ShapeDtypeStruct((bsz, t, w), F32),
                   jax.ShapeDtypeStruct((bsz, n_pairs, LANES, LANES), F32)],
        scratch_shapes=[pltpu.VMEM((n_pairs, LANES, LANES), F32), pltpu.VMEM((1, RW_COLS), F32)],
        compiler_params=_cparams("parallel", "arbitrary"),
        name="rw_scan",
    )(p_rw, shift0, *params)


def _unpack_state(st):
    hd = RW_HEAD_DIM
    blocks = [st[:, p, j * hd:(j + 1) * hd, j * hd:(j + 1) * hd] for p in range(st.shape[1]) for j in range(2)]
    return jnp.swapaxes(jnp.stack(blocks, axis=1), -1, -2)


def _rwstep_kernel(s_ref, r_ref, k_ref, v_ref, a_ref, b_ref, lw_ref, bon_ref, g_ref, lng_ref, lnb_ref, so_ref, o_ref,
                   acc_scr, vt_scr):
    hd = RW_HEAD_DIM
    vt_scr[...] = v_ref[...].T
    decay2 = jnp.exp(lw_ref[...]).T
    kkn2 = a_ref[...].T
    kka2 = b_ref[...].T
    kmod2 = k_ref[...].T
    r2 = r_ref[...].T
    normed = []
    for h in range(2):
        ch = slice(hd * h, hd * (h + 1))
        decay, kkn, kka, kmod, r = decay2[ch], kkn2[ch], kka2[ch], kmod2[ch], r2[ch]

        def body(i, carry):
            s = s_ref[h, i]
            sa = jnp.sum(s * kkn, axis=0, keepdims=True)
            s_new = s * decay + sa * kka + vt_scr[pl.ds(hd * h + i, 1), :] * kmod
            so_ref[h, i] = s_new
            acc_scr[pl.ds(hd * h + i, 1), :] = jnp.sum(s_new * r, axis=0, keepdims=True)
            return carry

        lax.fori_loop(0, hd, body, 0, unroll=8)
        o = acc_scr[ch, :]
        mu = jnp.mean(o, axis=0, keepdims=True)
        d = o - mu
        var = jnp.mean(d * d, axis=0, keepdims=True)
        normed.append(d * lax.rsqrt(var + GN_EPS))
    y = jnp.concatenate(normed, axis=0).T * lng_ref[...] + lnb_ref[...]
    o_ref[...] = (y + bon_ref[...]) * g_ref[...]


def _rw_step(state_t, rw, wts):
    n = state_t.shape[-1]
    hd = RW_HEAD_DIM
    vec = pl.BlockSpec((n, 2 * hd), lambda p: (0, p))
    par = pl.BlockSpec((1, 2 * hd), lambda p: (0, p))
    st = pl.BlockSpec((2, hd, hd, n), lambda p: (p, 0, 0, 0))
    ins = [rw[nm] for nm in ("r", "kmod", "v", "kkn", "bvec", "logw", "bonus", "g")]
    return pl.pallas_call(
        _rwstep_kernel,
        grid=(RW_HEADS // 2,),
        in_specs=[st] + [vec] * 8 + [par, par],
        out_specs=[st, vec],
        out_shape=[jax.ShapeDtypeStruct(state_t.shape, F32), jax.ShapeDtypeStruct((n, RW_WIDTH), F32)],
        scratch_shapes=[pltpu.VMEM((2 * hd, n), F32), pltpu.VMEM((2 * hd, n), F32)],
        compiler_params=_cparams("parallel"),
        name="rw_step",
    )(state_t, *ins, wts["lnx_g"], wts["lnx_b"])


def _rms(x, g):
    return x * lax.rsqrt(jnp.mean(x * x, axis=-1, keepdims=True) + RMS_EPS) * g


def _mla_common(low, pos, gq, gkv, wqa, wqb, invf, q_scale):
    cq = _rms(low[:, :Q_LORA], gq)
    ckv = _rms(low[:, Q_LORA:Q_LORA + KV_LORA], gkv)
    kr = low[:, Q_LORA + KV_LORA:Q_LORA + KV_LORA + LANES]
    kr_rot = low[:, Q_LORA + KV_LORA + LANES:]
    ang = pos * invf
    cos = jnp.cos(ang)
    sin = jnp.sin(ang)
    cos8 = jnp.concatenate([cos] * MLA_HEADS, axis=1)
    sin8 = jnp.concatenate([sin] * MLA_HEADS, axis=1)
    q = (_mm(cq, wqa) * cos8 + _mm(cq, wqb) * sin8) * q_scale
    k_rope = kr * pltpu.roll(cos, LANES - QK_NOPE, 1) + kr_rot * pltpu.roll(sin, LANES - QK_NOPE, 1)
    return q, ckv, k_rope


def _mlaprep_prompt_kernel(low_ref, gq_ref, gkv_ref, wqa_ref, wqb_ref, wuk_ref, wuv_ref, invf_ref,
                           q_ref, k_ref, v_ref, ckv_ref, kr_ref):
    i = pl.program_id(1)
    rows = low_ref.shape[0]
    pos = (i * rows + _iota((rows, 1), 0)).astype(F32)
    q, ckv, k_rope = _mla_common(low_ref[...], pos, gq_ref[...], gkv_ref[...], wqa_ref[...], wqb_ref[...],
                                 invf_ref[...], ATTN_SCALE * math.log2(math.e))
    q_ref[...] = q.astype(BF16)
    ckv_ref[...] = ckv
    kr_ref[...] = k_rope[:, :QK_ROPE]
    k_slot = pltpu.roll(k_rope, QK_NOPE, 1)
    k_ref[...] = (_mm(ckv, wuk_ref[...]) + jnp.concatenate([k_slot] * MLA_HEADS, axis=1)).astype(BF16)
    v_ref[...] = _mm_t(wuv_ref[...], ckv).astype(BF16)


def _mlaprep_prompt(low, wts, tile):
    bsz, t, _ = low.shape
    params = [wts["g_qnorm"], wts["g_kvnorm"], wts["w_qa"], wts["w_qb"], wts["w_uk_pad"], wts["w_uv_t"], wts["invf"]]
    row = lambda w: pl.BlockSpec((None, tile, w), lambda b, i: (b, i, 0))
    slots = MLA_HEADS * HEAD_SLOT
    return pl.pallas_call(
        _mlaprep_prompt_kernel,
        grid=(bsz, t // tile),
        in_specs=[row(MLA_LOW_COLS)] + [_const_spec(a) for a in params],
        out_specs=[row(slots), row(slots), pl.BlockSpec((None, MLA_HEADS * V_HEAD, tile), lambda b, i: (b, 0, i)),
                   row(KV_LORA), row(QK_ROPE)],
        out_shape=[jax.ShapeDtypeStruct((bsz, t, slots), BF16), jax.ShapeDtypeStruct((bsz, t, slots), BF16),
                   jax.ShapeDtypeStruct((bsz, MLA_HEADS * V_HEAD, t), BF16),
                   jax.ShapeDtypeStruct((bsz, t, KV_LORA), F32), jax.ShapeDtypeStruct((bsz, t, QK_ROPE), F32)],
        compiler_params=_cparams("parallel", "arbitrary"),
        name="mlaprep_prompt",
    )(low, *params)


def _mlaprep_sample_kernel(low_ref, gq_ref, gkv_ref, wqa_ref, wqb_ref, wukt_ref, invf_ref, pos_ref,
                           qlat_ref, qr_ref, ckv_ref, kr_ref):
    q, ckv, k_rope = _mla_common(low_ref[...], pos_ref[...], gq_ref[...], gkv_ref[...], wqa_ref[...], wqb_ref[...],
                                 invf_ref[...], ATTN_SCALE)
    ckv_ref[...] = ckv
    kr_ref[...] = k_rope[:, :QK_ROPE]
    rope_lanes = _iota((1, HEAD_SLOT), 1) < QK_ROPE
    for h in range(MLA_HEADS):
        slot = q[:, HEAD_SLOT * h:HEAD_SLOT * (h + 1)]
        qlat_ref[:, KV_LORA * h:KV_LORA * (h + 1)] = _mm(slot, wukt_ref[h])
        qr_ref[:, HEAD_SLOT * h:HEAD_SLOT * (h + 1)] = jnp.where(rope_lanes, pltpu.roll(slot, LANES - QK_NOPE, 1), 0.0)


def _mlaprep_sample(low, pos, wts):
    n = low.shape[0]
    params = [wts["g_qnorm"], wts["g_kvnorm"], wts["w_qa"], wts["w_qb"], wts["w_ukt_pad"], wts["invf"], pos]
    full = lambda shape: pl.BlockSpec(shape, lambda i: (0,) * len(shape))
    return pl.pallas_call(
        _mlaprep_sample_kernel,
        grid=(1,),
        in_specs=[_const_spec(low)] + [_const_spec(a) for a in params],
        out_specs=[full((n, MLA_HEADS * KV_LORA)), full((n, MLA_HEADS * HEAD_SLOT)), full((n, KV_LORA)),
                   full((n, QK_ROPE))],
        out_shape=[jax.ShapeDtypeStruct((n, MLA_HEADS * KV_LORA), F32),
                   jax.ShapeDtypeStruct((n, MLA_HEADS * HEAD_SLOT), F32),
                   jax.ShapeDtypeStruct((n, KV_LORA), F32), jax.ShapeDtypeStruct((n, QK_ROPE), F32)],
        compiler_params=_cparams("arbitrary"),
        name="mlaprep_sample",
    )(low, *params)


def _attn_kernel(qi_ref, ki_ref, q_ref, k_ref, vt_ref, o_ref, m_scr, l_scr, acc_scr):
    step = pl.program_id(2)
    qi = qi_ref[step]
    ki = ki_ref[step]
    bq = q_ref.shape[0]
    bk = k_ref.shape[0]
    on_diagonal = ki == (qi * bq) // bk

    @pl.when(ki == 0)
    def _():
        m_scr[...] = jnp.full(m_scr.shape, -jnp.inf, F32)
        l_scr[...] = jnp.zeros_like(l_scr)
        acc_scr[...] = jnp.zeros_like(acc_scr)

    n_heads = m_scr.shape[0]

    def accumulate(masked):
        scores = [_mm_t(k_ref[:, HEAD_SLOT * h:HEAD_SLOT * (h + 1)], q_ref[:, HEAD_SLOT * h:HEAD_SLOT * (h + 1)])
                  for h in range(n_heads)]
        if masked:
            visible = ki * bk + _iota((bk, bq), 0) <= qi * bq + _iota((bk, bq), 1)
            scores = [jnp.where(visible, s, -jnp.inf) for s in scores]
        for h, s in enumerate(scores):
            m_prev = m_scr[h]
            m_new = jnp.maximum(m_prev, jnp.max(s, axis=0, keepdims=True))
            p = jnp.exp2(s - m_new)
            alpha = jnp.exp2(m_prev - m_new)
            l_scr[h] = alpha * l_scr[h] + jnp.sum(p, axis=0, keepdims=True)
            acc_scr[h] = alpha * acc_scr[h] + _mm(vt_ref[V_HEAD * h:V_HEAD * (h + 1), :], p)
            m_scr[h] = m_new

    @pl.when(jnp.logical_not(on_diagonal))
    def _():
        accumulate(False)

    @pl.when(on_diagonal)
    def _():
        accumulate(True)
        out_t = jnp.concatenate([acc_scr[h] / l_scr[h] for h in range(n_heads)], axis=0)
        o_ref[...] = out_t.T


def _attention_prompt(q, k, v_t, bq, bk):
    bsz, t, _ = q.shape
    assert bk % bq == 0
    hps = ATTN_HEADS_PER_STEP
    pairs = [(qi, ki) for qi in range(t // bq) for ki in range((qi * bq) // bk + 1)]
    qi_of = jnp.asarray([p[0] for p in pairs], jnp.int32)
    ki_of = jnp.asarray([p[1] for p in pairs], jnp.int32)
    grid_spec = pltpu.PrefetchScalarGridSpec(
        num_scalar_prefetch=2,
        grid=(bsz, MLA_HEADS // hps, len(pairs)),
        in_specs=[pl.BlockSpec((None, bq, hps * HEAD_SLOT), lambda b, g, s, qi, ki: (b, qi[s], g)),
                  pl.BlockSpec((None, bk, hps * HEAD_SLOT), lambda b, g, s, qi, ki: (b, ki[s], g)),
                  pl.BlockSpec((None, hps * V_HEAD, bk), lambda b, g, s, qi, ki: (b, g, ki[s]))],
        out_specs=pl.BlockSpec((None, bq, hps * V_HEAD), lambda b, g, s, qi, ki: (b, qi[s], g)),
        scratch_shapes=[pltpu.VMEM((hps, 1, bq), F32), pltpu.VMEM((hps, 1, bq), F32),
                        pltpu.VMEM((hps, V_HEAD, bq), F32)],
    )
    return pl.pallas_call(
        _attn_kernel,
        grid_spec=grid_spec,
        out_shape=jax.ShapeDtypeStruct((bsz, t, MLA_HEADS * V_HEAD), F32),
        compiler_params=_cparams("parallel", "parallel", "arbitrary"),
        name="attn_prompt",
    )(qi_of, ki_of, q, k, v_t)


def _sample_attn_kernel(pt_ref, qlat_ref, qr_ref, cn_ref, kn_ref, ckv_hbm, kr_hbm, o_ref, kbuf, rbuf, s_scr, sem):
    b = pl.program_id(0)
    n_seq = pl.num_programs(0)
    n_pages = kbuf.shape[1]
    slot = lax.rem(b, 2)

    def page_copies(seq, slt):
        out = []
        for pg in range(n_pages):
            page = pt_ref[seq, pg]
            out.append(pltpu.make_async_copy(ckv_hbm.at[page], kbuf.at[slt, pg], sem.at[slt, 0]))
            out.append(pltpu.make_async_copy(kr_hbm.at[page], rbuf.at[slt, pg], sem.at[slt, 1]))
        return out

    @pl.when(b == 0)
    def _():
        for cp in page_copies(0, 0):
            cp.start()

    @pl.when(b + 1 < n_seq)
    def _():
        for cp in page_copies(b + 1, 1 - slot):
            cp.start()

    for cp in page_copies(b, slot):
        cp.wait()

    ql = qlat_ref[...]
    qr = qr_ref[:, :QK_ROPE]
    n_chunks = n_pages // 2
    rows = 2 * PAGE_SIZE

    def score(i, carry):
        kc = kbuf[slot, pl.ds(2 * i, 2)].reshape(rows, KV_LORA)
        rc = jnp.concatenate([rbuf[slot, 2 * i], rbuf[slot, 2 * i + 1]], axis=1)
        s_scr[i] = _mm_t(ql, kc) + jnp.dot(qr, rc, preferred_element_type=F32)
        return carry

    lax.fori_loop(0, n_chunks, score, 0, unroll=SAMPLE_ATTN_UNROLL)
    cn = cn_ref[...]
    kn = kn_ref[...]
    s_new = jnp.sum(ql * cn, axis=-1, keepdims=True) + jnp.sum(qr * kn, axis=-1, keepdims=True)
    s_all = s_scr[...]
    m = jnp.maximum(jnp.max(jnp.max(s_all, axis=0), axis=-1, keepdims=True), s_new)
    p_all = jnp.exp(s_all - m)
    p_new = jnp.exp(s_new - m)
    denom = jnp.sum(jnp.sum(p_all, axis=0), axis=-1, keepdims=True) + p_new
    s_scr[...] = p_all

    def accumulate(i, acc):
        kc = kbuf[slot, pl.ds(2 * i, 2)].reshape(rows, KV_LORA)
        return acc + _mm(s_scr[i], kc)

    acc = lax.fori_loop(0, n_chunks, accumulate, jnp.zeros((ql.shape[0], KV_LORA), F32), unroll=SAMPLE_ATTN_UNROLL)
    o_ref[...] = (acc + p_new * cn) / denom


def _attention_sample(page_table, qlat, qr, ckv_new, kr_new, cache_ckv, cache_kr):
    n, n_pages = page_table.shape
    grid_spec = pltpu.PrefetchScalarGridSpec(
        num_scalar_prefetch=1,
        grid=(n,),
        in_specs=[pl.BlockSpec((None, MLA_HEADS, KV_LORA), lambda b, pt: (b, 0, 0)),
                  pl.BlockSpec((None, MLA_HEADS, HEAD_SLOT), lambda b, pt: (b, 0, 0)),
                  pl.BlockSpec((None, 1, KV_LORA), lambda b, pt: (b, 0, 0)),
                  pl.BlockSpec((None, 1, QK_ROPE), lambda b, pt: (b, 0, 0)),
                  pl.BlockSpec(memory_space=pl.ANY),
                  pl.BlockSpec(memory_space=pl.ANY)],
        out_specs=pl.BlockSpec((None, MLA_HEADS, KV_LORA), lambda b, pt: (b, 0, 0)),
        scratch_shapes=[pltpu.VMEM((2, n_pages, PAGE_SIZE, KV_LORA), F32),
                        pltpu.VMEM((2, n_pages, QK_ROPE, PAGE_SIZE), F32),
                        pltpu.VMEM((n_pages // 2, MLA_HEADS, 2 * PAGE_SIZE), F32),
                        pltpu.SemaphoreType.DMA((2, 2))],
    )
    return pl.pallas_call(
        _sample_attn_kernel,
        grid_spec=grid_spec,
        out_shape=jax.ShapeDtypeStruct((n, MLA_HEADS, KV_LORA), F32),
        compiler_params=_cparams("arbitrary"),
        name="attn_sample",
    )(page_table, qlat.reshape(n, MLA_HEADS, KV_LORA), qr.reshape(n, MLA_HEADS, HEAD_SLOT),
      ckv_new.reshape(n, 1, KV_LORA), kr_new.reshape(n, 1, QK_ROPE), cache_ckv, cache_kr)


def _uv_kernel(ol_ref, wuv_ref, o_ref):
    outs = [_mm(ol_ref[:, KV_LORA * h:KV_LORA * (h + 1)], wuv_ref[:, V_HEAD * h:V_HEAD * (h + 1)])
            for h in range(MLA_HEADS)]
    o_ref[...] = jnp.concatenate(outs, axis=1)


def _latent_to_heads(o_lat, w_uv):
    n = o_lat.shape[0]
    o_lat = o_lat.reshape(n, MLA_HEADS * KV_LORA)
    return pl.pallas_call(
        _uv_kernel,
        grid=(1,),
        in_specs=[_const_spec(o_lat), _const_spec(w_uv)],
        out_specs=pl.BlockSpec((n, MLA_HEADS * V_HEAD), lambda i: (0, 0)),
        out_shape=jax.ShapeDtypeStruct((n, MLA_HEADS * V_HEAD), F32),
        compiler_params=_cparams("arbitrary"),
        name="latent_to_heads",
    )(o_lat, w_uv)


def _layernorm(z, g, b):
    mu = jnp.mean(z, axis=-1, keepdims=True)
    d = z - mu
    var = jnp.mean(d * d, axis=-1, keepdims=True)
    return d * lax.rsqrt(var + LN_EPS) * g + b


def _first_max(x, idx, sentinel):
    mx = jnp.max(x, axis=0, keepdims=True)
    return jnp.min(jnp.where(x == mx, idx, sentinel), axis=0, keepdims=True)


def _route(scores, bias):
    n_tok = scores.shape[1]
    per_group = N_EXPERTS // N_GROUPS
    sb = scores + bias
    sb3 = sb.reshape(N_GROUPS, per_group, n_tok)
    member = _iota(sb3.shape, 1)
    m1 = jnp.max(sb3, axis=1, keepdims=True)
    first = jnp.min(jnp.where(sb3 == m1, member, per_group), axis=1, keepdims=True)
    m2 = jnp.max(jnp.where(member == first, -jnp.inf, sb3), axis=1, keepdims=True)
    g_score = (m1 + m2).reshape(N_GROUPS, n_tok)
    g_idx = _iota(g_score.shape, 0)
    g_sel = jnp.zeros(g_score.shape, F32)
    for _ in range(TOPK_GROUPS):
        hit = g_idx == _first_max(g_score, g_idx, N_GROUPS)
        g_sel = jnp.where(hit, 1.0, g_sel)
        g_score = jnp.where(hit, -jnp.inf, g_score)
    e_mask = jnp.broadcast_to(g_sel.reshape(N_GROUPS, 1, n_tok), sb3.shape).reshape(N_EXPERTS, n_tok) > 0.5
    cur = jnp.where(e_mask, sb, -jnp.inf)
    e_idx = _iota(cur.shape, 0)
    sel = jnp.zeros(cur.shape, F32)
    for _ in range(TOP_K):
        hit = e_idx == _first_max(cur, e_idx, N_EXPERTS)
        sel = jnp.where(hit, 1.0, sel)
        cur = jnp.where(hit, -jnp.inf, cur)
    picked = jnp.where(sel > 0.5, scores, 0.0)
    return picked / jnp.sum(picked, axis=0, keepdims=True) * ROUTED_SCALE


def _merge_kernel(x_ref, oa_ref, ob_ref, gs_ref, gt1_ref, sh2_ref, sc2_ref, wba_ref, wbb_ref, wout_ref, g1_ref, b1_ref,
                  wrt_ref, br_ref, x1_ref, h2_ref, cw_ref):
    d = D_MODEL
    ya = _mm(oa_ref[...], wba_ref[...])
    yb = _mm(ob_ref[...], wbb_ref[...])
    merged = gs_ref[:, :d] * ya + gs_ref[:, d:] * yb
    z = DN_ALPHA * x_ref[...] + gt1_ref[...] * _mm(merged, wout_ref[...])
    x1 = _layernorm(z, g1_ref[...], b1_ref[...])
    x1_ref[...] = x1
    h2 = x1 * (1 + sc2_ref[...]) + sh2_ref[...]
    h2_ref[...] = h2.astype(BF16)
    scores = jax.nn.sigmoid(_mm3_t(wrt_ref[...], h2))
    cw_ref[...] = _route(scores, br_ref[...]).T


def _merge(x3, oa, ob, gs, mod, wts, tile):
    bsz, t, d = x3.shape
    params = [wts["w_branch_a"], wts["w_branch_b"], wts["w_out"], wts["ln1_g"], wts["ln1_b"], wts["w_router_t"],
              wts["b_router"]]
    row = lambda w: pl.BlockSpec((None, tile, w), lambda b, i: (b, i, 0))
    return pl.pallas_call(
        _merge_kernel,
        grid=(bsz, t // tile),
        in_specs=[row(d), row(RW_WIDTH), row(MLA_HEADS * V_HEAD), row(2 * d), mod.spec(2), mod.spec(3), mod.spec(4)]
        + [_const_spec(a) for a in params],
        out_specs=[row(d), row(d), row(N_EXPERTS)],
        out_shape=[jax.ShapeDtypeStruct((bsz, t, d), F32), jax.ShapeDtypeStruct((bsz, t, d), BF16),
                   jax.ShapeDtypeStruct((bsz, t, N_EXPERTS), F32)],
        compiler_params=_cparams("parallel", "arbitrary"),
        name="merge",
    )(x3, oa, ob, gs, mod.arr, mod.arr, mod.arr, *params)


def _experts_kernel(h_ref, cw_ref, wg_ref, wu_ref, wd_ref, o_ref):
    step = pl.program_id(1)
    n_e = wg_ref.shape[0]

    @pl.when(step == 0)
    def _():
        o_ref[...] = jnp.zeros_like(o_ref)

    x = h_ref[...]
    cw = cw_ref[...]
    lane = _iota((1, N_EXPERTS), 1)
    acts = []
    for e in range(n_e):
        act = _silu(_mm(x, wg_ref[e])) * _mm(x, wu_ref[e])
        w_col = jnp.sum(jnp.where(lane == step * n_e + e, cw, 0.0), axis=1, keepdims=True)
        acts.append((act * w_col).astype(BF16))
    o_ref[...] += _mm(jnp.concatenate(acts, axis=1), wd_ref[...].reshape(n_e * EXPERT_FF, -1))


def _experts(h2, cw, wts, tile):
    n, d = h2.shape
    n_e = EXPERTS_PER_STEP
    return pl.pallas_call(
        _experts_kernel,
        grid=(n // tile, N_EXPERTS // n_e),
        in_specs=[pl.BlockSpec((tile, d), lambda i, e: (i, 0)),
                  pl.BlockSpec((tile, N_EXPERTS), lambda i, e: (i, 0)),
                  pl.BlockSpec((n_e, d, EXPERT_FF), lambda i, e: (e, 0, 0)),
                  pl.BlockSpec((n_e, d, EXPERT_FF), lambda i, e: (e, 0, 0)),
                  pl.BlockSpec((n_e, EXPERT_FF, d), lambda i, e: (e, 0, 0))],
        out_specs=pl.BlockSpec((tile, d), lambda i, e: (i, 0)),
        out_shape=jax.ShapeDtypeStruct((n, d), F32),
        compiler_params=_cparams("parallel", "arbitrary"),
        name="experts",
    )(h2, cw, wts["w_exp_gate"], wts["w_exp_up"], wts["w_exp_down"])


def _final_kernel(x1_ref, h2_ref, routed_ref, gt2_ref, wsg_ref, wsu_ref, wsd_ref, g2_ref, b2_ref, y_ref):
    h2 = h2_ref[...]
    act = _silu(jnp.dot(h2, wsg_ref[...], preferred_element_type=F32)) * jnp.dot(h2, wsu_ref[...],
                                                                               preferred_element_type=F32)
    ffn = routed_ref[...] + _mm(act, wsd_ref[...])
    y_ref[...] = _layernorm(DN_ALPHA * x1_ref[...] + gt2_ref[...] * ffn, g2_ref[...], b2_ref[...])


def _final(x1, h2, routed, mod, wts, tile):
    bsz, t, d = x1.shape
    params = [wts["w_sh_gate"], wts["w_sh_up"], wts["w_sh_down"], wts["ln2_g"], wts["ln2_b"]]
    row = lambda w: pl.BlockSpec((None, tile, w), lambda b, i: (b, i, 0))
    return pl.pallas_call(
        _final_kernel,
        grid=(bsz, t // tile),
        in_specs=[row(d), row(d), row(d), mod.spec(5)] + [_const_spec(a) for a in params],
        out_specs=row(d),
        out_shape=jax.ShapeDtypeStruct((bsz, t, d), F32),
        compiler_params=_cparams("parallel", "arbitrary"),
        name="final",
    )(x1, h2, routed, mod.arr, *params)


def _rope_rotation_columns(w):
    half = QK_ROPE // 2
    return jnp.concatenate([-w[..., half:], w[..., :half]], axis=-1)


def _prepare_weights(P):
    d = D_MODEL
    w = {}
    w_in = P["w_in"]
    o1 = RW_COLS
    o2 = o1 + Q_LORA
    o3 = o2 + KV_LORA
    o4 = o3 + QK_ROPE
    w_kr = w_in[:, o3:o4]
    pad = jnp.zeros((d, LANES - QK_ROPE), F32)
    w["w_rw"] = w_in[:, :o1].astype(BF16)
    w["w_mla"] = jnp.concatenate([w_in[:, o1:o3], w_kr, pad, _rope_rotation_columns(w_kr), pad], axis=1).astype(BF16)
    w["w_gate"] = w_in[:, o4:].astype(BF16)

    row = lambda v: v.reshape(1, -1)
    w["mu_shift"] = row(P["mu_shift"])
    zl = jnp.zeros((DECAY_LORA, RW_WIDTH), F32)
    w["w_lora"] = jnp.concatenate([jnp.concatenate([P["w_decay_up"], zl], axis=1),
                                   jnp.concatenate([zl, P["w_iclr_up"]], axis=1)], axis=0).astype(BF16)
    w["w_gate_up"] = P["w_gate_up"].astype(BF16)
    for nm in ("w_decay0", "a0", "k_k", "k_a", "r_k", "lnx_g", "lnx_b", "g_qnorm", "g_kvnorm", "ln1_g", "ln1_b",
               "ln2_g", "ln2_b"):
        w[nm] = row(P[nm])
    head_of = np.arange(RW_WIDTH) // RW_HEAD_DIM
    w["ones_blk"] = jnp.asarray(head_of[:, None] == head_of[None, :], BF16)

    wq = P["w_uq"].reshape(Q_LORA, MLA_HEADS, QK_NOPE + QK_ROPE)
    zq = jnp.zeros((Q_LORA, MLA_HEADS, HEAD_SLOT - QK_NOPE - QK_ROPE), F32)
    w["w_qa"] = jnp.concatenate([wq, zq], axis=-1).reshape(Q_LORA, MLA_HEADS * HEAD_SLOT).astype(BF16)
    w["w_qb"] = jnp.concatenate([jnp.zeros((Q_LORA, MLA_HEADS, QK_NOPE), F32), _rope_rotation_columns(wq[..., QK_NOPE:]), zq],
                                axis=-1).reshape(Q_LORA, MLA_HEADS * HEAD_SLOT).astype(BF16)
    wuk = P["w_uk"].reshape(KV_LORA, MLA_HEADS, QK_NOPE)
    zk = jnp.zeros((KV_LORA, MLA_HEADS, HEAD_SLOT - QK_NOPE), F32)
    w["w_uk_pad"] = jnp.concatenate([wuk, zk], axis=-1).reshape(KV_LORA, MLA_HEADS * HEAD_SLOT).astype(BF16)
    w["w_ukt_pad"] = jnp.transpose(jnp.concatenate([wuk, zk], axis=-1), (1, 2, 0)).astype(BF16)
    w["w_uv"] = P["w_uv"].astype(BF16)
    w["w_uv_t"] = P["w_uv"].T.astype(BF16)
    half = QK_ROPE // 2
    inv = ROPE_THETA ** (-jnp.arange(half, dtype=F32) / half)
    w["invf"] = jnp.concatenate([jnp.zeros((QK_NOPE,), F32), inv, inv,
                                 jnp.zeros((HEAD_SLOT - QK_NOPE - QK_ROPE,), F32)]).reshape(1, HEAD_SLOT)

    for nm in ("w_branch_a", "w_branch_b", "w_out", "w_sh_gate", "w_sh_up", "w_sh_down"):
        w[nm] = P[nm].astype(BF16)
    for nm in ("w_exp_gate", "w_exp_up", "w_exp_down"):
        w[nm] = P[nm]
    w["w_router_t"] = P["w_router"].T
    w["b_router"] = P["b_router"].reshape(N_EXPERTS, 1)
    return w


def _row_tile(t, cap):
    tile = min(t, cap)
    assert t % tile == 0 and tile % 16 == 0, (t, tile)
    return tile


def _layer_tail(x3, oa, ob, gs, mod, wts, tile, moe_tile):
    bsz, t, d = x3.shape
    x1, h2, cw = _merge(x3, oa, ob, gs, mod, wts, tile)
    routed = _experts(h2.reshape(bsz * t, d), cw.reshape(bsz * t, N_EXPERTS), wts, moe_tile).reshape(bsz, t, d)
    return _final(x1, h2, routed, mod, wts, tile)


def _layer_prompt(x, mod_rows, wts):
    bsz, t, d = x.shape
    tile = _row_tile(t, 256)
    mod = _Mod(mod_rows.reshape(bsz * 6, 1, d), per_token=False, tile=tile)
    p_rw, low, gs = _inproj(x, mod, wts, tile)
    oa, state = _rw_scan(p_rw, jnp.zeros((bsz, 1, RW_COLS), F32), wts)
    q, k, v_t, ckv, k_rope = _mlaprep_prompt(low, wts, tile)
    ob = _attention_prompt(q, k, v_t, _row_tile(t, 512), _row_tile(t, 1024))
    y = _layer_tail(x, oa, ob, gs, mod, wts, _row_tile(t, 512), _row_tile(bsz * t, 1024))
    return y, ckv, k_rope, _unpack_state(state), p_rw[:, -1]


def _layer_sample(x, mod_rows, state, shift, page_table, cache_ckv, cache_kr, wts):
    n, s_new, d = x.shape
    assert s_new == 1
    past = page_table.shape[1] * PAGE_SIZE
    x3 = x.reshape(1, n, d)
    mod = _Mod(mod_rows, per_token=True, tile=n)
    p_rw, low, gs = _inproj(x3, mod, wts, n)
    p_rw2 = p_rw.reshape(n, RW_COLS)
    rw = dict(zip(RW_OUT_NAMES, _rwprep_sample(p_rw2, shift, wts)))
    new_state_t, oa = _rw_step(jnp.transpose(state, (1, 2, 3, 0)), rw, wts)
    new_state = jnp.transpose(new_state_t, (3, 0, 1, 2))
    pos = jnp.full((n, 1), past, F32)
    qlat, qr, ckv, k_rope = _mlaprep_sample(low.reshape(n, MLA_LOW_COLS), pos, wts)
    o_lat = _attention_sample(page_table, qlat, qr, ckv, k_rope, cache_ckv, jnp.swapaxes(cache_kr, 1, 2))
    ob = _latent_to_heads(o_lat, wts["w_uv"])
    y = _layer_tail(x3, oa.reshape(1, n, RW_WIDTH), ob.reshape(1, n, MLA_HEADS * V_HEAD), gs, mod, wts, n, n)
    return (y.reshape(n, 1, d), ckv.reshape(n, 1, KV_LORA), k_rope.reshape(n, 1, QK_ROPE), new_state, p_rw2)


def kernel(x_prompt, x_sample, c_prompt, c_sample, cache_ckv, cache_krope, state_wkv, state_shift, page_table, w_ada, b_ada, w_in, mu_shift, w_decay0, w_decay_up, a0, w_iclr_up, w_gate_up, k_k, k_a, r_k, lnx_g, lnx_b, w_branch_a, g_qnorm, w_uq, g_kvnorm, w_uk, w_uv, w_branch_b, w_out, ln1_g, ln1_b, w_router, b_router, w_exp_gate, w_exp_up, w_exp_down, w_sh_gate, w_sh_up, w_sh_down, ln2_g, ln2_b):
    params = dict(w_ada=w_ada, b_ada=b_ada, w_in=w_in, mu_shift=mu_shift, w_decay0=w_decay0, w_decay_up=w_decay_up,
                  a0=a0, w_iclr_up=w_iclr_up, w_gate_up=w_gate_up, k_k=k_k, k_a=k_a, r_k=r_k, lnx_g=lnx_g,
                  lnx_b=lnx_b, w_branch_a=w_branch_a, g_qnorm=g_qnorm, w_uq=w_uq, g_kvnorm=g_kvnorm, w_uk=w_uk,
                  w_uv=w_uv, w_branch_b=w_branch_b, w_out=w_out, ln1_g=ln1_g, ln1_b=ln1_b, w_router=w_router,
                  b_router=b_router, w_exp_gate=w_exp_gate, w_exp_up=w_exp_up, w_exp_down=w_exp_down,
                  w_sh_gate=w_sh_gate, w_sh_up=w_sh_up, w_sh_down=w_sh_down, ln2_g=ln2_g, ln2_b=ln2_b)
    depth = w_in.shape[0]
    bp = x_prompt.shape[0]
    bd = x_sample.shape[0]
    n_c = bp + bd
    c_all = jnp.concatenate([c_prompt, c_sample, jnp.zeros((-n_c % 8, D_MODEL), F32)], axis=0)
    yp, ys = x_prompt, x_sample
    outs = [[] for _ in range(8)]
    for l in range(depth):
        wts = _prepare_weights({name: arr[l] for name, arr in params.items()})
        mod = _adaln_mod(c_all, params["w_ada"][l], params["b_ada"][l].reshape(1, -1))
        yp, *rest_p = _layer_prompt(yp, mod[:bp], wts)
        ys, *rest_s = _layer_sample(ys, mod[bp:n_c], state_wkv[l], state_shift[l], page_table, cache_ckv[l],
                                    cache_krope[l], wts)
        for acc, val in zip(outs, rest_p + rest_s):
            acc.append(val)
    return (yp, ys) + tuple(jnp.stack(o) for o in outs)
```

```python
import math

import numpy as np
import jax
import jax.numpy as jnp
from jax import lax
from jax.experimental import pallas as pl
from jax.experimental.pallas import tpu as pltpu

F32 = jnp.float32
BF16 = jnp.bfloat16

D_MODEL = 1024
PAGE_SIZE = 128
RW_HEADS = 8
RW_HEAD_DIM = 64
RW_WIDTH = RW_HEADS * RW_HEAD_DIM
DECAY_LORA = 64
ICLR_LORA = 64
GATE_LORA = 128
RW_COLS = 3 * RW_WIDTH + DECAY_LORA + ICLR_LORA + GATE_LORA
GN_EPS = 64e-5
MLA_HEADS = 8
QK_NOPE = 64
QK_ROPE = 32
V_HEAD = 64
Q_LORA = 384
KV_LORA = 256
ROPE_THETA = 10000.0
ATTN_SCALE = (QK_NOPE + QK_ROPE) ** -0.5
N_EXPERTS = 64
TOP_K = 8
N_GROUPS = 8
TOPK_GROUPS = 4
EXPERT_FF = 256
SHARED_FF = 256
ROUTED_SCALE = 2.5
DEPTH = 1
DN_ALPHA = (2 * DEPTH) ** 0.25
LN_EPS = 1e-5
RMS_EPS = 1e-6

LANES = 128
HEAD_SLOT = LANES
MLA_LOW_COLS = Q_LORA + KV_LORA + 2 * LANES
SCAN_CHUNK = 64
SCAN_SUBCHUNKS = 8
SAMPLE_ATTN_UNROLL = 8
ATTN_HEADS_PER_STEP = 4
EXPERTS_PER_STEP = 4
VMEM_LIMIT = 56 * 1024 * 1024


def _cparams(*sem):
    return pltpu.CompilerParams(dimension_semantics=sem, vmem_limit_bytes=VMEM_LIMIT)


def _mm(a, b):
    return jnp.dot(a.astype(BF16), b.astype(BF16), preferred_element_type=F32)


def _mm_t(a, b):
    return lax.dot_general(a.astype(BF16), b.astype(BF16), (((1,), (1,)), ((), ())), preferred_element_type=F32)


def _mm_tl(a, b):
    return lax.dot_general(a.astype(BF16), b.astype(BF16), (((0,), (0,)), ((), ())), preferred_element_type=F32)


def _split(x):
    hi = x.astype(BF16)
    lo = (x - hi.astype(F32)).astype(BF16)
    return hi, lo


def _three_pass(f, a, b):
    ah, al = _split(a)
    bh, bl = _split(b)
    return f(ah, bh) + (f(ah, bl) + f(al, bh))


def _mm3_t(a, b):
    return _three_pass(_mm_t, a, b)


def _mm_exact_rhs(a, b_exact):
    ah, al = _split(a)
    return _mm(ah, b_exact) + _mm(al, b_exact)


def _silu(x):
    return x * jax.nn.sigmoid(x)


def _iota(shape, dim):
    return lax.broadcasted_iota(jnp.int32, shape, dim)


def _mod_kernel(c_ref, w_ref, b_ref, o_ref):
    o_ref[...] = _mm(_silu(c_ref[...]), w_ref[...]) + b_ref[...]


def _adaln_mod(c_all, w_ada, b_ada):
    n = c_all.shape[0]
    d = D_MODEL
    return pl.pallas_call(
        _mod_kernel,
        grid=(6,),
        in_specs=[pl.BlockSpec((n, d), lambda j: (0, 0)),
                  pl.BlockSpec((d, d), lambda j: (0, j)),
                  pl.BlockSpec((1, d), lambda j: (0, j))],
        out_specs=pl.BlockSpec((n, d), lambda j: (0, j)),
        out_shape=jax.ShapeDtypeStruct((n, 6 * d), F32),
        compiler_params=_cparams("arbitrary"),
        name="adaln_mod",
    )(c_all, w_ada, b_ada)


class _Mod:
    def __init__(self, arr, per_token, tile):
        self.arr = arr
        self.per_token = per_token
        self.tile = tile

    def spec(self, j):
        if self.per_token:
            return pl.BlockSpec((self.tile, D_MODEL), lambda b, i, *_: (i, j))
        return pl.BlockSpec((None, 1, D_MODEL), lambda b, i, *_: (b * 6 + j, 0, 0))


def _const_spec(arr):
    nd = arr.ndim
    return pl.BlockSpec(arr.shape, lambda *_: (0,) * nd)


def _inproj_kernel(x_ref, sh_ref, sc_ref, wrw_ref, wmla_ref, wg_ref, prw_ref, mla_ref, gs_ref):
    h = (x_ref[...] * (1 + sc_ref[...]) + sh_ref[...]).astype(BF16)
    prw_ref[...] = jnp.dot(h, wrw_ref[...], preferred_element_type=F32)
    mla_ref[...] = jnp.dot(h, wmla_ref[...], preferred_element_type=F32)
    gs_ref[...] = jax.nn.sigmoid(jnp.dot(h, wg_ref[...], preferred_element_type=F32))


def _inproj(x3, mod, wts, tile):
    bsz, t, d = x3.shape
    row = lambda w: pl.BlockSpec((None, tile, w), lambda b, i: (b, i, 0))
    return pl.pallas_call(
        _inproj_kernel,
        grid=(bsz, t // tile),
        in_specs=[row(d), mod.spec(0), mod.spec(1),
                  _const_spec(wts["w_rw"]), _const_spec(wts["w_mla"]), _const_spec(wts["w_gate"])],
        out_specs=[row(RW_COLS), row(MLA_LOW_COLS), row(2 * d)],
        out_shape=[jax.ShapeDtypeStruct((bsz, t, RW_COLS), F32),
                   jax.ShapeDtypeStruct((bsz, t, MLA_LOW_COLS), F32),
                   jax.ShapeDtypeStruct((bsz, t, 2 * d), F32)],
        compiler_params=_cparams("parallel", "arbitrary"),
        name="inproj",
    )(x3, mod.arr, mod.arr, wts["w_rw"], wts["w_mla"], wts["w_gate"])


RW_OUT_NAMES = ("r", "kmod", "v", "kkn", "bvec", "logw", "bonus", "g")


def _rw_elementwise(p, pp, mu, w_lora, w_gate_up, w_decay0, a0, k_k, k_a, r_k, ones_blk):
    w = RW_WIDTH
    pm = p + (pp - p) * mu
    r, k, v = pm[:, 0:w], pm[:, w:2 * w], pm[:, 2 * w:3 * w]
    xwa = pm[:, 3 * w:3 * w + DECAY_LORA + ICLR_LORA]
    xg = pm[:, 3 * w + DECAY_LORA + ICLR_LORA:]
    lane = _iota((1, DECAY_LORA + ICLR_LORA), 1)
    z = jnp.where(lane < DECAY_LORA, jnp.tanh(xwa), xwa)
    lora = _mm(z, w_lora)
    y = -(w_decay0 + lora[:, :w])
    softplus = jnp.maximum(y, 0.0) + jnp.log1p(jnp.exp(-jnp.abs(y)))
    w_log = -softplus - 0.5
    logw = -jnp.exp(w_log)
    a = jax.nn.sigmoid(a0 + lora[:, w:])
    g = _mm(jax.nn.sigmoid(xg), w_gate_up)
    kk = k * k_k
    kk = kk * lax.rsqrt(_mm_exact_rhs(kk * kk, ones_blk) + 1e-12)
    kmod = k * (1.0 + (a - 1.0) * k_a)
    bonus = _mm_exact_rhs(r * kmod * r_k, ones_blk) * v
    return r, kmod, v, -kk, kk * a, logw, bonus, g


def _rwprep_sample_kernel(p_ref, pp_ref, mu_ref, wl_ref, wgu_ref, wd0_ref, a0_ref, kk_ref, ka_ref, rk_ref,
                          ones_ref, *out_refs):
    outs = _rw_elementwise(p_ref[...], pp_ref[...], mu_ref[...], wl_ref[...], wgu_ref[...], wd0_ref[...], a0_ref[...],
                           kk_ref[...], ka_ref[...], rk_ref[...], ones_ref[...])
    for o_ref, val in zip(out_refs, outs):
        o_ref[...] = val


def _rw_param_list(wts):
    return [wts["mu_shift"], wts["w_lora"], wts["w_gate_up"], wts["w_decay0"], wts["a0"], wts["k_k"], wts["k_a"],
            wts["r_k"], wts["ones_blk"]]


def _rwprep_sample(p_rw, p_prev, wts):
    n = p_rw.shape[0]
    params = _rw_param_list(wts)
    return pl.pallas_call(
        _rwprep_sample_kernel,
        grid=(1,),
        in_specs=[_const_spec(p_rw), _const_spec(p_prev)] + [_const_spec(a) for a in params],
        out_specs=[pl.BlockSpec((n, RW_WIDTH), lambda i: (0, 0))] * 8,
        out_shape=[jax.ShapeDtypeStruct((n, RW_WIDTH), F32)] * 8,
        compiler_params=_cparams("arbitrary"),
        name="rwprep_sample",
    )(p_rw, p_prev, *params)


def _group_norm_pair(o, ones_pair):
    mu = _mm_exact_rhs(o, ones_pair) * (1.0 / RW_HEAD_DIM)
    d = o - mu
    var = _mm_exact_rhs(d * d, ones_pair) * (1.0 / RW_HEAD_DIM)
    return d * lax.rsqrt(var + GN_EPS)


def _scan_kernel(p_ref, s0_ref, mu_ref, wl_ref, wgu_ref, wd0_ref, a0_ref, kk_ref, ka_ref, rk_ref, ones_ref,
                 lng_ref, lnb_ref, o_ref, st_ref, h_scr, prev_scr):
    c = pl.program_id(1)
    n_steps = pl.num_programs(1)
    n_pairs = h_scr.shape[0]
    cl = SCAN_CHUNK
    n_rows = p_ref.shape[0]
    n_sub = n_rows // cl

    @pl.when(c == 0)
    def _():
        h_scr[...] = jnp.zeros_like(h_scr)
        prev_scr[...] = s0_ref[...]

    p = p_ref[...]
    pp = jnp.where(_iota(p.shape, 0) == 0, prev_scr[...], pltpu.roll(p, 1, 0))
    prev_scr[...] = p[n_rows - 1:n_rows, :]
    r_all, k_all, v_all, a_all, b_all, lw_all, bon_all, g_all = _rw_elementwise(
        p, pp, mu_ref[...], wl_ref[...], wgu_ref[...], wd0_ref[...], a0_ref[...], kk_ref[...], ka_ref[...],
        rk_ref[...], ones_ref[...])

    row = _iota((cl, cl), 0)
    col = _iota((cl, cl), 1)
    tri_incl = row >= col
    tri_strict = row > col
    tri_ones = tri_incl.astype(BF16)
    eye = (row == col).astype(F32)
    head0 = _iota((1, LANES), 1) < RW_HEAD_DIM
    head_mask = (head0, jnp.logical_not(head0))
    head0_2 = (_iota((1, 2 * LANES), 1) & RW_HEAD_DIM) == 0
    r128 = _iota((LANES, LANES), 0)
    c128 = _iota((LANES, LANES), 1)
    same_head = (r128 < RW_HEAD_DIM) == (c128 < RW_HEAD_DIM)
    diag = r128 == c128
    ones_pair = same_head.astype(BF16)
    units = [(s, p) for s in range(n_sub) for p in range(n_pairs)]
    heads = [(s, p, h) for (s, p) in units for h in range(2)]
    rows = {s: slice(cl * s, cl * (s + 1)) for s in range(n_sub)}
    lanes = {p: slice(LANES * p, LANES * (p + 1)) for p in range(n_pairs)}
    at = lambda ref, u: ref[rows[u[0]], lanes[u[1]]]

    lw = {u: at(lw_all, u) for u in units}
    lg = {}
    for u in units:
        l1 = lw[u].astype(BF16)
        rem = lw[u] - l1.astype(F32)
        l2 = rem.astype(BF16)
        l3 = (rem - l2.astype(F32)).astype(BF16)
        lg[u] = _mm(tri_ones, l1) + (_mm(tri_ones, l2) + _mm(tri_ones, l3))
    vv = {u: at(v_all, u) for u in units}
    rh, ah, bh, kh, bt, kt, g_last = {}, {}, {}, {}, {}, {}, {}
    for u in units:
        bb = at(b_all, u)
        kk = at(k_all, u)
        lg_last = lg[u][cl - 1:cl, :]
        rh[u] = at(r_all, u) * jnp.exp(lg[u])
        ah[u] = at(a_all, u) * jnp.exp(lg[u] - lw[u])
        inv = jnp.exp(-lg[u])
        bh[u] = bb * inv
        kh[u] = kk * inv
        tail = jnp.exp(lg_last - lg[u])
        bt[u] = bb * tail
        kt[u] = kk * tail
        g_last[u] = jnp.exp(lg_last)

    xb, xk = {}, {}
    for (s, p, h) in heads:
        ar_h = jnp.where(head_mask[h], jnp.concatenate([ah[s, p], rh[s, p]], axis=0), 0.0)
        xb[s, p, h] = _mm_t(ar_h, bh[s, p])
        xk[s, p, h] = _mm_t(ar_h, kh[s, p])
    l_ab = {k: jnp.where(tri_strict, x[:cl], 0.0) for k, x in xb.items()}
    m_rb = {k: jnp.where(tri_incl, x[cl:], 0.0) for k, x in xb.items()}
    l_ak = {k: jnp.where(tri_strict, x[:cl], 0.0) for k, x in xk.items()}
    m_rk = {k: jnp.where(tri_incl, x[cl:], 0.0) for k, x in xk.items()}
    lv_h = {k: _mm(l_ak[k], vv[k[:2]]) for k in heads}
    mrkv_h = {k: _mm(m_rk[k], vv[k[:2]]) for k in heads}
    ktv = {u: _mm_tl(kt[u], vv[u]) for u in units}

    base_bits = 3
    blk = lambda bits: (row >> bits) == (col >> bits)
    sum_half = _iota((1, 2 * cl), 1) >= cl
    x = {k: jnp.concatenate([jnp.where(blk(base_bits), l_ab[k], 0.0), eye], axis=1) for k in heads}
    for _ in range(base_bits):
        x = {k: _mm(x[k][:, :cl], x[k]) + jnp.where(sum_half, x[k], 0.0) for k in heads}
    t_inv = {k: x[k][:, cl:] for k in heads}
    for bits in range(base_bits, int(math.log2(cl))):
        lower_left = jnp.logical_and(blk(bits + 1), jnp.logical_not(blk(bits)))
        y = {k: _mm(jnp.where(lower_left, l_ab[k], 0.0), t_inv[k]) for k in heads}
        t_inv = {k: t_inv[k] + _mm(t_inv[k], y[k]) for k in heads}

    pick = lambda d, u: jnp.where(head0, d[u + (0,)], d[u + (1,)])
    pick2 = lambda d, u: jnp.where(head0_2, d[u + (0,)], d[u + (1,)])
    z = {u: jnp.concatenate([ah[u], pick(lv_h, u)], axis=1) for u in units}
    tz = {k: _mm(t_inv[k], z[k[:2]]) for k in heads}
    w12 = {u: pick2(tz, u) for u in units}
    q12_h = {k: _mm(m_rb[k], w12[k[:2]]) for k in heads}
    g12 = {u: _mm_tl(bt[u], w12[u]) for u in units}
    q1, q2, g1, g2 = {}, {}, {}, {}
    for u in units:
        q12 = pick2(q12_h, u)
        q1[u] = rh[u] + q12[:, :LANES]
        q2[u] = q12[:, LANES:] + pick(mrkv_h, u)
        g1[u] = jnp.where(same_head, g12[u][:, :LANES], 0.0) + jnp.where(diag, g_last[u], 0.0)
        g2[u] = jnp.where(same_head, g12[u][:, LANES:] + ktv[u], 0.0)

    state = {p: h_scr[p] for p in range(n_pairs)}
    outs = {}
    for s in range(n_sub):
        for p in range(n_pairs):
            outs[s, p] = _mm(q1[s, p], state[p]) + q2[s, p]
        state = {p: _mm(g1[s, p], state[p]) + g2[s, p] for p in range(n_pairs)}
    for p in range(n_pairs):
        h_scr[p] = state[p]

    for u in units:
        y = _group_norm_pair(outs[u], ones_pair) * lng_ref[:, lanes[u[1]]] + lnb_ref[:, lanes[u[1]]]
        o_ref[rows[u[0]], lanes[u[1]]] = (y + at(bon_all, u)) * at(g_all, u)

    @pl.when(c == n_steps - 1)
    def _():
        st_ref[...] = h_scr[...]


def _rw_scan(p_rw, shift0, wts):
    bsz, t, _ = p_rw.shape
    w = RW_WIDTH
    n_pairs = w // LANES
    step_rows = SCAN_CHUNK * SCAN_SUBCHUNKS
    assert t % step_rows == 0, (t, step_rows)
    blk = lambda width: pl.BlockSpec((None, step_rows, width), lambda b, c: (b, c, 0))
    params = _rw_param_list(wts) + [wts["lnx_g"], wts["lnx_b"]]
    return pl.pallas_call(
        _scan_kernel,
        grid=(bsz, t // step_rows),
        in_specs=[blk(RW_COLS), pl.BlockSpec((None, 1, RW_COLS), lambda b, c: (b, 0, 0))]
        + [_const_spec(a) for a in params],
        out_specs=[blk(w), pl.BlockSpec((None, n_pairs, LANES, LANES), lambda b, c: (b, 0, 0, 0))],
        out_shape=[jax.ShapeDtypeStruct((bsz, t, w), F32),
                   jax.ShapeDtypeStruct((bsz, n_pairs, LANES, LANES), F32)],
        scratch_shapes=[pltpu.VMEM((n_pairs, LANES, LANES), F32), pltpu.VMEM((1, RW_COLS), F32)],
        compiler_params=_cparams("parallel", "arbitrary"),
        name="rw_scan",
    )(p_rw, shift0, *params)


def _unpack_state(st):
    hd = RW_HEAD_DIM
    blocks = [st[:, p, j * hd:(j + 1) * hd, j * hd:(j + 1) * hd] for p in range(st.shape[1]) for j in range(2)]
    return jnp.swapaxes(jnp.stack(blocks, axis=1), -1, -2)


def _rwstep_kernel(s_ref, r_ref, k_ref, v_ref, a_ref, b_ref, lw_ref, bon_ref, g_ref, lng_ref, lnb_ref, so_ref, o_ref,
                   acc_scr, vt_scr):
    hd = RW_HEAD_DIM
    vt_scr[...] = v_ref[...].T
    decay2 = jnp.exp(lw_ref[...]).T
    kkn2 = a_ref[...].T
    kka2 = b_ref[...].T
    kmod2 = k_ref[...].T
    r2 = r_ref[...].T
    normed = []
    for h in range(2):
        ch = slice(hd * h, hd * (h + 1))
        decay, kkn, kka, kmod, r = decay2[ch], kkn2[ch], kka2[ch], kmod2[ch], r2[ch]

        def body(i, carry):
            s = s_ref[h, i]
            sa = jnp.sum(s * kkn, axis=0, keepdims=True)
            s_new = s * decay + sa * kka + vt_scr[pl.ds(hd * h + i, 1), :] * kmod
            so_ref[h, i] = s_new
            acc_scr[pl.ds(hd * h + i, 1), :] = jnp.sum(s_new * r, axis=0, keepdims=True)
            return carry

        lax.fori_loop(0, hd, body, 0, unroll=8)
        o = acc_scr[ch, :]
        mu = jnp.mean(o, axis=0, keepdims=True)
        d = o - mu
        var = jnp.mean(d * d, axis=0, keepdims=True)
        normed.append(d * lax.rsqrt(var + GN_EPS))
    y = jnp.concatenate(normed, axis=0).T * lng_ref[...] + lnb_ref[...]
    o_ref[...] = (y + bon_ref[...]) * g_ref[...]


def _rw_step(state_t, rw, wts):
    n = state_t.shape[-1]
    hd = RW_HEAD_DIM
    vec = pl.BlockSpec((n, 2 * hd), lambda p: (0, p))
    par = pl.BlockSpec((1, 2 * hd), lambda p: (0, p))
    st = pl.BlockSpec((2, hd, hd, n), lambda p: (p, 0, 0, 0))
    ins = [rw[nm] for nm in ("r", "kmod", "v", "kkn", "bvec", "logw", "bonus", "g")]
    return pl.pallas_call(
        _rwstep_kernel,
        grid=(RW_HEADS // 2,),
        in_specs=[st] + [vec] * 8 + [par, par],
        out_specs=[st, vec],
        out_shape=[jax.ShapeDtypeStruct(state_t.shape, F32), jax.ShapeDtypeStruct((n, RW_WIDTH), F32)],
        scratch_shapes=[pltpu.VMEM((2 * hd, n), F32), pltpu.VMEM((2 * hd, n), F32)],
        compiler_params=_cparams("parallel"),
        name="rw_step",
    )(state_t, *ins, wts["lnx_g"], wts["lnx_b"])


def _rms(x, g):
    return x * lax.rsqrt(jnp.mean(x * x, axis=-1, keepdims=True) + RMS_EPS) * g


def _mla_common(low, pos, gq, gkv, wqa, wqb, invf, q_scale):
    cq = _rms(low[:, :Q_LORA], gq)
    ckv = _rms(low[:, Q_LORA:Q_LORA + KV_LORA], gkv)
    kr = low[:, Q_LORA + KV_LORA:Q_LORA + KV_LORA + LANES]
    kr_rot = low[:, Q_LORA + KV_LORA + LANES:]
    ang = pos * invf
    cos = jnp.cos(ang)
    sin = jnp.sin(ang)
    cos8 = jnp.concatenate([cos] * MLA_HEADS, axis=1)
    sin8 = jnp.concatenate([sin] * MLA_HEADS, axis=1)
    q = (_mm(cq, wqa) * cos8 + _mm(cq, wqb) * sin8) * q_scale
    k_rope = kr * pltpu.roll(cos, LANES - QK_NOPE, 1) + kr_rot * pltpu.roll(sin, LANES - QK_NOPE, 1)
    return q, ckv, k_rope


def _mlaprep_prompt_kernel(low_ref, gq_ref, gkv_ref, wqa_ref, wqb_ref, wuk_ref, wuv_ref, invf_ref,
                           q_ref, k_ref, v_ref, ckv_ref, kr_ref):
    i = pl.program_id(1)
    rows = low_ref.shape[0]
    pos = (i * rows + _iota((rows, 1), 0)).astype(F32)
    q, ckv, k_rope = _mla_common(low_ref[...], pos, gq_ref[...], gkv_ref[...], wqa_ref[...], wqb_ref[...],
                                 invf_ref[...], ATTN_SCALE * math.log2(math.e))
    q_ref[...] = q.astype(BF16)
    ckv_ref[...] = ckv
    kr_ref[...] = k_rope[:, :QK_ROPE]
    k_slot = pltpu.roll(k_rope, QK_NOPE, 1)
    k_ref[...] = (_mm(ckv, wuk_ref[...]) + jnp.concatenate([k_slot] * MLA_HEADS, axis=1)).astype(BF16)
    v_ref[...] = _mm_t(wuv_ref[...], ckv).astype(BF16)


def _mlaprep_prompt(low, wts, tile):
    bsz, t, _ = low.shape
    params = [wts["g_qnorm"], wts["g_kvnorm"], wts["w_qa"], wts["w_qb"], wts["w_uk_pad"], wts["w_uv_t"], wts["invf"]]
    row = lambda w: pl.BlockSpec((None, tile, w), lambda b, i: (b, i, 0))
    slots = MLA_HEADS * HEAD_SLOT
    return pl.pallas_call(
        _mlaprep_prompt_kernel,
        grid=(bsz, t // tile),
        in_specs=[row(MLA_LOW_COLS)] + [_const_spec(a) for a in params],
        out_specs=[row(slots), row(slots), pl.BlockSpec((None, MLA_HEADS * V_HEAD, tile), lambda b, i: (b, 0, i)),
                   row(KV_LORA), row(QK_ROPE)],
        out_shape=[jax.ShapeDtypeStruct((bsz, t, slots), BF16), jax.ShapeDtypeStruct((bsz, t, slots), BF16),
                   jax.ShapeDtypeStruct((bsz, MLA_HEADS * V_HEAD, t), BF16),
                   jax.ShapeDtypeStruct((bsz, t, KV_LORA), F32), jax.ShapeDtypeStruct((bsz, t, QK_ROPE), F32)],
        compiler_params=_cparams("parallel", "arbitrary"),
        name="mlaprep_prompt",
    )(low, *params)


def _mlaprep_sample_kernel(low_ref, gq_ref, gkv_ref, wqa_ref, wqb_ref, wukt_ref, invf_ref, pos_ref,
                           qlat_ref, qr_ref, ckv_ref, kr_ref):
    q, ckv, k_rope = _mla_common(low_ref[...], pos_ref[...], gq_ref[...], gkv_ref[...], wqa_ref[...], wqb_ref[...],
                                 invf_ref[...], ATTN_SCALE)
    ckv_ref[...] = ckv
    kr_ref[...] = k_rope[:, :QK_ROPE]
    rope_lanes = _iota((1, HEAD_SLOT), 1) < QK_ROPE
    for h in range(MLA_HEADS):
        slot = q[:, HEAD_SLOT * h:HEAD_SLOT * (h + 1)]
        qlat_ref[:, KV_LORA * h:KV_LORA * (h + 1)] = _mm(slot, wukt_ref[h])
        qr_ref[:, HEAD_SLOT * h:HEAD_SLOT * (h + 1)] = jnp.where(rope_lanes, pltpu.roll(slot, LANES - QK_NOPE, 1), 0.0)


def _mlaprep_sample(low, pos, wts):
    n = low.shape[0]
    params = [wts["g_qnorm"], wts["g_kvnorm"], wts["w_qa"], wts["w_qb"], wts["w_ukt_pad"], wts["invf"], pos]
    full = lambda shape: pl.BlockSpec(shape, lambda i: (0,) * len(shape))
    return pl.pallas_call(
        _mlaprep_sample_kernel,
        grid=(1,),
        in_specs=[_const_spec(low)] + [_const_spec(a) for a in params],
        out_specs=[full((n, MLA_HEADS * KV_LORA)), full((n, MLA_HEADS * HEAD_SLOT)), full((n, KV_LORA)),
                   full((n, QK_ROPE))],
        out_shape=[jax.ShapeDtypeStruct((n, MLA_HEADS * KV_LORA), F32),
                   jax.ShapeDtypeStruct((n, MLA_HEADS * HEAD_SLOT), F32),
                   jax.ShapeDtypeStruct((n, KV_LORA), F32), jax.ShapeDtypeStruct((n, QK_ROPE), F32)],
        compiler_params=_cparams("arbitrary"),
        name="mlaprep_sample",
    )(low, *params)


def _attn_kernel(qi_ref, ki_ref, q_ref, k_ref, vt_ref, o_ref, m_scr, l_scr, acc_scr):
    step = pl.program_id(2)
    qi = qi_ref[step]
    ki = ki_ref[step]
    bq = q_ref.shape[0]
    bk = k_ref.shape[0]
    on_diagonal = ki == (qi * bq) // bk

    @pl.when(ki == 0)
    def _():
        m_scr[...] = jnp.full(m_scr.shape, -jnp.inf, F32)
        l_scr[...] = jnp.zeros_like(l_scr)
        acc_scr[...] = jnp.zeros_like(acc_scr)

    n_heads = m_scr.shape[0]

    def accumulate(masked):
        scores = [_mm_t(k_ref[:, HEAD_SLOT * h:HEAD_SLOT * (h + 1)], q_ref[:, HEAD_SLOT * h:HEAD_SLOT * (h + 1)])
                  for h in range(n_heads)]
        if masked:
            visible = ki * bk + _iota((bk, bq), 0) <= qi * bq + _iota((bk, bq), 1)
            scores = [jnp.where(visible, s, -jnp.inf) for s in scores]
        for h, s in enumerate(scores):
            m_prev = m_scr[h]
            m_new = jnp.maximum(m_prev, jnp.max(s, axis=0, keepdims=True))
            p = jnp.exp2(s - m_new)
            alpha = jnp.exp2(m_prev - m_new)
            l_scr[h] = alpha * l_scr[h] + jnp.sum(p, axis=0, keepdims=True)
            acc_scr[h] = alpha * acc_scr[h] + _mm(vt_ref[V_HEAD * h:V_HEAD * (h + 1), :], p)
            m_scr[h] = m_new

    @pl.when(jnp.logical_not(on_diagonal))
    def _():
        accumulate(False)

    @pl.when(on_diagonal)
    def _():
        accumulate(True)
        out_t = jnp.concatenate([acc_scr[h] / l_scr[h] for h in range(n_heads)], axis=0)
        o_ref[...] = out_t.T


def _attention_prompt(q, k, v_t, bq, bk):
    bsz, t, _ = q.shape
    assert bk % bq == 0
    hps = ATTN_HEADS_PER_STEP
    pairs = [(qi, ki) for qi in range(t // bq) for ki in range((qi * bq) // bk + 1)]
    qi_of = jnp.asarray([p[0] for p in pairs], jnp.int32)
    ki_of = jnp.asarray([p[1] for p in pairs], jnp.int32)
    grid_spec = pltpu.PrefetchScalarGridSpec(
        num_scalar_prefetch=2,
        grid=(bsz, MLA_HEADS // hps, len(pairs)),
        in_specs=[pl.BlockSpec((None, bq, hps * HEAD_SLOT), lambda b, g, s, qi, ki: (b, qi[s], g)),
                  pl.BlockSpec((None, bk, hps * HEAD_SLOT), lambda b, g, s, qi, ki: (b, ki[s], g)),
                  pl.BlockSpec((None, hps * V_HEAD, bk), lambda b, g, s, qi, ki: (b, g, ki[s]))],
        out_specs=pl.BlockSpec((None, bq, hps * V_HEAD), lambda b, g, s, qi, ki: (b, qi[s], g)),
        scratch_shapes=[pltpu.VMEM((hps, 1, bq), F32), pltpu.VMEM((hps, 1, bq), F32),
                        pltpu.VMEM((hps, V_HEAD, bq), F32)],
    )
    return pl.pallas_call(
        _attn_kernel,
        grid_spec=grid_spec,
        out_shape=jax.ShapeDtypeStruct((bsz, t, MLA_HEADS * V_HEAD), F32),
        compiler_params=_cparams("parallel", "parallel", "arbitrary"),
        name="attn_prompt",
    )(qi_of, ki_of, q, k, v_t)


def _sample_attn_kernel(pt_ref, qlat_ref, qr_ref, cn_ref, kn_ref, ckv_hbm, kr_hbm, o_ref, kbuf, rbuf, s_scr, sem):
    b = pl.program_id(0)
    n_seq = pl.num_programs(0)
    n_pages = kbuf.shape[1]
    slot = lax.rem(b, 2)

    def page_copies(seq, slt):
        out = []
        for pg in range(n_pages):
            page = pt_ref[seq, pg]
            out.append(pltpu.make_async_copy(ckv_hbm.at[page], kbuf.at[slt, pg], sem.at[slt, 0]))
            out.append(pltpu.make_async_copy(kr_hbm.at[page], rbuf.at[slt, pg], sem.at[slt, 1]))
        return out

    @pl.when(b == 0)
    def _():
        for cp in page_copies(0, 0):
            cp.start()

    @pl.when(b + 1 < n_seq)
    def _():
        for cp in page_copies(b + 1, 1 - slot):
            cp.start()

    for cp in page_copies(b, slot):
        cp.wait()

    ql = qlat_ref[...]
    qr = qr_ref[:, :QK_ROPE]
    n_chunks = n_pages // 2
    rows = 2 * PAGE_SIZE

    def score(i, carry):
        kc = kbuf[slot, pl.ds(2 * i, 2)].reshape(rows, KV_LORA)
        rc = jnp.concatenate([rbuf[slot, 2 * i], rbuf[slot, 2 * i + 1]], axis=1)
        s_scr[i] = _mm_t(ql, kc) + jnp.dot(qr, rc, preferred_element_type=F32)
        return carry

    lax.fori_loop(0, n_chunks, score, 0, unroll=SAMPLE_ATTN_UNROLL)
    cn = cn_ref[...]
    kn = kn_ref[...]
    s_new = jnp.sum(ql * cn, axis=-1, keepdims=True) + jnp.sum(qr * kn, axis=-1, keepdims=True)
    s_all = s_scr[...]
    m = jnp.maximum(jnp.max(jnp.max(s_all, axis=0), axis=-1, keepdims=True), s_new)
    p_all = jnp.exp(s_all - m)
    p_new = jnp.exp(s_new - m)
    denom = jnp.sum(jnp.sum(p_all, axis=0), axis=-1, keepdims=True) + p_new
    s_scr[...] = p_all

    def accumulate(i, acc):
        kc = kbuf[slot, pl.ds(2 * i, 2)].reshape(rows, KV_LORA)
        return acc + _mm(s_scr[i], kc)

    acc = lax.fori_loop(0, n_chunks, accumulate, jnp.zeros((ql.shape[0], KV_LORA), F32), unroll=SAMPLE_ATTN_UNROLL)
    o_ref[...] = (acc + p_new * cn) / denom


def _attention_sample(page_table, qlat, qr, ckv_new, kr_new, cache_ckv, cache_kr):
    n, n_pages = page_table.shape
    grid_spec = pltpu.PrefetchScalarGridSpec(
        num_scalar_prefetch=1,
        grid=(n,),
        in_specs=[pl.BlockSpec((None, MLA_HEADS, KV_LORA), lambda b, pt: (b, 0, 0)),
                  pl.BlockSpec((None, MLA_HEADS, HEAD_SLOT), lambda b, pt: (b, 0, 0)),
                  pl.BlockSpec((None, 1, KV_LORA), lambda b, pt: (b, 0, 0)),
                  pl.BlockSpec((None, 1, QK_ROPE), lambda b, pt: (b, 0, 0)),
                  pl.BlockSpec(memory_space=pl.ANY),
                  pl.BlockSpec(memory_space=pl.ANY)],
        out_specs=pl.BlockSpec((None, MLA_HEADS, KV_LORA), lambda b, pt: (b, 0, 0)),
        scratch_shapes=[pltpu.VMEM((2, n_pages, PAGE_SIZE, KV_LORA), F32),
                        pltpu.VMEM((2, n_pages, QK_ROPE, PAGE_SIZE), F32),
                        pltpu.VMEM((n_pages // 2, MLA_HEADS, 2 * PAGE_SIZE), F32),
                        pltpu.SemaphoreType.DMA((2, 2))],
    )
    return pl.pallas_call(
        _sample_attn_kernel,
        grid_spec=grid_spec,
        out_shape=jax.ShapeDtypeStruct((n, MLA_HEADS, KV_LORA), F32),
        compiler_params=_cparams("arbitrary"),
        name="attn_sample",
    )(page_table, qlat.reshape(n, MLA_HEADS, KV_LORA), qr.reshape(n, MLA_HEADS, HEAD_SLOT),
      ckv_new.reshape(n, 1, KV_LORA), kr_new.reshape(n, 1, QK_ROPE), cache_ckv, cache_kr)


def _uv_kernel(ol_ref, wuv_ref, o_ref):
    outs = [_mm(ol_ref[:, KV_LORA * h:KV_LORA * (h + 1)], wuv_ref[:, V_HEAD * h:V_HEAD * (h + 1)])
            for h in range(MLA_HEADS)]
    o_ref[...] = jnp.concatenate(outs, axis=1)


def _latent_to_heads(o_lat, w_uv):
    n = o_lat.shape[0]
    o_lat = o_lat.reshape(n, MLA_HEADS * KV_LORA)
    return pl.pallas_call(
        _uv_kernel,
        grid=(1,),
        in_specs=[_const_spec(o_lat), _const_spec(w_uv)],
        out_specs=pl.BlockSpec((n, MLA_HEADS * V_HEAD), lambda i: (0, 0)),
        out_shape=jax.ShapeDtypeStruct((n, MLA_HEADS * V_HEAD), F32),
        compiler_params=_cparams("arbitrary"),
        name="latent_to_heads",
    )(o_lat, w_uv)


def _layernorm(z, g, b):
    mu = jnp.mean(z, axis=-1, keepdims=True)
    d = z - mu
    var = jnp.mean(d * d, axis=-1, keepdims=True)
    return d * lax.rsqrt(var + LN_EPS) * g + b


def _first_max(x, idx, sentinel):
    mx = jnp.max(x, axis=0, keepdims=True)
    return jnp.min(jnp.where(x == mx, idx, sentinel), axis=0, keepdims=True)


def _route(scores, bias):
    n_tok = scores.shape[1]
    per_group = N_EXPERTS // N_GROUPS
    sb = scores + bias
    sb3 = sb.reshape(N_GROUPS, per_group, n_tok)
    member = _iota(sb3.shape, 1)
    m1 = jnp.max(sb3, axis=1, keepdims=True)
    first = jnp.min(jnp.where(sb3 == m1, member, per_group), axis=1, keepdims=True)
    m2 = jnp.max(jnp.where(member == first, -jnp.inf, sb3), axis=1, keepdims=True)
    g_score = (m1 + m2).reshape(N_GROUPS, n_tok)
    g_idx = _iota(g_score.shape, 0)
    g_sel = jnp.zeros(g_score.shape, F32)
    for _ in range(TOPK_GROUPS):
        hit = g_idx == _first_max(g_score, g_idx, N_GROUPS)
        g_sel = jnp.where(hit, 1.0, g_sel)
        g_score = jnp.where(hit, -jnp.inf, g_score)
    e_mask = jnp.broadcast_to(g_sel.reshape(N_GROUPS, 1, n_tok), sb3.shape).reshape(N_EXPERTS, n_tok) > 0.5
    cur = jnp.where(e_mask, sb, -jnp.inf)
    e_idx = _iota(cur.shape, 0)
    sel = jnp.zeros(cur.shape, F32)
    for _ in range(TOP_K):
        hit = e_idx == _first_max(cur, e_idx, N_EXPERTS)
        sel = jnp.where(hit, 1.0, sel)
        cur = jnp.where(hit, -jnp.inf, cur)
    picked = jnp.where(sel > 0.5, scores, 0.0)
    return picked / jnp.sum(picked, axis=0, keepdims=True) * ROUTED_SCALE


def _merge_kernel(x_ref, oa_ref, ob_ref, gs_ref, gt1_ref, sh2_ref, sc2_ref, wba_ref, wbb_ref, wout_ref, g1_ref, b1_ref,
                  wrt_ref, br_ref, x1_ref, h2_ref, cw_ref):
    d = D_MODEL
    ya = _mm(oa_ref[...], wba_ref[...])
    yb = _mm(ob_ref[...], wbb_ref[...])
    merged = gs_ref[:, :d] * ya + gs_ref[:, d:] * yb
    z = DN_ALPHA * x_ref[...] + gt1_ref[...] * _mm(merged, wout_ref[...])
    x1 = _layernorm(z, g1_ref[...], b1_ref[...])
    x1_ref[...] = x1
    h2 = x1 * (1 + sc2_ref[...]) + sh2_ref[...]
    h2_ref[...] = h2.astype(BF16)
    scores = jax.nn.sigmoid(_mm3_t(wrt_ref[...], h2))
    cw_ref[...] = _route(scores, br_ref[...]).T


def _merge(x3, oa, ob, gs, mod, wts, tile):
    bsz, t, d = x3.shape
    params = [wts["w_branch_a"], wts["w_branch_b"], wts["w_out"], wts["ln1_g"], wts["ln1_b"], wts["w_router_t"],
              wts["b_router"]]
    row = lambda w: pl.BlockSpec((None, tile, w), lambda b, i: (b, i, 0))
    return pl.pallas_call(
        _merge_kernel,
        grid=(bsz, t // tile),
        in_specs=[row(d), row(RW_WIDTH), row(MLA_HEADS * V_HEAD), row(2 * d), mod.spec(2), mod.spec(3), mod.spec(4)]
        + [_const_spec(a) for a in params],
        out_specs=[row(d), row(d), row(N_EXPERTS)],
        out_shape=[jax.ShapeDtypeStruct((bsz, t, d), F32), jax.ShapeDtypeStruct((bsz, t, d), BF16),
                   jax.ShapeDtypeStruct((bsz, t, N_EXPERTS), F32)],
        compiler_params=_cparams("parallel", "arbitrary"),
        name="merge",
    )(x3, oa, ob, gs, mod.arr, mod.arr, mod.arr, *params)


def _experts_kernel(h_ref, cw_ref, wg_ref, wu_ref, wd_ref, o_ref):
    step = pl.program_id(1)
    n_e = wg_ref.shape[0]

    @pl.when(step == 0)
    def _():
        o_ref[...] = jnp.zeros_like(o_ref)

    x = h_ref[...]
    cw = cw_ref[...]
    lane = _iota((1, N_EXPERTS), 1)
    acts = []
    for e in range(n_e):
        act = _silu(_mm(x, wg_ref[e])) * _mm(x, wu_ref[e])
        w_col = jnp.sum(jnp.where(lane == step * n_e + e, cw, 0.0), axis=1, keepdims=True)
        acts.append((act * w_col).astype(BF16))
    o_ref[...] += _mm(jnp.concatenate(acts, axis=1), wd_ref[...].reshape(n_e * EXPERT_FF, -1))


def _experts(h2, cw, wts, tile):
    n, d = h2.shape
    n_e = EXPERTS_PER_STEP
    return pl.pallas_call(
        _experts_kernel,
        grid=(n // tile, N_EXPERTS // n_e),
        in_specs=[pl.BlockSpec((tile, d), lambda i, e: (i, 0)),
                  pl.BlockSpec((tile, N_EXPERTS), lambda i, e: (i, 0)),
                  pl.BlockSpec((n_e, d, EXPERT_FF), lambda i, e: (e, 0, 0)),
                  pl.BlockSpec((n_e, d, EXPERT_FF), lambda i, e: (e, 0, 0)),
                  pl.BlockSpec((n_e, EXPERT_FF, d), lambda i, e: (e, 0, 0))],
        out_specs=pl.BlockSpec((tile, d), lambda i, e: (i, 0)),
        out_shape=jax.ShapeDtypeStruct((n, d), F32),
        compiler_params=_cparams("parallel", "arbitrary"),
        name="experts",
    )(h2, cw, wts["w_exp_gate"], wts["w_exp_up"], wts["w_exp_down"])


def _final_kernel(x1_ref, h2_ref, routed_ref, gt2_ref, wsg_ref, wsu_ref, wsd_ref, g2_ref, b2_ref, y_ref):
    h2 = h2_ref[...]
    act = _silu(jnp.dot(h2, wsg_ref[...], preferred_element_type=F32)) * jnp.dot(h2, wsu_ref[...],
                                                                               preferred_element_type=F32)
    ffn = routed_ref[...] + _mm(act, wsd_ref[...])
    y_ref[...] = _layernorm(DN_ALPHA * x1_ref[...] + gt2_ref[...] * ffn, g2_ref[...], b2_ref[...])


def _final(x1, h2, routed, mod, wts, tile):
    bsz, t, d = x1.shape
    params = [wts["w_sh_gate"], wts["w_sh_up"], wts["w_sh_down"], wts["ln2_g"], wts["ln2_b"]]
    row = lambda w: pl.BlockSpec((None, tile, w), lambda b, i: (b, i, 0))
    return pl.pallas_call(
        _final_kernel,
        grid=(bsz, t // tile),
        in_specs=[row(d), row(d), row(d), mod.spec(5)] + [_const_spec(a) for a in params],
        out_specs=row(d),
        out_shape=jax.ShapeDtypeStruct((bsz, t, d), F32),
        compiler_params=_cparams("parallel", "arbitrary"),
        name="final",
    )(x1, h2, routed, mod.arr, *params)


def _rope_rotation_columns(w):
    half = QK_ROPE // 2
    return jnp.concatenate([-w[..., half:], w[..., :half]], axis=-1)


def _prepare_weights(P):
    d = D_MODEL
    w = {}
    w_in = P["w_in"]
    o1 = RW_COLS
    o2 = o1 + Q_LORA
    o3 = o2 + KV_LORA
    o4 = o3 + QK_ROPE
    w_kr = w_in[:, o3:o4]
    pad = jnp.zeros((d, LANES - QK_ROPE), F32)
    w["w_rw"] = w_in[:, :o1].astype(BF16)
    w["w_mla"] = jnp.concatenate([w_in[:, o1:o3], w_kr, pad, _rope_rotation_columns(w_kr), pad], axis=1).astype(BF16)
    w["w_gate"] = w_in[:, o4:].astype(BF16)

    row = lambda v: v.reshape(1, -1)
    w["mu_shift"] = row(P["mu_shift"])
    zl = jnp.zeros((DECAY_LORA, RW_WIDTH), F32)
    w["w_lora"] = jnp.concatenate([jnp.concatenate([P["w_decay_up"], zl], axis=1),
                                   jnp.concatenate([zl, P["w_iclr_up"]], axis=1)], axis=0).astype(BF16)
    w["w_gate_up"] = P["w_gate_up"].astype(BF16)
    for nm in ("w_decay0", "a0", "k_k", "k_a", "r_k", "lnx_g", "lnx_b", "g_qnorm", "g_kvnorm", "ln1_g", "ln1_b",
               "ln2_g", "ln2_b"):
        w[nm] = row(P[nm])
    head_of = np.arange(RW_WIDTH) // RW_HEAD_DIM
    w["ones_blk"] = jnp.asarray(head_of[:, None] == head_of[None, :], BF16)

    wq = P["w_uq"].reshape(Q_LORA, MLA_HEADS, QK_NOPE + QK_ROPE)
    zq = jnp.zeros((Q_LORA, MLA_HEADS, HEAD_SLOT - QK_NOPE - QK_ROPE), F32)
    w["w_qa"] = jnp.concatenate([wq, zq], axis=-1).reshape(Q_LORA, MLA_HEADS * HEAD_SLOT).astype(BF16)
    w["w_qb"] = jnp.concatenate([jnp.zeros((Q_LORA, MLA_HEADS, QK_NOPE), F32), _rope_rotation_columns(wq[..., QK_NOPE:]), zq],
                                axis=-1).reshape(Q_LORA, MLA_HEADS * HEAD_SLOT).astype(BF16)
    wuk = P["w_uk"].reshape(KV_LORA, MLA_HEADS, QK_NOPE)
    zk = jnp.zeros((KV_LORA, MLA_HEADS, HEAD_SLOT - QK_NOPE), F32)
    w["w_uk_pad"] = jnp.concatenate([wuk, zk], axis=-1).reshape(KV_LORA, MLA_HEADS * HEAD_SLOT).astype(BF16)
    w["w_ukt_pad"] = jnp.transpose(jnp.concatenate([wuk, zk], axis=-1), (1, 2, 0)).astype(BF16)
    w["w_uv"] = P["w_uv"].astype(BF16)
    w["w_uv_t"] = P["w_uv"].T.astype(BF16)
    half = QK_ROPE // 2
    inv = ROPE_THETA ** (-jnp.arange(half, dtype=F32) / half)
    w["invf"] = jnp.concatenate([jnp.zeros((QK_NOPE,), F32), inv, inv,
                                 jnp.zeros((HEAD_SLOT - QK_NOPE - QK_ROPE,), F32)]).reshape(1, HEAD_SLOT)

    for nm in ("w_branch_a", "w_branch_b", "w_out", "w_sh_gate", "w_sh_up", "w_sh_down"):
        w[nm] = P[nm].astype(BF16)
    for nm in ("w_exp_gate", "w_exp_up", "w_exp_down"):
        w[nm] = P[nm]
    w["w_router_t"] = P["w_router"].T
    w["b_router"] = P["b_router"].reshape(N_EXPERTS, 1)
    return w


def _row_tile(t, cap):
    tile = min(t, cap)
    assert t % tile == 0 and tile % 16 == 0, (t, tile)
    return tile


def _layer_tail(x3, oa, ob, gs, mod, wts, tile, moe_tile):
    bsz, t, d = x3.shape
    x1, h2, cw = _merge(x3, oa, ob, gs, mod, wts, tile)
    routed = _experts(h2.reshape(bsz * t, d), cw.reshape(bsz * t, N_EXPERTS), wts, moe_tile).reshape(bsz, t, d)
    return _final(x1, h2, routed, mod, wts, tile)


def _layer_prompt(x, mod_rows, wts):
    bsz, t, d = x.shape
    tile = _row_tile(t, 256)
    mod = _Mod(mod_rows.reshape(bsz * 6, 1, d), per_token=False, tile=tile)
    p_rw, low, gs = _inproj(x, mod, wts, tile)
    oa, state = _rw_scan(p_rw, jnp.zeros((bsz, 1, RW_COLS), F32), wts)
    q, k, v_t, ckv, k_rope = _mlaprep_prompt(low, wts, tile)
    ob = _attention_prompt(q, k, v_t, _row_tile(t, 512), _row_tile(t, 1024))
    y = _layer_tail(x, oa, ob, gs, mod, wts, _row_tile(t, 512), _row_tile(bsz * t, 1024))
    return y, ckv, k_rope, _unpack_state(state), p_rw[:, -1]


def _layer_sample(x, mod_rows, state, shift, page_table, cache_ckv, cache_kr, wts):
    n, s_new, d = x.shape
    assert s_new == 1
    past = page_table.shape[1] * PAGE_SIZE
    x3 = x.reshape(1, n, d)
    mod = _Mod(mod_rows, per_token=True, tile=n)
    p_rw, low, gs = _inproj(x3, mod, wts, n)
    p_rw2 = p_rw.reshape(n, RW_COLS)
    rw = dict(zip(RW_OUT_NAMES, _rwprep_sample(p_rw2, shift, wts)))
    new_state_t, oa = _rw_step(jnp.transpose(state, (1, 2, 3, 0)), rw, wts)
    new_state = jnp.transpose(new_state_t, (3, 0, 1, 2))
    pos = jnp.full((n, 1), past, F32)
    qlat, qr, ckv, k_rope = _mlaprep_sample(low.reshape(n, MLA_LOW_COLS), pos, wts)
    o_lat = _attention_sample(page_table, qlat, qr, ckv, k_rope, cache_ckv, jnp.swapaxes(cache_kr, 1, 2))
    ob = _latent_to_heads(o_lat, wts["w_uv"])
    y = _layer_tail(x3, oa.reshape(1, n, RW_WIDTH), ob.reshape(1, n, MLA_HEADS * V_HEAD), gs, mod, wts, n, n)
    return (y.reshape(n, 1, d), ckv.reshape(n, 1, KV_LORA), k_rope.reshape(n, 1, QK_ROPE), new_state, p_rw2)


def kernel(x_prompt, x_sample, c_prompt, c_sample, cache_ckv, cache_krope, state_wkv, state_shift, page_table, w_ada, b_ada, w_in, mu_shift, w_decay0, w_decay_up, a0, w_iclr_up, w_gate_up, k_k, k_a, r_k, lnx_g, lnx_b, w_branch_a, g_qnorm, w_uq, g_kvnorm, w_uk, w_uv, w_branch_b, w_out, ln1_g, ln1_b, w_router, b_router, w_exp_gate, w_exp_up, w_exp_down, w_sh_gate, w_sh_up, w_sh_down, ln2_g, ln2_b):
    params = dict(w_ada=w_ada, b_ada=b_ada, w_in=w_in, mu_shift=mu_shift, w_decay0=w_decay0, w_decay_up=w_decay_up,
                  a0=a0, w_iclr_up=w_iclr_up, w_gate_up=w_gate_up, k_k=k_k, k_a=k_a, r_k=r_k, lnx_g=lnx_g,
                  lnx_b=lnx_b, w_branch_a=w_branch_a, g_qnorm=g_qnorm, w_uq=w_uq, g_kvnorm=g_kvnorm, w_uk=w_uk,
                  w_uv=w_uv, w_branch_b=w_branch_b, w_out=w_out, ln1_g=ln1_g, ln1_b=ln1_b, w_router=w_router,
                  b_router=b_router, w_exp_gate=w_exp_gate, w_exp_up=w_exp_up, w_exp_down=w_exp_down,
                  w_sh_gate=w_sh_gate, w_sh_up=w_sh_up, w_sh_down=w_sh_down, ln2_g=ln2_g, ln2_b=ln2_b)
    depth = w_in.shape[0]
    bp = x_prompt.shape[0]
    bd = x_sample.shape[0]
    n_c = bp + bd
    c_all = jnp.concatenate([c_prompt, c_sample, jnp.zeros((-n_c % 8, D_MODEL), F32)], axis=0)
    yp, ys = x_prompt, x_sample
    outs = [[] for _ in range(8)]
    for l in range(depth):
        wts = _prepare_weights({name: arr[l] for name, arr in params.items()})
        mod = _adaln_mod(c_all, params["w_ada"][l], params["b_ada"][l].reshape(1, -1))
        yp, *rest_p = _layer_prompt(yp, mod[:bp], wts)
        ys, *rest_s = _layer_sample(ys, mod[bp:n_c], state_wkv[l], state_shift[l], page_table, cache_ckv[l],
                                    cache_krope[l], wts)
        for acc, val in zip(outs, rest_p + rest_s):
            acc.append(val)
    return (yp, ys) + tuple(jnp.stack(o) for o in outs)
```

```python
import math

import numpy as np
import jax
import jax.numpy as jnp
from jax import lax
from jax.experimental import pallas as pl
from jax.experimental.pallas import tpu as pltpu

F32 = jnp.float32
BF16 = jnp.bfloat16

D_MODEL = 1024
PAGE_SIZE = 128
RW_HEADS = 8
RW_HEAD_DIM = 64
RW_WIDTH = RW_HEADS * RW_HEAD_DIM
DECAY_LORA = 64
ICLR_LORA = 64
GATE_LORA = 128
RW_COLS = 3 * RW_WIDTH + DECAY_LORA + ICLR_LORA + GATE_LORA
GN_EPS = 64e-5
MLA_HEADS = 8
QK_NOPE = 64
QK_ROPE = 32
V_HEAD = 64
Q_LORA = 384
KV_LORA = 256
ROPE_THETA = 10000.0
ATTN_SCALE = (QK_NOPE + QK_ROPE) ** -0.5
N_EXPERTS = 64
TOP_K = 8
N_GROUPS = 8
TOPK_GROUPS = 4
EXPERT_FF = 256
SHARED_FF = 256
ROUTED_SCALE = 2.5
DEPTH = 1
DN_ALPHA = (2 * DEPTH) ** 0.25
LN_EPS = 1e-5
RMS_EPS = 1e-6

LANES = 128
HEAD_SLOT = LANES
MLA_LOW_COLS = Q_LORA + KV_LORA + 2 * LANES
SCAN_CHUNK = 64
SCAN_SUBCHUNKS = 8
SAMPLE_ATTN_UNROLL = 8
ATTN_HEADS_PER_STEP = 4
EXPERTS_PER_STEP = 4
VMEM_LIMIT = 56 * 1024 * 1024


def _cparams(*sem):
    return pltpu.CompilerParams(dimension_semantics=sem, vmem_limit_bytes=VMEM_LIMIT)


def _mm(a, b):
    return jnp.dot(a.astype(BF16), b.astype(BF16), preferred_element_type=F32)


def _mm_t(a, b):
    return lax.dot_general(a.astype(BF16), b.astype(BF16), (((1,), (1,)), ((), ())), preferred_element_type=F32)


def _mm_tl(a, b):
    return lax.dot_general(a.astype(BF16), b.astype(BF16), (((0,), (0,)), ((), ())), preferred_element_type=F32)


def _split(x):
    hi = x.astype(BF16)
    lo = (x - hi.astype(F32)).astype(BF16)
    return hi, lo


def _three_pass(f, a, b):
    ah, al = _split(a)
    bh, bl = _split(b)
    return f(ah, bh) + (f(ah, bl) + f(al, bh))


def _mm3_t(a, b):
    return _three_pass(_mm_t, a, b)


def _mm_exact_rhs(a, b_exact):
    ah, al = _split(a)
    return _mm(ah, b_exact) + _mm(al, b_exact)


def _silu(x):
    return x * jax.nn.sigmoid(x)


def _iota(shape, dim):
    return lax.broadcasted_iota(jnp.int32, shape, dim)


def _mod_kernel(c_ref, w_ref, b_ref, o_ref):
    o_ref[...] = _mm(_silu(c_ref[...]), w_ref[...]) + b_ref[...]


def _adaln_mod(c_all, w_ada, b_ada):
    n = c_all.shape[0]
    d = D_MODEL
    return pl.pallas_call(
        _mod_kernel,
        grid=(6,),
        in_specs=[pl.BlockSpec((n, d), lambda j: (0, 0)),
                  pl.BlockSpec((d, d), lambda j: (0, j)),
                  pl.BlockSpec((1, d), lambda j: (0, j))],
        out_specs=pl.BlockSpec((n, d), lambda j: (0, j)),
        out_shape=jax.ShapeDtypeStruct((n, 6 * d), F32),
        compiler_params=_cparams("arbitrary"),
        name="adaln_mod",
    )(c_all, w_ada, b_ada)


class _Mod:
    def __init__(self, arr, per_token, tile):
        self.arr = arr
        self.per_token = per_token
        self.tile = tile

    def spec(self, j):
        if self.per_token:
            return pl.BlockSpec((self.tile, D_MODEL), lambda b, i, *_: (i, j))
        return pl.BlockSpec((None, 1, D_MODEL), lambda b, i, *_: (b * 6 + j, 0, 0))


def _const_spec(arr):
    nd = arr.ndim
    return pl.BlockSpec(arr.shape, lambda *_: (0,) * nd)


def _inproj_kernel(x_ref, sh_ref, sc_ref, wrw_ref, wmla_ref, wg_ref, prw_ref, mla_ref, gs_ref):
    h = (x_ref[...] * (1 + sc_ref[...]) + sh_ref[...]).astype(BF16)
    prw_ref[...] = jnp.dot(h, wrw_ref[...], preferred_element_type=F32)
    mla_ref[...] = jnp.dot(h, wmla_ref[...], preferred_element_type=F32)
    gs_ref[...] = jax.nn.sigmoid(jnp.dot(h, wg_ref[...], preferred_element_type=F32))


def _inproj(x3, mod, wts, tile):
    bsz, t, d = x3.shape
    row = lambda w: pl.BlockSpec((None, tile, w), lambda b, i: (b, i, 0))
    return pl.pallas_call(
        _inproj_kernel,
        grid=(bsz, t // tile),
        in_specs=[row(d), mod.spec(0), mod.spec(1),
                  _const_spec(wts["w_rw"]), _const_spec(wts["w_mla"]), _const_spec(wts["w_gate"])],
        out_specs=[row(RW_COLS), row(MLA_LOW_COLS), row(2 * d)],
        out_shape=[jax.ShapeDtypeStruct((bsz, t, RW_COLS), F32),
                   jax.ShapeDtypeStruct((bsz, t, MLA_LOW_COLS), F32),
                   jax.ShapeDtypeStruct((bsz, t, 2 * d), F32)],
        compiler_params=_cparams("parallel", "arbitrary"),
        name="inproj",
    )(x3, mod.arr, mod.arr, wts["w_rw"], wts["w_mla"], wts["w_gate"])


RW_OUT_NAMES = ("r", "kmod", "v", "kkn", "bvec", "logw", "bonus", "g")


def _rw_elementwise(p, pp, mu, w_lora, w_gate_up, w_decay0, a0, k_k, k_a, r_k, ones_blk):
    w = RW_WIDTH
    pm = p + (pp - p) * mu
    r, k, v = pm[:, 0:w], pm[:, w:2 * w], pm[:, 2 * w:3 * w]
    xwa = pm[:, 3 * w:3 * w + DECAY_LORA + ICLR_LORA]
    xg = pm[:, 3 * w + DECAY_LORA + ICLR_LORA:]
    lane = _iota((1, DECAY_LORA + ICLR_LORA), 1)
    z = jnp.where(lane < DECAY_LORA, jnp.tanh(xwa), xwa)
    lora = _mm(z, w_lora)
    y = -(w_decay0 + lora[:, :w])
    softplus = jnp.maximum(y, 0.0) + jnp.log1p(jnp.exp(-jnp.abs(y)))
    w_log = -softplus - 0.5
    logw = -jnp.exp(w_log)
    a = jax.nn.sigmoid(a0 + lora[:, w:])
    g = _mm(jax.nn.sigmoid(xg), w_gate_up)
    kk = k * k_k
    kk = kk * lax.rsqrt(_mm_exact_rhs(kk * kk, ones_blk) + 1e-12)
    kmod = k * (1.0 + (a - 1.0) * k_a)
    bonus = _mm_exact_rhs(r * kmod * r_k, ones_blk) * v
    return r, kmod, v, -kk, kk * a, logw, bonus, g


def _rwprep_sample_kernel(p_ref, pp_ref, mu_ref, wl_ref, wgu_ref, wd0_ref, a0_ref, kk_ref, ka_ref, rk_ref,
                          ones_ref, *out_refs):
    outs = _rw_elementwise(p_ref[...], pp_ref[...], mu_ref[...], wl_ref[...], wgu_ref[...], wd0_ref[...], a0_ref[...],
                           kk_ref[...], ka_ref[...], rk_ref[...], ones_ref[...])
    for o_ref, val in zip(out_refs, outs):
        o_ref[...] = val


def _rw_param_list(wts):
    return [wts["mu_shift"], wts["w_lora"], wts["w_gate_up"], wts["w_decay0"], wts["a0"], wts["k_k"], wts["k_a"],
            wts["r_k"], wts["ones_blk"]]


def _rwprep_sample(p_rw, p_prev, wts):
    n = p_rw.shape[0]
    params = _rw_param_list(wts)
    return pl.pallas_call(
        _rwprep_sample_kernel,
        grid=(1,),
        in_specs=[_const_spec(p_rw), _const_spec(p_prev)] + [_const_spec(a) for a in params],
        out_specs=[pl.BlockSpec((n, RW_WIDTH), lambda i: (0, 0))] * 8,
        out_shape=[jax.ShapeDtypeStruct((n, RW_WIDTH), F32)] * 8,
        compiler_params=_cparams("arbitrary"),
        name="rwprep_sample",
    )(p_rw, p_prev, *params)


def _group_norm_pair(o, ones_pair):
    mu = _mm_exact_rhs(o, ones_pair) * (1.0 / RW_HEAD_DIM)
    d = o - mu
    var = _mm_exact_rhs(d * d, ones_pair) * (1.0 / RW_HEAD_DIM)
    return d * lax.rsqrt(var + GN_EPS)


def _scan_kernel(p_ref, s0_ref, mu_ref, wl_ref, wgu_ref, wd0_ref, a0_ref, kk_ref, ka_ref, rk_ref, ones_ref,
                 lng_ref, lnb_ref, o_ref, st_ref, h_scr, prev_scr):
    c = pl.program_id(1)
    n_steps = pl.num_programs(1)
    n_pairs = h_scr.shape[0]
    cl = SCAN_CHUNK
    n_rows = p_ref.shape[0]
    n_sub = n_rows // cl

    @pl.when(c == 0)
    def _():
        h_scr[...] = jnp.zeros_like(h_scr)
        prev_scr[...] = s0_ref[...]

    p = p_ref[...]
    pp = jnp.where(_iota(p.shape, 0) == 0, prev_scr[...], pltpu.roll(p, 1, 0))
    prev_scr[...] = p[n_rows - 1:n_rows, :]
    r_all, k_all, v_all, a_all, b_all, lw_all, bon_all, g_all = _rw_elementwise(
        p, pp, mu_ref[...], wl_ref[...], wgu_ref[...], wd0_ref[...], a0_ref[...], kk_ref[...], ka_ref[...],
        rk_ref[...], ones_ref[...])

    row = _iota((cl, cl), 0)
    col = _iota((cl, cl), 1)
    tri_incl = row >= col
    tri_strict = row > col
    tri_ones = tri_incl.astype(BF16)
    eye = (row == col).astype(F32)
    head0 = _iota((1, LANES), 1) < RW_HEAD_DIM
    head_mask = (head0, jnp.logical_not(head0))
    head0_2 = (_iota((1, 2 * LANES), 1) & RW_HEAD_DIM) == 0
    r128 = _iota((LANES, LANES), 0)
    c128 = _iota((LANES, LANES), 1)
    same_head = (r128 < RW_HEAD_DIM) == (c128 < RW_HEAD_DIM)
    diag = r128 == c128
    ones_pair = same_head.astype(BF16)
    units = [(s, p) for s in range(n_sub) for p in range(n_pairs)]
    heads = [(s, p, h) for (s, p) in units for h in range(2)]
    rows = {s: slice(cl * s, cl * (s + 1)) for s in range(n_sub)}
    lanes = {p: slice(LANES * p, LANES * (p + 1)) for p in range(n_pairs)}
    at = lambda ref, u: ref[rows[u[0]], lanes[u[1]]]

    lw = {u: at(lw_all, u) for u in units}
    lg = {}
    for u in units:
        l1 = lw[u].astype(BF16)
        rem = lw[u] - l1.astype(F32)
        l2 = rem.astype(BF16)
        l3 = (rem - l2.astype(F32)).astype(BF16)
        lg[u] = _mm(tri_ones, l1) + (_mm(tri_ones, l2) + _mm(tri_ones, l3))
    vv = {u: at(v_all, u) for u in units}
    rh, ah, bh, kh, bt, kt, g_last = {}, {}, {}, {}, {}, {}, {}
    for u in units:
        bb = at(b_all, u)
        kk = at(k_all, u)
        lg_last = lg[u][cl - 1:cl, :]
        rh[u] = at(r_all, u) * jnp.exp(lg[u])
        ah[u] = at(a_all, u) * jnp.exp(lg[u] - lw[u])
        inv = jnp.exp(-lg[u])
        bh[u] = bb * inv
        kh[u] = kk * inv
        tail = jnp.exp(lg_last - lg[u])
        bt[u] = bb * tail
        kt[u] = kk * tail
        g_last[u] = jnp.exp(lg_last)

    xb, xk = {}, {}
    for (s, p, h) in heads:
        ar_h = jnp.where(head_mask[h], jnp.concatenate([ah[s, p], rh[s, p]], axis=0), 0.0)
        xb[s, p, h] = _mm_t(ar_h, bh[s, p])
        xk[s, p, h] = _mm_t(ar_h, kh[s, p])
    l_ab = {k: jnp.where(tri_strict, x[:cl], 0.0) for k, x in xb.items()}
    m_rb = {k: jnp.where(tri_incl, x[cl:], 0.0) for k, x in xb.items()}
    l_ak = {k: jnp.where(tri_strict, x[:cl], 0.0) for k, x in xk.items()}
    m_rk = {k: jnp.where(tri_incl, x[cl:], 0.0) for k, x in xk.items()}
    lv_h = {k: _mm(l_ak[k], vv[k[:2]]) for k in heads}
    mrkv_h = {k: _mm(m_rk[k], vv[k[:2]]) for k in heads}
    ktv = {u: _mm_tl(kt[u], vv[u]) for u in units}

    base_bits = 3
    blk = lambda bits: (row >> bits) == (col >> bits)
    sum_half = _iota((1, 2 * cl), 1) >= cl
    x = {k: jnp.concatenate([jnp.where(blk(base_bits), l_ab[k], 0.0), eye], axis=1) for k in heads}
    for _ in range(base_bits):
        x = {k: _mm(x[k][:, :cl], x[k]) + jnp.where(sum_half, x[k], 0.0) for k in heads}
    t_inv = {k: x[k][:, cl:] for k in heads}
    for bits in range(base_bits, int(math.log2(cl))):
        lower_left = jnp.logical_and(blk(bits + 1), jnp.logical_not(blk(bits)))
        y = {k: _mm(jnp.where(lower_left, l_ab[k], 0.0), t_inv[k]) for k in heads}
        t_inv = {k: t_inv[k] + _mm(t_inv[k], y[k]) for k in heads}

    pick = lambda d, u: jnp.where(head0, d[u + (0,)], d[u + (1,)])
    pick2 = lambda d, u: jnp.where(head0_2, d[u + (0,)], d[u + (1,)])
    z = {u: jnp.concatenate([ah[u], pick(lv_h, u)], axis=1) for u in units}
    tz = {k: _mm(t_inv[k], z[k[:2]]) for k in heads}
    w12 = {u: pick2(tz, u) for u in units}
    q12_h = {k: _mm(m_rb[k], w12[k[:2]]) for k in heads}
    g12 = {u: _mm_tl(bt[u], w12[u]) for u in units}
    q1, q2, g1, g2 = {}, {}, {}, {}
    for u in units:
        q12 = pick2(q12_h, u)
        q1[u] = rh[u] + q12[:, :LANES]
        q2[u] = q12[:, LANES:] + pick(mrkv_h, u)
        g1[u] = jnp.where(same_head, g12[u][:, :LANES], 0.0) + jnp.where(diag, g_last[u], 0.0)
        g2[u] = jnp.where(same_head, g12[u][:, LANES:] + ktv[u], 0.0)

    state = {p: h_scr[p] for p in range(n_pairs)}
    outs = {}
    for s in range(n_sub):
        for p in range(n_pairs):
            outs[s, p] = _mm(q1[s, p], state[p]) + q2[s, p]
        state = {p: _mm(g1[s, p], state[p]) + g2[s, p] for p in range(n_pairs)}
    for p in range(n_pairs):
        h_scr[p] = state[p]

    for u in units:
        y = _group_norm_pair(outs[u], ones_pair) * lng_ref[:, lanes[u[1]]] + lnb_ref[:, lanes[u[1]]]
        o_ref[rows[u[0]], lanes[u[1]]] = (y + at(bon_all, u)) * at(g_all, u)

    @pl.when(c == n_steps - 1)
    def _():
        st_ref[...] = h_scr[...]


def _rw_scan(p_rw, shift0, wts):
    bsz, t, _ = p_rw.shape
    w = RW_WIDTH
    n_pairs = w // LANES
    step_rows = SCAN_CHUNK * SCAN_SUBCHUNKS
    assert t % step_rows == 0, (t, step_rows)
    blk = lambda width: pl.BlockSpec((None, step_rows, width), lambda b, c: (b, c, 0))
    params = _rw_param_list(wts) + [wts["lnx_g"], wts["lnx_b"]]
    return pl.pallas_call(
        _scan_kernel,
        grid=(bsz, t // step_rows),
        in_specs=[blk(RW_COLS), pl.BlockSpec((None, 1, RW_COLS), lambda b, c: (b, 0, 0))]
        + [_const_spec(a) for a in params],
        out_specs=[blk(w), pl.BlockSpec((None, n_pairs, LANES, LANES), lambda b, c: (b, 0, 0, 0))],
        out_shape=[jax.ShapeDtypeStruct((bsz, t, w), F32),
                   jax.ShapeDtypeStruct((bsz, n_pairs, LANES, LANES), F32)],
        scratch_shapes=[pltpu.VMEM((n_pairs, LANES, LANES), F32), pltpu.VMEM((1, RW_COLS), F32)],
        compiler_params=_cparams("parallel", "arbitrary"),
        name="rw_scan",
    )(p_rw, shift0, *params)


def _unpack_state(st):
    hd = RW_HEAD_DIM
    blocks = [st[:, p, j * hd:(j + 1) * hd, j * hd:(j + 1) * hd] for p in range(st.shape[1]) for j in range(2)]
    return jnp.swapaxes(jnp.stack(blocks, axis=1), -1, -2)


def _rwstep_kernel(s_ref, r_ref, k_ref, v_ref, a_ref, b_ref, lw_ref, bon_ref, g_ref, lng_ref, lnb_ref, so_ref, o_ref,
                   acc_scr, vt_scr):
    hd = RW_HEAD_DIM
    vt_scr[...] = v_ref[...].T
    decay2 = jnp.exp(lw_ref[...]).T
    kkn2 = a_ref[...].T
    kka2 = b_ref[...].T
    kmod2 = k_ref[...].T
    r2 = r_ref[...].T
    normed = []
    for h in range(2):
        ch = slice(hd * h, hd * (h + 1))
        decay, kkn, kka, kmod, r = decay2[ch], kkn2[ch], kka2[ch], kmod2[ch], r2[ch]

        def body(i, carry):
            s = s_ref[h, i]
            sa = jnp.sum(s * kkn, axis=0, keepdims=True)
            s_new = s * decay + sa * kka + vt_scr[pl.ds(hd * h + i, 1), :] * kmod
            so_ref[h, i] = s_new
            acc_scr[pl.ds(hd * h + i, 1), :] = jnp.sum(s_new * r, axis=0, keepdims=True)
            return carry

        lax.fori_loop(0, hd, body, 0, unroll=8)
        o = acc_scr[ch, :]
        mu = jnp.mean(o, axis=0, keepdims=True)
        d = o - mu
        var = jnp.mean(d * d, axis=0, keepdims=True)
        normed.append(d * lax.rsqrt(var + GN_EPS))
    y = jnp.concatenate(normed, axis=0).T * lng_ref[...] + lnb_ref[...]
    o_ref[...] = (y + bon_ref[...]) * g_ref[...]


def _rw_step(state_t, rw, wts):
    n = state_t.shape[-1]
    hd = RW_HEAD_DIM
    vec = pl.BlockSpec((n, 2 * hd), lambda p: (0, p))
    par = pl.BlockSpec((1, 2 * hd), lambda p: (0, p))
    st = pl.BlockSpec((2, hd, hd, n), lambda p: (p, 0, 0, 0))
    ins = [rw[nm] for nm in ("r", "kmod", "v", "kkn", "bvec", "logw", "bonus", "g")]
    return pl.pallas_call(
        _rwstep_kernel,
        grid=(RW_HEADS // 2,),
        in_specs=[st] + [vec] * 8 + [par, par],
        out_specs=[st, vec],
        out_shape=[jax.ShapeDtypeStruct(state_t.shape, F32), jax.ShapeDtypeStruct((n, RW_WIDTH), F32)],
        scratch_shapes=[pltpu.VMEM((2 * hd, n), F32), pltpu.VMEM((2 * hd, n), F32)],
        compiler_params=_cparams("parallel"),
        name="rw_step",
    )(state_t, *ins, wts["lnx_g"], wts["lnx_b"])


def _rms(x, g):
    return x * lax.rsqrt(jnp.mean(x * x, axis=-1, keepdims=True) + RMS_EPS) * g


def _mla_common(low, pos, gq, gkv, wqa, wqb, invf, q_scale):
    cq = _rms(low[:, :Q_LORA], gq)
    ckv = _rms(low[:, Q_LORA:Q_LORA + KV_LORA], gkv)
    kr = low[:, Q_LORA + KV_LORA:Q_LORA + KV_LORA + LANES]
    kr_rot = low[:, Q_LORA + KV_LORA + LANES:]
    ang = pos * invf
    cos = jnp.cos(ang)
    sin = jnp.sin(ang)
    cos8 = jnp.concatenate([cos] * MLA_HEADS, axis=1)
    sin8 = jnp.concatenate([sin] * MLA_HEADS, axis=1)
    q = (_mm(cq, wqa) * cos8 + _mm(cq, wqb) * sin8) * q_scale
    k_rope = kr * pltpu.roll(cos, LANES - QK_NOPE, 1) + kr_rot * pltpu.roll(sin, LANES - QK_NOPE, 1)
    return q, ckv, k_rope


def _inproj_prompt_kernel(x_ref, sh_ref, sc_ref, wrw_ref, wmla_ref, wg_ref, gq_ref, gkv_ref, wqa_ref, wqb_ref, wuk_ref,
                          wuv_ref, invf_ref, prw_ref, gs_ref, q_ref, k_ref, v_ref, ckv_ref, kr_ref):
    i = pl.program_id(1)
    rows = x_ref.shape[0]
    h = (x_ref[...] * (1 + sc_ref[...]) + sh_ref[...]).astype(BF16)
    prw_ref[...] = jnp.dot(h, wrw_ref[...], preferred_element_type=F32)
    gs_ref[...] = jax.nn.sigmoid(jnp.dot(h, wg_ref[...], preferred_element_type=F32))
    low = jnp.dot(h, wmla_ref[...], preferred_element_type=F32)
    pos = (i * rows + _iota((rows, 1), 0)).astype(F32)
    q, ckv, k_rope = _mla_common(low, pos, gq_ref[...], gkv_ref[...], wqa_ref[...], wqb_ref[...], invf_ref[...],
                                 ATTN_SCALE * math.log2(math.e))
    q_ref[...] = q.astype(BF16)
    ckv_ref[...] = ckv
    kr_ref[...] = k_rope[:, :QK_ROPE]
    k_slot = pltpu.roll(k_rope, QK_NOPE, 1)
    k_ref[...] = (_mm(ckv, wuk_ref[...]) + jnp.concatenate([k_slot] * MLA_HEADS, axis=1)).astype(BF16)
    v_ref[...] = _mm_t(wuv_ref[...], ckv).astype(BF16)


def _inproj_prompt(x, mod, wts, tile):
    bsz, t, d = x.shape
    params = [wts["w_rw"], wts["w_mla"], wts["w_gate"], wts["g_qnorm"], wts["g_kvnorm"], wts["w_qa"], wts["w_qb"],
              wts["w_uk_pad"], wts["w_uv_t"], wts["invf"]]
    row = lambda w: pl.BlockSpec((None, tile, w), lambda b, i: (b, i, 0))
    slots = MLA_HEADS * HEAD_SLOT
    return pl.pallas_call(
        _inproj_prompt_kernel,
        grid=(bsz, t // tile),
        in_specs=[row(d), mod.spec(0), mod.spec(1)] + [_const_spec(a) for a in params],
        out_specs=[row(RW_COLS), row(2 * d), row(slots), row(slots),
                   pl.BlockSpec((None, MLA_HEADS * V_HEAD, tile), lambda b, i: (b, 0, i)), row(KV_LORA), row(QK_ROPE)],
        out_shape=[jax.ShapeDtypeStruct((bsz, t, RW_COLS), F32), jax.ShapeDtypeStruct((bsz, t, 2 * d), F32),
                   jax.ShapeDtypeStruct((bsz, t, slots), BF16), jax.ShapeDtypeStruct((bsz, t, slots), BF16),
                   jax.ShapeDtypeStruct((bsz, MLA_HEADS * V_HEAD, t), BF16),
                   jax.ShapeDtypeStruct((bsz, t, KV_LORA), F32), jax.ShapeDtypeStruct((bsz, t, QK_ROPE), F32)],
        compiler_params=_cparams("parallel", "arbitrary"),
        name="inproj_prompt",
    )(x, mod.arr, mod.arr, *params)


def _mlaprep_sample_kernel(low_ref, gq_ref, gkv_ref, wqa_ref, wqb_ref, wukt_ref, invf_ref, pos_ref,
                           qlat_ref, qr_ref, ckv_ref, kr_ref):
    q, ckv, k_rope = _mla_common(low_ref[...], pos_ref[...], gq_ref[...], gkv_ref[...], wqa_ref[...], wqb_ref[...],
                                 invf_ref[...], ATTN_SCALE)
    ckv_ref[...] = ckv
    kr_ref[...] = k_rope[:, :QK_ROPE]
    rope_lanes = _iota((1, HEAD_SLOT), 1) < QK_ROPE
    for h in range(MLA_HEADS):
        slot = q[:, HEAD_SLOT * h:HEAD_SLOT * (h + 1)]
        qlat_ref[:, KV_LORA * h:KV_LORA * (h + 1)] = _mm(slot, wukt_ref[h])
        qr_ref[:, HEAD_SLOT * h:HEAD_SLOT * (h + 1)] = jnp.where(rope_lanes, pltpu.roll(slot, LANES - QK_NOPE, 1), 0.0)


def _mlaprep_sample(low, pos, wts):
    n = low.shape[0]
    params = [wts["g_qnorm"], wts["g_kvnorm"], wts["w_qa"], wts["w_qb"], wts["w_ukt_pad"], wts["invf"], pos]
    full = lambda shape: pl.BlockSpec(shape, lambda i: (0,) * len(shape))
    return pl.pallas_call(
        _mlaprep_sample_kernel,
        grid=(1,),
        in_specs=[_const_spec(low)] + [_const_spec(a) for a in params],
        out_specs=[full((n, MLA_HEADS * KV_LORA)), full((n, MLA_HEADS * HEAD_SLOT)), full((n, KV_LORA)),
                   full((n, QK_ROPE))],
        out_shape=[jax.ShapeDtypeStruct((n, MLA_HEADS * KV_LORA), F32),
                   jax.ShapeDtypeStruct((n, MLA_HEADS * HEAD_SLOT), F32),
                   jax.ShapeDtypeStruct((n, KV_LORA), F32), jax.ShapeDtypeStruct((n, QK_ROPE), F32)],
        compiler_params=_cparams("arbitrary"),
        name="mlaprep_sample",
    )(low, *params)


def _attn_kernel(qi_ref, ki_ref, q_ref, k_ref, vt_ref, o_ref, m_scr, l_scr, acc_scr):
    step = pl.program_id(2)
    qi = qi_ref[step]
    ki = ki_ref[step]
    bq = q_ref.shape[0]
    bk = k_ref.shape[0]
    on_diagonal = ki == (qi * bq) // bk

    @pl.when(ki == 0)
    def _():
        m_scr[...] = jnp.full(m_scr.shape, -jnp.inf, F32)
        l_scr[...] = jnp.zeros_like(l_scr)
        acc_scr[...] = jnp.zeros_like(acc_scr)

    n_heads = m_scr.shape[0]

    def accumulate(masked):
        scores = [_mm_t(k_ref[:, HEAD_SLOT * h:HEAD_SLOT * (h + 1)], q_ref[:, HEAD_SLOT * h:HEAD_SLOT * (h + 1)])
                  for h in range(n_heads)]
        if masked:
            visible = ki * bk + _iota((bk, bq), 0) <= qi * bq + _iota((bk, bq), 1)
            scores = [jnp.where(visible, s, -jnp.inf) for s in scores]
        for h, s in enumerate(scores):
            m_prev = m_scr[h]
            m_new = jnp.maximum(m_prev, jnp.max(s, axis=0, keepdims=True))
            p = jnp.exp2(s - m_new)
            alpha = jnp.exp2(m_prev - m_new)
            l_scr[h] = alpha * l_scr[h] + jnp.sum(p, axis=0, keepdims=True)
            acc_scr[h] = alpha * acc_scr[h] + _mm(vt_ref[V_HEAD * h:V_HEAD * (h + 1), :], p)
            m_scr[h] = m_new

    @pl.when(jnp.logical_not(on_diagonal))
    def _():
        accumulate(False)

    @pl.when(on_diagonal)
    def _():
        accumulate(True)
        out_t = jnp.concatenate([acc_scr[h] / l_scr[h] for h in range(n_heads)], axis=0)
        o_ref[...] = out_t.T


def _attention_prompt(q, k, v_t, bq, bk):
    bsz, t, _ = q.shape
    assert bk % bq == 0
    hps = ATTN_HEADS_PER_STEP
    pairs = [(qi, ki) for qi in range(t // bq) for ki in range((qi * bq) // bk + 1)]
    qi_of = jnp.asarray([p[0] for p in pairs], jnp.int32)
    ki_of = jnp.asarray([p[1] for p in pairs], jnp.int32)
    grid_spec = pltpu.PrefetchScalarGridSpec(
        num_scalar_prefetch=2,
        grid=(bsz, MLA_HEADS // hps, len(pairs)),
        in_specs=[pl.BlockSpec((None, bq, hps * HEAD_SLOT), lambda b, g, s, qi, ki: (b, qi[s], g)),
                  pl.BlockSpec((None, bk, hps * HEAD_SLOT), lambda b, g, s, qi, ki: (b, ki[s], g)),
                  pl.BlockSpec((None, hps * V_HEAD, bk), lambda b, g, s, qi, ki: (b, g, ki[s]))],
        out_specs=pl.BlockSpec((None, bq, hps * V_HEAD), lambda b, g, s, qi, ki: (b, qi[s], g)),
        scratch_shapes=[pltpu.VMEM((hps, 1, bq), F32), pltpu.VMEM((hps, 1, bq), F32),
                        pltpu.VMEM((hps, V_HEAD, bq), F32)],
    )
    return pl.pallas_call(
        _attn_kernel,
        grid_spec=grid_spec,
        out_shape=jax.ShapeDtypeStruct((bsz, t, MLA_HEADS * V_HEAD), F32),
        compiler_params=_cparams("parallel", "parallel", "arbitrary"),
        name="attn_prompt",
    )(qi_of, ki_of, q, k, v_t)


def _sample_attn_kernel(pt_ref, qlat_ref, qr_ref, cn_ref, kn_ref, ckv_hbm, kr_hbm, o_ref, kbuf, rbuf, s_scr, sem):
    b = pl.program_id(0)
    n_seq = pl.num_programs(0)
    n_pages = kbuf.shape[1]
    slot = lax.rem(b, 2)

    def page_copies(seq, slt):
        out = []
        for pg in range(n_pages):
            page = pt_ref[seq, pg]
            out.append(pltpu.make_async_copy(ckv_hbm.at[page], kbuf.at[slt, pg], sem.at[slt, 0]))
            out.append(pltpu.make_async_copy(kr_hbm.at[page], rbuf.at[slt, pg], sem.at[slt, 1]))
        return out

    @pl.when(b == 0)
    def _():
        for cp in page_copies(0, 0):
            cp.start()

    @pl.when(b + 1 < n_seq)
    def _():
        for cp in page_copies(b + 1, 1 - slot):
            cp.start()

    for cp in page_copies(b, slot):
        cp.wait()

    ql = qlat_ref[...]
    qr = qr_ref[:, :QK_ROPE]
    n_chunks = n_pages // 2
    rows = 2 * PAGE_SIZE

    def score(i, carry):
        kc = kbuf[slot, pl.ds(2 * i, 2)].reshape(rows, KV_LORA)
        rc = jnp.concatenate([rbuf[slot, 2 * i], rbuf[slot, 2 * i + 1]], axis=1)
        s_scr[i] = _mm_t(ql, kc) + jnp.dot(qr, rc, preferred_element_type=F32)
        return carry

    lax.fori_loop(0, n_chunks, score, 0, unroll=SAMPLE_ATTN_UNROLL)
    cn = cn_ref[...]
    kn = kn_ref[...]
    s_new = jnp.sum(ql * cn, axis=-1, keepdims=True) + jnp.sum(qr * kn, axis=-1, keepdims=True)
    s_all = s_scr[...]
    m = jnp.maximum(jnp.max(jnp.max(s_all, axis=0), axis=-1, keepdims=True), s_new)
    p_all = jnp.exp(s_all - m)
    p_new = jnp.exp(s_new - m)
    denom = jnp.sum(jnp.sum(p_all, axis=0), axis=-1, keepdims=True) + p_new
    s_scr[...] = p_all

    def accumulate(i, acc):
        kc = kbuf[slot, pl.ds(2 * i, 2)].reshape(rows, KV_LORA)
        return acc + _mm(s_scr[i], kc)

    acc = lax.fori_loop(0, n_chunks, accumulate, jnp.zeros((ql.shape[0], KV_LORA), F32), unroll=SAMPLE_ATTN_UNROLL)
    o_ref[...] = (acc + p_new * cn) / denom


def _attention_sample(page_table, qlat, qr, ckv_new, kr_new, cache_ckv, cache_kr):
    n, n_pages = page_table.shape
    grid_spec = pltpu.PrefetchScalarGridSpec(
        num_scalar_prefetch=1,
        grid=(n,),
        in_specs=[pl.BlockSpec((None, MLA_HEADS, KV_LORA), lambda b, pt: (b, 0, 0)),
                  pl.BlockSpec((None, MLA_HEADS, HEAD_SLOT), lambda b, pt: (b, 0, 0)),
                  pl.BlockSpec((None, 1, KV_LORA), lambda b, pt: (b, 0, 0)),
                  pl.BlockSpec((None, 1, QK_ROPE), lambda b, pt: (b, 0, 0)),
                  pl.BlockSpec(memory_space=pl.ANY),
                  pl.BlockSpec(memory_space=pl.ANY)],
        out_specs=pl.BlockSpec((None, MLA_HEADS, KV_LORA), lambda b, pt: (b, 0, 0)),
        scratch_shapes=[pltpu.VMEM((2, n_pages, PAGE_SIZE, KV_LORA), F32),
                        pltpu.VMEM((2, n_pages, QK_ROPE, PAGE_SIZE), F32),
                        pltpu.VMEM((n_pages // 2, MLA_HEADS, 2 * PAGE_SIZE), F32),
                        pltpu.SemaphoreType.DMA((2, 2))],
    )
    return pl.pallas_call(
        _sample_attn_kernel,
        grid_spec=grid_spec,
        out_shape=jax.ShapeDtypeStruct((n, MLA_HEADS, KV_LORA), F32),
        compiler_params=_cparams("arbitrary"),
        name="attn_sample",
    )(page_table, qlat.reshape(n, MLA_HEADS, KV_LORA), qr.reshape(n, MLA_HEADS, HEAD_SLOT),
      ckv_new.reshape(n, 1, KV_LORA), kr_new.reshape(n, 1, QK_ROPE), cache_ckv, cache_kr)


def _uv_kernel(ol_ref, wuv_ref, o_ref):
    outs = [_mm(ol_ref[:, KV_LORA * h:KV_LORA * (h + 1)], wuv_ref[:, V_HEAD * h:V_HEAD * (h + 1)])
            for h in range(MLA_HEADS)]
    o_ref[...] = jnp.concatenate(outs, axis=1)


def _latent_to_heads(o_lat, w_uv):
    n = o_lat.shape[0]
    o_lat = o_lat.reshape(n, MLA_HEADS * KV_LORA)
    return pl.pallas_call(
        _uv_kernel,
        grid=(1,),
        in_specs=[_const_spec(o_lat), _const_spec(w_uv)],
        out_specs=pl.BlockSpec((n, MLA_HEADS * V_HEAD), lambda i: (0, 0)),
        out_shape=jax.ShapeDtypeStruct((n, MLA_HEADS * V_HEAD), F32),
        compiler_params=_cparams("arbitrary"),
        name="latent_to_heads",
    )(o_lat, w_uv)


def _layernorm(z, g, b):
    mu = jnp.mean(z, axis=-1, keepdims=True)
    d = z - mu
    var = jnp.mean(d * d, axis=-1, keepdims=True)
    return d * lax.rsqrt(var + LN_EPS) * g + b


def _first_max(x, idx, sentinel):
    mx = jnp.max(x, axis=0, keepdims=True)
    return jnp.min(jnp.where(x == mx, idx, sentinel), axis=0, keepdims=True)


def _route(scores, bias):
    n_tok = scores.shape[1]
    per_group = N_EXPERTS // N_GROUPS
    sb = scores + bias
    sb3 = sb.reshape(N_GROUPS, per_group, n_tok)
    member = _iota(sb3.shape, 1)
    m1 = jnp.max(sb3, axis=1, keepdims=True)
    first = jnp.min(jnp.where(sb3 == m1, member, per_group), axis=1, keepdims=True)
    m2 = jnp.max(jnp.where(member == first, -jnp.inf, sb3), axis=1, keepdims=True)
    g_score = (m1 + m2).reshape(N_GROUPS, n_tok)
    g_idx = _iota(g_score.shape, 0)
    g_sel = jnp.zeros(g_score.shape, F32)
    for _ in range(TOPK_GROUPS):
        hit = g_idx == _first_max(g_score, g_idx, N_GROUPS)
        g_sel = jnp.where(hit, 1.0, g_sel)
        g_score = jnp.where(hit, -jnp.inf, g_score)
    e_mask = jnp.broadcast_to(g_sel.reshape(N_GROUPS, 1, n_tok), sb3.shape).reshape(N_EXPERTS, n_tok) > 0.5
    cur = jnp.where(e_mask, sb, -jnp.inf)
    e_idx = _iota(cur.shape, 0)
    sel = jnp.zeros(cur.shape, F32)
    for _ in range(TOP_K):
        hit = e_idx == _first_max(cur, e_idx, N_EXPERTS)
        sel = jnp.where(hit, 1.0, sel)
        cur = jnp.where(hit, -jnp.inf, cur)
    picked = jnp.where(sel > 0.5, scores, 0.0)
    return picked / jnp.sum(picked, axis=0, keepdims=True) * ROUTED_SCALE


def _merge_kernel(x_ref, oa_ref, ob_ref, gs_ref, gt1_ref, sh2_ref, sc2_ref, wba_ref, wbb_ref, wout_ref, g1_ref, b1_ref,
                  wrt_ref, br_ref, x1_ref, h2_ref, cw_ref):
    d = D_MODEL
    ya = _mm(oa_ref[...], wba_ref[...])
    yb = _mm(ob_ref[...], wbb_ref[...])
    merged = gs_ref[:, :d] * ya + gs_ref[:, d:] * yb
    z = DN_ALPHA * x_ref[...] + gt1_ref[...] * _mm(merged, wout_ref[...])
    x1 = _layernorm(z, g1_ref[...], b1_ref[...])
    x1_ref[...] = x1
    h2 = x1 * (1 + sc2_ref[...]) + sh2_ref[...]
    h2_ref[...] = h2.astype(BF16)
    scores = jax.nn.sigmoid(_mm3_t(wrt_ref[...], h2))
    cw_ref[...] = _route(scores, br_ref[...]).T


def _merge(x3, oa, ob, gs, mod, wts, tile):
    bsz, t, d = x3.shape
    params = [wts["w_branch_a"], wts["w_branch_b"], wts["w_out"], wts["ln1_g"], wts["ln1_b"], wts["w_router_t"],
              wts["b_router"]]
    row = lambda w: pl.BlockSpec((None, tile, w), lambda b, i: (b, i, 0))
    return pl.pallas_call(
        _merge_kernel,
        grid=(bsz, t // tile),
        in_specs=[row(d), row(RW_WIDTH), row(MLA_HEADS * V_HEAD), row(2 * d), mod.spec(2), mod.spec(3), mod.spec(4)]
        + [_const_spec(a) for a in params],
        out_specs=[row(d), row(d), row(N_EXPERTS)],
        out_shape=[jax.ShapeDtypeStruct((bsz, t, d), F32), jax.ShapeDtypeStruct((bsz, t, d), BF16),
                   jax.ShapeDtypeStruct((bsz, t, N_EXPERTS), F32)],
        compiler_params=_cparams("parallel", "arbitrary"),
        name="merge",
    )(x3, oa, ob, gs, mod.arr, mod.arr, mod.arr, *params)


def _experts_kernel(h_ref, cw_ref, wg_ref, wu_ref, wd_ref, o_ref):
    step = pl.program_id(1)
    n_e = wg_ref.shape[0]

    @pl.when(step == 0)
    def _():
        o_ref[...] = jnp.zeros_like(o_ref)

    x = h_ref[...]
    cw = cw_ref[...]
    lane = _iota((1, N_EXPERTS), 1)
    acts = []
    for e in range(n_e):
        act = _silu(_mm(x, wg_ref[e])) * _mm(x, wu_ref[e])
        w_col = jnp.sum(jnp.where(lane == step * n_e + e, cw, 0.0), axis=1, keepdims=True)
        acts.append((act * w_col).astype(BF16))
    o_ref[...] += _mm(jnp.concatenate(acts, axis=1), wd_ref[...].reshape(n_e * EXPERT_FF, -1))


def _experts(h2, cw, wts, tile):
    n, d = h2.shape
    n_e = EXPERTS_PER_STEP
    return pl.pallas_call(
        _experts_kernel,
        grid=(n // tile, N_EXPERTS // n_e),
        in_specs=[pl.BlockSpec((tile, d), lambda i, e: (i, 0)),
                  pl.BlockSpec((tile, N_EXPERTS), lambda i, e: (i, 0)),
                  pl.BlockSpec((n_e, d, EXPERT_FF), lambda i, e: (e, 0, 0)),
                  pl.BlockSpec((n_e, d, EXPERT_FF), lambda i, e: (e, 0, 0)),
                  pl.BlockSpec((n_e, EXPERT_FF, d), lambda i, e: (e, 0, 0))],
        out_specs=pl.BlockSpec((tile, d), lambda i, e: (i, 0)),
        out_shape=jax.ShapeDtypeStruct((n, d), F32),
        compiler_params=_cparams("parallel", "arbitrary"),
        name="experts",
    )(h2, cw, wts["w_exp_gate"], wts["w_exp_up"], wts["w_exp_down"])


def _final_kernel(x1_ref, h2_ref, routed_ref, gt2_ref, wsg_ref, wsu_ref, wsd_ref, g2_ref, b2_ref, y_ref):
    h2 = h2_ref[...]
    act = _silu(jnp.dot(h2, wsg_ref[...], preferred_element_type=F32)) * jnp.dot(h2, wsu_ref[...],
                                                                               preferred_element_type=F32)
    ffn = routed_ref[...] + _mm(act, wsd_ref[...])
    y_ref[...] = _layernorm(DN_ALPHA * x1_ref[...] + gt2_ref[...] * ffn, g2_ref[...], b2_ref[...])


def _final(x1, h2, routed, mod, wts, tile):
    bsz, t, d = x1.shape
    params = [wts["w_sh_gate"], wts["w_sh_up"], wts["w_sh_down"], wts["ln2_g"], wts["ln2_b"]]
    row = lambda w: pl.BlockSpec((None, tile, w), lambda b, i: (b, i, 0))
    return pl.pallas_call(
        _final_kernel,
        grid=(bsz, t // tile),
        in_specs=[row(d), row(d), row(d), mod.spec(5)] + [_const_spec(a) for a in params],
        out_specs=row(d),
        out_shape=jax.ShapeDtypeStruct((bsz, t, d), F32),
        compiler_params=_cparams("parallel", "arbitrary"),
        name="final",
    )(x1, h2, routed, mod.arr, *params)


def _rope_rotation_columns(w):
    half = QK_ROPE // 2
    return jnp.concatenate([-w[..., half:], w[..., :half]], axis=-1)


def _prepare_weights(P):
    d = D_MODEL
    w = {}
    w_in = P["w_in"]
    o1 = RW_COLS
    o2 = o1 + Q_LORA
    o3 = o2 + KV_LORA
    o4 = o3 + QK_ROPE
    w_kr = w_in[:, o3:o4]
    pad = jnp.zeros((d, LANES - QK_ROPE), F32)
    w["w_rw"] = w_in[:, :o1].astype(BF16)
    w["w_mla"] = jnp.concatenate([w_in[:, o1:o3], w_kr, pad, _rope_rotation_columns(w_kr), pad], axis=1).astype(BF16)
    w["w_gate"] = w_in[:, o4:].astype(BF16)

    row = lambda v: v.reshape(1, -1)
    w["mu_shift"] = row(P["mu_shift"])
    zl = jnp.zeros((DECAY_LORA, RW_WIDTH), F32)
    w["w_lora"] = jnp.concatenate([jnp.concatenate([P["w_decay_up"], zl], axis=1),
                                   jnp.concatenate([zl, P["w_iclr_up"]], axis=1)], axis=0).astype(BF16)
    w["w_gate_up"] = P["w_gate_up"].astype(BF16)
    for nm in ("w_decay0", "a0", "k_k", "k_a", "r_k", "lnx_g", "lnx_b", "g_qnorm", "g_kvnorm", "ln1_g", "ln1_b",
               "ln2_g", "ln2_b"):
        w[nm] = row(P[nm])
    head_of = np.arange(RW_WIDTH) // RW_HEAD_DIM
    w["ones_blk"] = jnp.asarray(head_of[:, None] == head_of[None, :], BF16)

    wq = P["w_uq"].reshape(Q_LORA, MLA_HEADS, QK_NOPE + QK_ROPE)
    zq = jnp.zeros((Q_LORA, MLA_HEADS, HEAD_SLOT - QK_NOPE - QK_ROPE), F32)
    w["w_qa"] = jnp.concatenate([wq, zq], axis=-1).reshape(Q_LORA, MLA_HEADS * HEAD_SLOT).astype(BF16)
    w["w_qb"] = jnp.concatenate([jnp.zeros((Q_LORA, MLA_HEADS, QK_NOPE), F32), _rope_rotation_columns(wq[..., QK_NOPE:]), zq],
                                axis=-1).reshape(Q_LORA, MLA_HEADS * HEAD_SLOT).astype(BF16)
    wuk = P["w_uk"].reshape(KV_LORA, MLA_HEADS, QK_NOPE)
    zk = jnp.zeros((KV_LORA, MLA_HEADS, HEAD_SLOT - QK_NOPE), F32)
    w["w_uk_pad"] = jnp.concatenate([wuk, zk], axis=-1).reshape(KV_LORA, MLA_HEADS * HEAD_SLOT).astype(BF16)
    w["w_ukt_pad"] = jnp.transpose(jnp.concatenate([wuk, zk], axis=-1), (1, 2, 0)).astype(BF16)
    w["w_uv"] = P["w_uv"].astype(BF16)
    w["w_uv_t"] = P["w_uv"].T.astype(BF16)
    half = QK_ROPE // 2
    inv = ROPE_THETA ** (-jnp.arange(half, dtype=F32) / half)
    w["invf"] = jnp.concatenate([jnp.zeros((QK_NOPE,), F32), inv, inv,
                                 jnp.zeros((HEAD_SLOT - QK_NOPE - QK_ROPE,), F32)]).reshape(1, HEAD_SLOT)

    for nm in ("w_branch_a", "w_branch_b", "w_out", "w_sh_gate", "w_sh_up", "w_sh_down"):
        w[nm] = P[nm].astype(BF16)
    for nm in ("w_exp_gate", "w_exp_up", "w_exp_down"):
        w[nm] = P[nm]
    w["w_router_t"] = P["w_router"].T
    w["b_router"] = P["b_router"].reshape(N_EXPERTS, 1)
    return w


def _row_tile(t, cap):
    tile = min(t, cap)
    assert t % tile == 0 and tile % 16 == 0, (t, tile)
    return tile


def _layer_tail(x3, oa, ob, gs, mod, wts, tile, moe_tile):
    bsz, t, d = x3.shape
    x1, h2, cw = _merge(x3, oa, ob, gs, mod, wts, tile)
    routed = _experts(h2.reshape(bsz * t, d), cw.reshape(bsz * t, N_EXPERTS), wts, moe_tile).reshape(bsz, t, d)
    return _final(x1, h2, routed, mod, wts, tile)


def _layer_prompt(x, mod_rows, wts):
    bsz, t, d = x.shape
    tile = _row_tile(t, 256)
    mod = _Mod(mod_rows.reshape(bsz * 6, 1, d), per_token=False, tile=tile)
    p_rw, gs, q, k, v_t, ckv, k_rope = _inproj_prompt(x, mod, wts, tile)
    oa, state = _rw_scan(p_rw, jnp.zeros((bsz, 1, RW_COLS), F32), wts)
    ob = _attention_prompt(q, k, v_t, _row_tile(t, 512), _row_tile(t, 1024))
    y = _layer_tail(x, oa, ob, gs, mod, wts, _row_tile(t, 512), _row_tile(bsz * t, 1024))
    return y, ckv, k_rope, _unpack_state(state), p_rw[:, -1]


def _layer_sample(x, mod_rows, state, shift, page_table, cache_ckv, cache_kr, wts):
    n, s_new, d = x.shape
    assert s_new == 1
    past = page_table.shape[1] * PAGE_SIZE
    x3 = x.reshape(1, n, d)
    mod = _Mod(mod_rows, per_token=True, tile=n)
    p_rw, low, gs = _inproj(x3, mod, wts, n)
    p_rw2 = p_rw.reshape(n, RW_COLS)
    rw = dict(zip(RW_OUT_NAMES, _rwprep_sample(p_rw2, shift, wts)))
    new_state_t, oa = _rw_step(jnp.transpose(state, (1, 2, 3, 0)), rw, wts)
    new_state = jnp.transpose(new_state_t, (3, 0, 1, 2))
    pos = jnp.full((n, 1), past, F32)
    qlat, qr, ckv, k_rope = _mlaprep_sample(low.reshape(n, MLA_LOW_COLS), pos, wts)
    o_lat = _attention_sample(page_table, qlat, qr, ckv, k_rope, cache_ckv, jnp.swapaxes(cache_kr, 1, 2))
    ob = _latent_to_heads(o_lat, wts["w_uv"])
    y = _layer_tail(x3, oa.reshape(1, n, RW_WIDTH), ob.reshape(1, n, MLA_HEADS * V_HEAD), gs, mod, wts, n, n)
    return (y.reshape(n, 1, d), ckv.reshape(n, 1, KV_LORA), k_rope.reshape(n, 1, QK_ROPE), new_state, p_rw2)


def kernel(x_prompt, x_sample, c_prompt, c_sample, cache_ckv, cache_krope, state_wkv, state_shift, page_table, w_ada, b_ada, w_in, mu_shift, w_decay0, w_decay_up, a0, w_iclr_up, w_gate_up, k_k, k_a, r_k, lnx_g, lnx_b, w_branch_a, g_qnorm, w_uq, g_kvnorm, w_uk, w_uv, w_branch_b, w_out, ln1_g, ln1_b, w_router, b_router, w_exp_gate, w_exp_up, w_exp_down, w_sh_gate, w_sh_up, w_sh_down, ln2_g, ln2_b):
    params = dict(w_ada=w_ada, b_ada=b_ada, w_in=w_in, mu_shift=mu_shift, w_decay0=w_decay0, w_decay_up=w_decay_up,
                  a0=a0, w_iclr_up=w_iclr_up, w_gate_up=w_gate_up, k_k=k_k, k_a=k_a, r_k=r_k, lnx_g=lnx_g,
                  lnx_b=lnx_b, w_branch_a=w_branch_a, g_qnorm=g_qnorm, w_uq=w_uq, g_kvnorm=g_kvnorm, w_uk=w_uk,
                  w_uv=w_uv, w_branch_b=w_branch_b, w_out=w_out, ln1_g=ln1_g, ln1_b=ln1_b, w_router=w_router,
                  b_router=b_router, w_exp_gate=w_exp_gate, w_exp_up=w_exp_up, w_exp_down=w_exp_down,
                  w_sh_gate=w_sh_gate, w_sh_up=w_sh_up, w_sh_down=w_sh_down, ln2_g=ln2_g, ln2_b=ln2_b)
    depth = w_in.shape[0]
    bp = x_prompt.shape[0]
    bd = x_sample.shape[0]
    n_c = bp + bd
    c_all = jnp.concatenate([c_prompt, c_sample, jnp.zeros((-n_c % 8, D_MODEL), F32)], axis=0)
    yp, ys = x_prompt, x_sample
    outs = [[] for _ in range(8)]
    for l in range(depth):
        wts = _prepare_weights({name: arr[l] for name, arr in params.items()})
        mod = _adaln_mod(c_all, params["w_ada"][l], params["b_ada"][l].reshape(1, -1))
        yp, *rest_p = _layer_prompt(yp, mod[:bp], wts)
        ys, *rest_s = _layer_sample(ys, mod[bp:n_c], state_wkv[l], state_shift[l], page_table, cache_ckv[l],
                                    cache_krope[l], wts)
        for acc, val in zip(outs, rest_p + rest_s):
            acc.append(val)
    return (yp, ys) + tuple(jnp.stack(o) for o in outs)
```

```python
import math

import numpy as np
import jax
import jax.numpy as jnp
from jax import lax
from jax.experimental import pallas as pl
from jax.experimental.pallas import tpu as pltpu

F32 = jnp.float32
BF16 = jnp.bfloat16

D_MODEL = 1024
PAGE_SIZE = 128
RW_HEADS = 8
RW_HEAD_DIM = 64
RW_WIDTH = RW_HEADS * RW_HEAD_DIM
DECAY_LORA = 64
ICLR_LORA = 64
GATE_LORA = 128
RW_COLS = 3 * RW_WIDTH + DECAY_LORA + ICLR_LORA + GATE_LORA
GN_EPS = 64e-5
MLA_HEADS = 8
QK_NOPE = 64
QK_ROPE = 32
V_HEAD = 64
Q_LORA = 384
KV_LORA = 256
ROPE_THETA = 10000.0
ATTN_SCALE = (QK_NOPE + QK_ROPE) ** -0.5
N_EXPERTS = 64
TOP_K = 8
N_GROUPS = 8
TOPK_GROUPS = 4
EXPERT_FF = 256
SHARED_FF = 256
ROUTED_SCALE = 2.5
DEPTH = 1
DN_ALPHA = (2 * DEPTH) ** 0.25
LN_EPS = 1e-5
RMS_EPS = 1e-6

LANES = 128
HEAD_SLOT = LANES
MLA_LOW_COLS = Q_LORA + KV_LORA + 2 * LANES
SCAN_CHUNK = 64
SCAN_SUBCHUNKS = 8
SAMPLE_ATTN_UNROLL = 32
ATTN_HEADS_PER_STEP = 4
EXPERTS_PER_STEP = 4
VMEM_LIMIT = 56 * 1024 * 1024


def _cparams(*sem):
    return pltpu.CompilerParams(dimension_semantics=sem, vmem_limit_bytes=VMEM_LIMIT)


def _mm(a, b):
    return jnp.dot(a.astype(BF16), b.astype(BF16), preferred_element_type=F32)


def _mm_t(a, b):
    return lax.dot_general(a.astype(BF16), b.astype(BF16), (((1,), (1,)), ((), ())), preferred_element_type=F32)


def _mm_tl(a, b):
    return lax.dot_general(a.astype(BF16), b.astype(BF16), (((0,), (0,)), ((), ())), preferred_element_type=F32)


def _split(x):
    hi = x.astype(BF16)
    lo = (x - hi.astype(F32)).astype(BF16)
    return hi, lo


def _three_pass(f, a, b):
    ah, al = _split(a)
    bh, bl = _split(b)
    return f(ah, bh) + (f(ah, bl) + f(al, bh))


def _mm3_t(a, b):
    return _three_pass(_mm_t, a, b)


def _mm_exact_rhs(a, b_exact):
    ah, al = _split(a)
    return _mm(ah, b_exact) + _mm(al, b_exact)


def _silu(x):
    return x * jax.nn.sigmoid(x)


def _iota(shape, dim):
    return lax.broadcasted_iota(jnp.int32, shape, dim)


def _mod_kernel(c_ref, w_ref, b_ref, o_ref):
    o_ref[...] = _mm(_silu(c_ref[...]), w_ref[...]) + b_ref[...]


def _adaln_mod(c_all, w_ada, b_ada):
    n = c_all.shape[0]
    d = D_MODEL
    return pl.pallas_call(
        _mod_kernel,
        grid=(6,),
        in_specs=[pl.BlockSpec((n, d), lambda j: (0, 0)),
                  pl.BlockSpec((d, d), lambda j: (0, j)),
                  pl.BlockSpec((1, d), lambda j: (0, j))],
        out_specs=pl.BlockSpec((n, d), lambda j: (0, j)),
        out_shape=jax.ShapeDtypeStruct((n, 6 * d), F32),
        compiler_params=_cparams("arbitrary"),
        name="adaln_mod",
    )(c_all, w_ada, b_ada)


class _Mod:
    def __init__(self, arr, per_token, tile):
        self.arr = arr
        self.per_token = per_token
        self.tile = tile

    def spec(self, j):
        if self.per_token:
            return pl.BlockSpec((self.tile, D_MODEL), lambda b, i, *_: (i, j))
        return pl.BlockSpec((None, 1, D_MODEL), lambda b, i, *_: (b * 6 + j, 0, 0))


def _const_spec(arr):
    nd = arr.ndim
    return pl.BlockSpec(arr.shape, lambda *_: (0,) * nd)


def _inproj_kernel(x_ref, sh_ref, sc_ref, wrw_ref, wmla_ref, wg_ref, prw_ref, mla_ref, gs_ref):
    h = (x_ref[...] * (1 + sc_ref[...]) + sh_ref[...]).astype(BF16)
    prw_ref[...] = jnp.dot(h, wrw_ref[...], preferred_element_type=F32)
    mla_ref[...] = jnp.dot(h, wmla_ref[...], preferred_element_type=F32)
    gs_ref[...] = jax.nn.sigmoid(jnp.dot(h, wg_ref[...], preferred_element_type=F32))


def _inproj(x3, mod, wts, tile):
    bsz, t, d = x3.shape
    row = lambda w: pl.BlockSpec((None, tile, w), lambda b, i: (b, i, 0))
    return pl.pallas_call(
        _inproj_kernel,
        grid=(bsz, t // tile),
        in_specs=[row(d), mod.spec(0), mod.spec(1),
                  _const_spec(wts["w_rw"]), _const_spec(wts["w_mla"]), _const_spec(wts["w_gate"])],
        out_specs=[row(RW_COLS), row(MLA_LOW_COLS), row(2 * d)],
        out_shape=[jax.ShapeDtypeStruct((bsz, t, RW_COLS), F32),
                   jax.ShapeDtypeStruct((bsz, t, MLA_LOW_COLS), F32),
                   jax.ShapeDtypeStruct((bsz, t, 2 * d), F32)],
        compiler_params=_cparams("parallel", "arbitrary"),
        name="inproj",
    )(x3, mod.arr, mod.arr, wts["w_rw"], wts["w_mla"], wts["w_gate"])


RW_OUT_NAMES = ("r", "kmod", "v", "kkn", "bvec", "logw", "bonus", "g")


def _rw_elementwise(p, pp, mu, w_lora, w_gate_up, w_decay0, a0, k_k, k_a, r_k, ones_blk):
    w = RW_WIDTH
    pm = p + (pp - p) * mu
    r, k, v = pm[:, 0:w], pm[:, w:2 * w], pm[:, 2 * w:3 * w]
    xwa = pm[:, 3 * w:3 * w + DECAY_LORA + ICLR_LORA]
    xg = pm[:, 3 * w + DECAY_LORA + ICLR_LORA:]
    lane = _iota((1, DECAY_LORA + ICLR_LORA), 1)
    z = jnp.where(lane < DECAY_LORA, jnp.tanh(xwa), xwa)
    lora = _mm(z, w_lora)
    y = -(w_decay0 + lora[:, :w])
    softplus = jnp.maximum(y, 0.0) + jnp.log1p(jnp.exp(-jnp.abs(y)))
    w_log = -softplus - 0.5
    logw = -jnp.exp(w_log)
    a = jax.nn.sigmoid(a0 + lora[:, w:])
    g = _mm(jax.nn.sigmoid(xg), w_gate_up)
    kk = k * k_k
    kk = kk * lax.rsqrt(_mm_exact_rhs(kk * kk, ones_blk) + 1e-12)
    kmod = k * (1.0 + (a - 1.0) * k_a)
    bonus = _mm_exact_rhs(r * kmod * r_k, ones_blk) * v
    return r, kmod, v, -kk, kk * a, logw, bonus, g


def _rwprep_sample_kernel(p_ref, pp_ref, mu_ref, wl_ref, wgu_ref, wd0_ref, a0_ref, kk_ref, ka_ref, rk_ref,
                          ones_ref, *out_refs):
    outs = _rw_elementwise(p_ref[...], pp_ref[...], mu_ref[...], wl_ref[...], wgu_ref[...], wd0_ref[...], a0_ref[...],
                           kk_ref[...], ka_ref[...], rk_ref[...], ones_ref[...])
    for o_ref, val in zip(out_refs, outs):
        o_ref[...] = val


def _rw_param_list(wts):
    return [wts["mu_shift"], wts["w_lora"], wts["w_gate_up"], wts["w_decay0"], wts["a0"], wts["k_k"], wts["k_a"],
            wts["r_k"], wts["ones_blk"]]


def _rwprep_sample(p_rw, p_prev, wts):
    n = p_rw.shape[0]
    params = _rw_param_list(wts)
    return pl.pallas_call(
        _rwprep_sample_kernel,
        grid=(1,),
        in_specs=[_const_spec(p_rw), _const_spec(p_prev)] + [_const_spec(a) for a in params],
        out_specs=[pl.BlockSpec((n, RW_WIDTH), lambda i: (0, 0))] * 8,
        out_shape=[jax.ShapeDtypeStruct((n, RW_WIDTH), F32)] * 8,
        compiler_params=_cparams("arbitrary"),
        name="rwprep_sample",
    )(p_rw, p_prev, *params)


def _group_norm_pair(o, ones_pair):
    mu = _mm_exact_rhs(o, ones_pair) * (1.0 / RW_HEAD_DIM)
    d = o - mu
    var = _mm_exact_rhs(d * d, ones_pair) * (1.0 / RW_HEAD_DIM)
    return d * lax.rsqrt(var + GN_EPS)


def _scan_kernel(p_ref, s0_ref, mu_ref, wl_ref, wgu_ref, wd0_ref, a0_ref, kk_ref, ka_ref, rk_ref, ones_ref,
                 lng_ref, lnb_ref, o_ref, st_ref, h_scr, prev_scr):
    c = pl.program_id(1)
    n_steps = pl.num_programs(1)
    n_pairs = h_scr.shape[0]
    cl = SCAN_CHUNK
    n_rows = p_ref.shape[0]
    n_sub = n_rows // cl

    @pl.when(c == 0)
    def _():
        h_scr[...] = jnp.zeros_like(h_scr)
        prev_scr[...] = s0_ref[...]

    p = p_ref[...]
    pp = jnp.where(_iota(p.shape, 0) == 0, prev_scr[...], pltpu.roll(p, 1, 0))
    prev_scr[...] = p[n_rows - 1:n_rows, :]
    r_all, k_all, v_all, a_all, b_all, lw_all, bon_all, g_all = _rw_elementwise(
        p, pp, mu_ref[...], wl_ref[...], wgu_ref[...], wd0_ref[...], a0_ref[...], kk_ref[...], ka_ref[...],
        rk_ref[...], ones_ref[...])

    row = _iota((cl, cl), 0)
    col = _iota((cl, cl), 1)
    tri_incl = row >= col
    tri_strict = row > col
    tri_ones = tri_incl.astype(BF16)
    eye = (row == col).astype(F32)
    head0 = _iota((1, LANES), 1) < RW_HEAD_DIM
    head_mask = (head0, jnp.logical_not(head0))
    head0_2 = (_iota((1, 2 * LANES), 1) & RW_HEAD_DIM) == 0
    r128 = _iota((LANES, LANES), 0)
    c128 = _iota((LANES, LANES), 1)
    same_head = (r128 < RW_HEAD_DIM) == (c128 < RW_HEAD_DIM)
    diag = r128 == c128
    ones_pair = same_head.astype(BF16)
    units = [(s, p) for s in range(n_sub) for p in range(n_pairs)]
    heads = [(s, p, h) for (s, p) in units for h in range(2)]
    rows = {s: slice(cl * s, cl * (s + 1)) for s in range(n_sub)}
    lanes = {p: slice(LANES * p, LANES * (p + 1)) for p in range(n_pairs)}
    at = lambda ref, u: ref[rows[u[0]], lanes[u[1]]]

    lw = {u: at(lw_all, u) for u in units}
    lg = {}
    for u in units:
        l1 = lw[u].astype(BF16)
        rem = lw[u] - l1.astype(F32)
        l2 = rem.astype(BF16)
        l3 = (rem - l2.astype(F32)).astype(BF16)
        lg[u] = _mm(tri_ones, l1) + (_mm(tri_ones, l2) + _mm(tri_ones, l3))
    vv = {u: at(v_all, u) for u in units}
    rh, ah, bh, kh, bt, kt, g_last = {}, {}, {}, {}, {}, {}, {}
    for u in units:
        bb = at(b_all, u)
        kk = at(k_all, u)
        lg_last = lg[u][cl - 1:cl, :]
        rh[u] = at(r_all, u) * jnp.exp(lg[u])
        ah[u] = at(a_all, u) * jnp.exp(lg[u] - lw[u])
        inv = jnp.exp(-lg[u])
        bh[u] = bb * inv
        kh[u] = kk * inv
        tail = jnp.exp(lg_last - lg[u])
        bt[u] = bb * tail
        kt[u] = kk * tail
        g_last[u] = jnp.exp(lg_last)

    xb, xk = {}, {}
    for (s, p, h) in heads:
        ar_h = jnp.where(head_mask[h], jnp.concatenate([ah[s, p], rh[s, p]], axis=0), 0.0)
        xb[s, p, h] = _mm_t(ar_h, bh[s, p])
        xk[s, p, h] = _mm_t(ar_h, kh[s, p])
    l_ab = {k: jnp.where(tri_strict, x[:cl], 0.0) for k, x in xb.items()}
    m_rb = {k: jnp.where(tri_incl, x[cl:], 0.0) for k, x in xb.items()}
    l_ak = {k: jnp.where(tri_strict, x[:cl], 0.0) for k, x in xk.items()}
    m_rk = {k: jnp.where(tri_incl, x[cl:], 0.0) for k, x in xk.items()}
    lv_h = {k: _mm(l_ak[k], vv[k[:2]]) for k in heads}
    mrkv_h = {k: _mm(m_rk[k], vv[k[:2]]) for k in heads}
    ktv = {u: _mm_tl(kt[u], vv[u]) for u in units}

    base_bits = 3
    blk = lambda bits: (row >> bits) == (col >> bits)
    sum_half = _iota((1, 2 * cl), 1) >= cl
    x = {k: jnp.concatenate([jnp.where(blk(base_bits), l_ab[k], 0.0), eye], axis=1) for k in heads}
    for _ in range(base_bits):
        x = {k: _mm(x[k][:, :cl], x[k]) + jnp.where(sum_half, x[k], 0.0) for k in heads}
    t_inv = {k: x[k][:, cl:] for k in heads}
    for bits in range(base_bits, int(math.log2(cl))):
        lower_left = jnp.logical_and(blk(bits + 1), jnp.logical_not(blk(bits)))
        y = {k: _mm(jnp.where(lower_left, l_ab[k], 0.0), t_inv[k]) for k in heads}
        t_inv = {k: t_inv[k] + _mm(t_inv[k], y[k]) for k in heads}

    pick = lambda d, u: jnp.where(head0, d[u + (0,)], d[u + (1,)])
    pick2 = lambda d, u: jnp.where(head0_2, d[u + (0,)], d[u + (1,)])
    z = {u: jnp.concatenate([ah[u], pick(lv_h, u)], axis=1) for u in units}
    tz = {k: _mm(t_inv[k], z[k[:2]]) for k in heads}
    w12 = {u: pick2(tz, u) for u in units}
    q12_h = {k: _mm(m_rb[k], w12[k[:2]]) for k in heads}
    g12 = {u: _mm_tl(bt[u], w12[u]) for u in units}
    q1, q2, g1, g2 = {}, {}, {}, {}
    for u in units:
        q12 = pick2(q12_h, u)
        q1[u] = rh[u] + q12[:, :LANES]
        q2[u] = q12[:, LANES:] + pick(mrkv_h, u)
        g1[u] = jnp.where(same_head, g12[u][:, :LANES], 0.0) + jnp.where(diag, g_last[u], 0.0)
        g2[u] = jnp.where(same_head, g12[u][:, LANES:] + ktv[u], 0.0)

    state = {p: h_scr[p] for p in range(n_pairs)}
    outs = {}
    for s in range(n_sub):
        for p in range(n_pairs):
            outs[s, p] = _mm(q1[s, p], state[p]) + q2[s, p]
        state = {p: _mm(g1[s, p], state[p]) + g2[s, p] for p in range(n_pairs)}
    for p in range(n_pairs):
        h_scr[p] = state[p]

    for u in units:
        y = _group_norm_pair(outs[u], ones_pair) * lng_ref[:, lanes[u[1]]] + lnb_ref[:, lanes[u[1]]]
        o_ref[rows[u[0]], lanes[u[1]]] = (y + at(bon_all, u)) * at(g_all, u)

    @pl.when(c == n_steps - 1)
    def _():
        st_ref[...] = h_scr[...]


def _rw_scan(p_rw, shift0, wts):
    bsz, t, _ = p_rw.shape
    w = RW_WIDTH
    n_pairs = w // LANES
    step_rows = SCAN_CHUNK * SCAN_SUBCHUNKS
    assert t % step_rows == 0, (t, step_rows)
    blk = lambda width: pl.BlockSpec((None, step_rows, width), lambda b, c: (b, c, 0))
    params = _rw_param_list(wts) + [wts["lnx_g"], wts["lnx_b"]]
    return pl.pallas_call(
        _scan_kernel,
        grid=(bsz, t // step_rows),
        in_specs=[blk(RW_COLS), pl.BlockSpec((None, 1, RW_COLS), lambda b, c: (b, 0, 0))]
        + [_const_spec(a) for a in params],
        out_specs=[blk(w), pl.BlockSpec((None, n_pairs, LANES, LANES), lambda b, c: (b, 0, 0, 0))],
        out_shape=[jax.ShapeDtypeStruct((bsz, t, w), F32),
                   jax.ShapeDtypeStruct((bsz, n_pairs, LANES, LANES), F32)],
        scratch_shapes=[pltpu.VMEM((n_pairs, LANES, LANES), F32), pltpu.VMEM((1, RW_COLS), F32)],
        compiler_params=_cparams("parallel", "arbitrary"),
        name="rw_scan",
    )(p_rw, shift0, *params)


def _unpack_state(st):
    hd = RW_HEAD_DIM
    blocks = [st[:, p, j * hd:(j + 1) * hd, j * hd:(j + 1) * hd] for p in range(st.shape[1]) for j in range(2)]
    return jnp.swapaxes(jnp.stack(blocks, axis=1), -1, -2)


def _rwstep_kernel(s_ref, r_ref, k_ref, v_ref, a_ref, b_ref, lw_ref, bon_ref, g_ref, lng_ref, lnb_ref, so_ref, o_ref,
                   acc_scr, vt_scr):
    hd = RW_HEAD_DIM
    vt_scr[...] = v_ref[...].T
    decay2 = jnp.exp(lw_ref[...]).T
    kkn2 = a_ref[...].T
    kka2 = b_ref[...].T
    kmod2 = k_ref[...].T
    r2 = r_ref[...].T
    normed = []
    for h in range(2):
        ch = slice(hd * h, hd * (h + 1))
        decay, kkn, kka, kmod, r = decay2[ch], kkn2[ch], kka2[ch], kmod2[ch], r2[ch]

        def body(i, carry):
            s = s_ref[h, i]
            sa = jnp.sum(s * kkn, axis=0, keepdims=True)
            s_new = s * decay + sa * kka + vt_scr[pl.ds(hd * h + i, 1), :] * kmod
            so_ref[h, i] = s_new
            acc_scr[pl.ds(hd * h + i, 1), :] = jnp.sum(s_new * r, axis=0, keepdims=True)
            return carry

        lax.fori_loop(0, hd, body, 0, unroll=8)
        o = acc_scr[ch, :]
        mu = jnp.mean(o, axis=0, keepdims=True)
        d = o - mu
        var = jnp.mean(d * d, axis=0, keepdims=True)
        normed.append(d * lax.rsqrt(var + GN_EPS))
    y = jnp.concatenate(normed, axis=0).T * lng_ref[...] + lnb_ref[...]
    o_ref[...] = (y + bon_ref[...]) * g_ref[...]


def _rw_step(state_t, rw, wts):
    n = state_t.shape[-1]
    hd = RW_HEAD_DIM
    vec = pl.BlockSpec((n, 2 * hd), lambda p: (0, p))
    par = pl.BlockSpec((1, 2 * hd), lambda p: (0, p))
    st = pl.BlockSpec((2, hd, hd, n), lambda p: (p, 0, 0, 0))
    ins = [rw[nm] for nm in ("r", "kmod", "v", "kkn", "bvec", "logw", "bonus", "g")]
    return pl.pallas_call(
        _rwstep_kernel,
        grid=(RW_HEADS // 2,),
        in_specs=[st] + [vec] * 8 + [par, par],
        out_specs=[st, vec],
        out_shape=[jax.ShapeDtypeStruct(state_t.shape, F32), jax.ShapeDtypeStruct((n, RW_WIDTH), F32)],
        scratch_shapes=[pltpu.VMEM((2 * hd, n), F32), pltpu.VMEM((2 * hd, n), F32)],
        compiler_params=_cparams("parallel"),
        name="rw_step",
    )(state_t, *ins, wts["lnx_g"], wts["lnx_b"])


def _rms(x, g):
    return x * lax.rsqrt(jnp.mean(x * x, axis=-1, keepdims=True) + RMS_EPS) * g


def _mla_common(low, pos, gq, gkv, wqa, wqb, invf, q_scale):
    cq = _rms(low[:, :Q_LORA], gq)
    ckv = _rms(low[:, Q_LORA:Q_LORA + KV_LORA], gkv)
    kr = low[:, Q_LORA + KV_LORA:Q_LORA + KV_LORA + LANES]
    kr_rot = low[:, Q_LORA + KV_LORA + LANES:]
    ang = pos * invf
    cos = jnp.cos(ang)
    sin = jnp.sin(ang)
    cos8 = jnp.concatenate([cos] * MLA_HEADS, axis=1)
    sin8 = jnp.concatenate([sin] * MLA_HEADS, axis=1)
    q = (_mm(cq, wqa) * cos8 + _mm(cq, wqb) * sin8) * q_scale
    k_rope = kr * pltpu.roll(cos, LANES - QK_NOPE, 1) + kr_rot * pltpu.roll(sin, LANES - QK_NOPE, 1)
    return q, ckv, k_rope


def _inproj_prompt_kernel(x_ref, sh_ref, sc_ref, wrw_ref, wmla_ref, wg_ref, gq_ref, gkv_ref, wqa_ref, wqb_ref, wuk_ref,
                          wuv_ref, invf_ref, prw_ref, gs_ref, q_ref, k_ref, v_ref, ckv_ref, kr_ref):
    i = pl.program_id(1)
    rows = x_ref.shape[0]
    h = (x_ref[...] * (1 + sc_ref[...]) + sh_ref[...]).astype(BF16)
    prw_ref[...] = jnp.dot(h, wrw_ref[...], preferred_element_type=F32)
    gs_ref[...] = jax.nn.sigmoid(jnp.dot(h, wg_ref[...], preferred_element_type=F32))
    low = jnp.dot(h, wmla_ref[...], preferred_element_type=F32)
    pos = (i * rows + _iota((rows, 1), 0)).astype(F32)
    q, ckv, k_rope = _mla_common(low, pos, gq_ref[...], gkv_ref[...], wqa_ref[...], wqb_ref[...], invf_ref[...],
                                 ATTN_SCALE * math.log2(math.e))
    q_ref[...] = q.astype(BF16)
    ckv_ref[...] = ckv
    kr_ref[...] = k_rope[:, :QK_ROPE]
    k_slot = pltpu.roll(k_rope, QK_NOPE, 1)
    k_ref[...] = (_mm(ckv, wuk_ref[...]) + jnp.concatenate([k_slot] * MLA_HEADS, axis=1)).astype(BF16)
    v_ref[...] = _mm_t(wuv_ref[...], ckv).astype(BF16)


def _inproj_prompt(x, mod, wts, tile):
    bsz, t, d = x.shape
    params = [wts["w_rw"], wts["w_mla"], wts["w_gate"], wts["g_qnorm"], wts["g_kvnorm"], wts["w_qa"], wts["w_qb"],
              wts["w_uk_pad"], wts["w_uv_t"], wts["invf"]]
    row = lambda w: pl.BlockSpec((None, tile, w), lambda b, i: (b, i, 0))
    slots = MLA_HEADS * HEAD_SLOT
    return pl.pallas_call(
        _inproj_prompt_kernel,
        grid=(bsz, t // tile),
        in_specs=[row(d), mod.spec(0), mod.spec(1)] + [_const_spec(a) for a in params],
        out_specs=[row(RW_COLS), row(2 * d), row(slots), row(slots),
                   pl.BlockSpec((None, MLA_HEADS * V_HEAD, tile), lambda b, i: (b, 0, i)), row(KV_LORA), row(QK_ROPE)],
        out_shape=[jax.ShapeDtypeStruct((bsz, t, RW_COLS), F32), jax.ShapeDtypeStruct((bsz, t, 2 * d), F32),
                   jax.ShapeDtypeStruct((bsz, t, slots), BF16), jax.ShapeDtypeStruct((bsz, t, slots), BF16),
                   jax.ShapeDtypeStruct((bsz, MLA_HEADS * V_HEAD, t), BF16),
                   jax.ShapeDtypeStruct((bsz, t, KV_LORA), F32), jax.ShapeDtypeStruct((bsz, t, QK_ROPE), F32)],
        compiler_params=_cparams("parallel", "arbitrary"),
        name="inproj_prompt",
    )(x, mod.arr, mod.arr, *params)


def _mlaprep_sample_kernel(low_ref, gq_ref, gkv_ref, wqa_ref, wqb_ref, wukt_ref, invf_ref, pos_ref,
                           qlat_ref, qr_ref, ckv_ref, kr_ref):
    q, ckv, k_rope = _mla_common(low_ref[...], pos_ref[...], gq_ref[...], gkv_ref[...], wqa_ref[...], wqb_ref[...],
                                 invf_ref[...], ATTN_SCALE)
    ckv_ref[...] = ckv
    kr_ref[...] = k_rope[:, :QK_ROPE]
    rope_lanes = _iota((1, HEAD_SLOT), 1) < QK_ROPE
    for h in range(MLA_HEADS):
        slot = q[:, HEAD_SLOT * h:HEAD_SLOT * (h + 1)]
        qlat_ref[:, KV_LORA * h:KV_LORA * (h + 1)] = _mm(slot, wukt_ref[h])
        qr_ref[:, HEAD_SLOT * h:HEAD_SLOT * (h + 1)] = jnp.where(rope_lanes, pltpu.roll(slot, LANES - QK_NOPE, 1), 0.0)


def _mlaprep_sample(low, pos, wts):
    n = low.shape[0]
    params = [wts["g_qnorm"], wts["g_kvnorm"], wts["w_qa"], wts["w_qb"], wts["w_ukt_pad"], wts["invf"], pos]
    full = lambda shape: pl.BlockSpec(shape, lambda i: (0,) * len(shape))
    return pl.pallas_call(
        _mlaprep_sample_kernel,
        grid=(1,),
        in_specs=[_const_spec(low)] + [_const_spec(a) for a in params],
        out_specs=[full((n, MLA_HEADS * KV_LORA)), full((n, MLA_HEADS * HEAD_SLOT)), full((n, KV_LORA)),
                   full((n, QK_ROPE))],
        out_shape=[jax.ShapeDtypeStruct((n, MLA_HEADS * KV_LORA), F32),
                   jax.ShapeDtypeStruct((n, MLA_HEADS * HEAD_SLOT), F32),
                   jax.ShapeDtypeStruct((n, KV_LORA), F32), jax.ShapeDtypeStruct((n, QK_ROPE), F32)],
        compiler_params=_cparams("arbitrary"),
        name="mlaprep_sample",
    )(low, *params)


def _attn_kernel(qi_ref, ki_ref, q_ref, k_ref, vt_ref, o_ref, m_scr, l_scr, acc_scr):
    step = pl.program_id(2)
    qi = qi_ref[step]
    ki = ki_ref[step]
    bq = q_ref.shape[0]
    bk = k_ref.shape[0]
    on_diagonal = ki == (qi * bq) // bk

    @pl.when(ki == 0)
    def _():
        m_scr[...] = jnp.full(m_scr.shape, -jnp.inf, F32)
        l_scr[...] = jnp.zeros_like(l_scr)
        acc_scr[...] = jnp.zeros_like(acc_scr)

    n_heads = m_scr.shape[0]

    def accumulate(masked):
        scores = [_mm_t(k_ref[:, HEAD_SLOT * h:HEAD_SLOT * (h + 1)], q_ref[:, HEAD_SLOT * h:HEAD_SLOT * (h + 1)])
                  for h in range(n_heads)]
        if masked:
            visible = ki * bk + _iota((bk, bq), 0) <= qi * bq + _iota((bk, bq), 1)
            scores = [jnp.where(visible, s, -jnp.inf) for s in scores]
        for h, s in enumerate(scores):
            m_prev = m_scr[h]
            m_new = jnp.maximum(m_prev, jnp.max(s, axis=0, keepdims=True))
            p = jnp.exp2(s - m_new)
            alpha = jnp.exp2(m_prev - m_new)
            l_scr[h] = alpha * l_scr[h] + jnp.sum(p, axis=0, keepdims=True)
            acc_scr[h] = alpha * acc_scr[h] + _mm(vt_ref[V_HEAD * h:V_HEAD * (h + 1), :], p)
            m_scr[h] = m_new

    @pl.when(jnp.logical_not(on_diagonal))
    def _():
        accumulate(False)

    @pl.when(on_diagonal)
    def _():
        accumulate(True)
        out_t = jnp.concatenate([acc_scr[h] / l_scr[h] for h in range(n_heads)], axis=0)
        o_ref[...] = out_t.T


def _attention_prompt(q, k, v_t, bq, bk):
    bsz, t, _ = q.shape
    assert bk % bq == 0
    hps = ATTN_HEADS_PER_STEP
    pairs = [(qi, ki) for qi in range(t // bq) for ki in range((qi * bq) // bk + 1)]
    qi_of = jnp.asarray([p[0] for p in pairs], jnp.int32)
    ki_of = jnp.asarray([p[1] for p in pairs], jnp.int32)
    grid_spec = pltpu.PrefetchScalarGridSpec(
        num_scalar_prefetch=2,
        grid=(bsz, MLA_HEADS // hps, len(pairs)),
        in_specs=[pl.BlockSpec((None, bq, hps * HEAD_SLOT), lambda b, g, s, qi, ki: (b, qi[s], g)),
                  pl.BlockSpec((None, bk, hps * HEAD_SLOT), lambda b, g, s, qi, ki: (b, ki[s], g)),
                  pl.BlockSpec((None, hps * V_HEAD, bk), lambda b, g, s, qi, ki: (b, g, ki[s]))],
        out_specs=pl.BlockSpec((None, bq, hps * V_HEAD), lambda b, g, s, qi, ki: (b, qi[s], g)),
        scratch_shapes=[pltpu.VMEM((hps, 1, bq), F32), pltpu.VMEM((hps, 1, bq), F32),
                        pltpu.VMEM((hps, V_HEAD, bq), F32)],
    )
    return pl.pallas_call(
        _attn_kernel,
        grid_spec=grid_spec,
        out_shape=jax.ShapeDtypeStruct((bsz, t, MLA_HEADS * V_HEAD), F32),
        compiler_params=_cparams("parallel", "parallel", "arbitrary"),
        name="attn_prompt",
    )(qi_of, ki_of, q, k, v_t)


def _sample_attn_kernel(pt_ref, qlat_ref, qr_ref, cn_ref, kn_ref, ckv_hbm, kr_hbm, o_ref, kbuf, rbuf, s_scr, sem):
    b = pl.program_id(0)
    n_seq = pl.num_programs(0)
    n_pages = kbuf.shape[1]
    slot = lax.rem(b, 2)

    def page_copies(seq, slt):
        out = []
        for pg in range(n_pages):
            page = pt_ref[seq, pg]
            out.append(pltpu.make_async_copy(ckv_hbm.at[page], kbuf.at[slt, pg], sem.at[slt, 0]))
            out.append(pltpu.make_async_copy(kr_hbm.at[page], rbuf.at[slt, pg], sem.at[slt, 1]))
        return out

    @pl.when(b == 0)
    def _():
        for cp in page_copies(0, 0):
            cp.start()

    @pl.when(b + 1 < n_seq)
    def _():
        for cp in page_copies(b + 1, 1 - slot):
            cp.start()

    for cp in page_copies(b, slot):
        cp.wait()

    ql = qlat_ref[...]
    qr = qr_ref[:, :QK_ROPE]
    n_chunks = n_pages // 2
    rows = 2 * PAGE_SIZE

    def score(i, carry):
        kc = kbuf[slot, pl.ds(2 * i, 2)].reshape(rows, KV_LORA)
        rc = jnp.concatenate([rbuf[slot, 2 * i], rbuf[slot, 2 * i + 1]], axis=1)
        s_scr[i] = _mm_t(ql, kc) + jnp.dot(qr, rc, preferred_element_type=F32)
        return carry

    lax.fori_loop(0, n_chunks, score, 0, unroll=SAMPLE_ATTN_UNROLL)
    cn = cn_ref[...]
    kn = kn_ref[...]
    s_new = jnp.sum(ql * cn, axis=-1, keepdims=True) + jnp.sum(qr * kn, axis=-1, keepdims=True)
    s_all = s_scr[...]
    m = jnp.maximum(jnp.max(jnp.max(s_all, axis=0), axis=-1, keepdims=True), s_new)
    p_all = jnp.exp(s_all - m)
    p_new = jnp.exp(s_new - m)
    denom = jnp.sum(jnp.sum(p_all, axis=0), axis=-1, keepdims=True) + p_new
    s_scr[...] = p_all

    def accumulate(i, acc):
        kc = kbuf[slot, pl.ds(2 * i, 2)].reshape(rows, KV_LORA)
        return acc + _mm(s_scr[i], kc)

    acc = lax.fori_loop(0, n_chunks, accumulate, jnp.zeros((ql.shape[0], KV_LORA), F32), unroll=SAMPLE_ATTN_UNROLL)
    o_ref[...] = (acc + p_new * cn) / denom


def _attention_sample(page_table, qlat, qr, ckv_new, kr_new, cache_ckv, cache_kr):
    n, n_pages = page_table.shape
    grid_spec = pltpu.PrefetchScalarGridSpec(
        num_scalar_prefetch=1,
        grid=(n,),
        in_specs=[pl.BlockSpec((None, MLA_HEADS, KV_LORA), lambda b, pt: (b, 0, 0)),
                  pl.BlockSpec((None, MLA_HEADS, HEAD_SLOT), lambda b, pt: (b, 0, 0)),
                  pl.BlockSpec((None, 1, KV_LORA), lambda b, pt: (b, 0, 0)),
                  pl.BlockSpec((None, 1, QK_ROPE), lambda b, pt: (b, 0, 0)),
                  pl.BlockSpec(memory_space=pl.ANY),
                  pl.BlockSpec(memory_space=pl.ANY)],
        out_specs=pl.BlockSpec((None, MLA_HEADS, KV_LORA), lambda b, pt: (b, 0, 0)),
        scratch_shapes=[pltpu.VMEM((2, n_pages, PAGE_SIZE, KV_LORA), F32),
                        pltpu.VMEM((2, n_pages, QK_ROPE, PAGE_SIZE), F32),
                        pltpu.VMEM((n_pages // 2, MLA_HEADS, 2 * PAGE_SIZE), F32),
                        pltpu.SemaphoreType.DMA((2, 2))],
    )
    return pl.pallas_call(
        _sample_attn_kernel,
        grid_spec=grid_spec,
        out_shape=jax.ShapeDtypeStruct((n, MLA_HEADS, KV_LORA), F32),
        compiler_params=_cparams("arbitrary"),
        name="attn_sample",
    )(page_table, qlat.reshape(n, MLA_HEADS, KV_LORA), qr.reshape(n, MLA_HEADS, HEAD_SLOT),
      ckv_new.reshape(n, 1, KV_LORA), kr_new.reshape(n, 1, QK_ROPE), cache_ckv, cache_kr)


def _uv_kernel(ol_ref, wuv_ref, o_ref):
    outs = [_mm(ol_ref[:, KV_LORA * h:KV_LORA * (h + 1)], wuv_ref[:, V_HEAD * h:V_HEAD * (h + 1)])
            for h in range(MLA_HEADS)]
    o_ref[...] = jnp.concatenate(outs, axis=1)


def _latent_to_heads(o_lat, w_uv):
    n = o_lat.shape[0]
    o_lat = o_lat.reshape(n, MLA_HEADS * KV_LORA)
    return pl.pallas_call(
        _uv_kernel,
        grid=(1,),
        in_specs=[_const_spec(o_lat), _const_spec(w_uv)],
        out_specs=pl.BlockSpec((n, MLA_HEADS * V_HEAD), lambda i: (0, 0)),
        out_shape=jax.ShapeDtypeStruct((n, MLA_HEADS * V_HEAD), F32),
        compiler_params=_cparams("arbitrary"),
        name="latent_to_heads",
    )(o_lat, w_uv)


def _layernorm(z, g, b):
    mu = jnp.mean(z, axis=-1, keepdims=True)
    d = z - mu
    var = jnp.mean(d * d, axis=-1, keepdims=True)
    return d * lax.rsqrt(var + LN_EPS) * g + b


def _first_max(x, idx, sentinel):
    mx = jnp.max(x, axis=0, keepdims=True)
    return jnp.min(jnp.where(x == mx, idx, sentinel), axis=0, keepdims=True)


def _route(scores, bias):
    n_tok = scores.shape[1]
    per_group = N_EXPERTS // N_GROUPS
    sb = scores + bias
    sb3 = sb.reshape(N_GROUPS, per_group, n_tok)
    member = _iota(sb3.shape, 1)
    m1 = jnp.max(sb3, axis=1, keepdims=True)
    first = jnp.min(jnp.where(sb3 == m1, member, per_group), axis=1, keepdims=True)
    m2 = jnp.max(jnp.where(member == first, -jnp.inf, sb3), axis=1, keepdims=True)
    g_score = (m1 + m2).reshape(N_GROUPS, n_tok)
    g_idx = _iota(g_score.shape, 0)
    g_sel = jnp.zeros(g_score.shape, F32)
    for _ in range(TOPK_GROUPS):
        hit = g_idx == _first_max(g_score, g_idx, N_GROUPS)
        g_sel = jnp.where(hit, 1.0, g_sel)
        g_score = jnp.where(hit, -jnp.inf, g_score)
    e_mask = jnp.broadcast_to(g_sel.reshape(N_GROUPS, 1, n_tok), sb3.shape).reshape(N_EXPERTS, n_tok) > 0.5
    cur = jnp.where(e_mask, sb, -jnp.inf)
    e_idx = _iota(cur.shape, 0)
    sel = jnp.zeros(cur.shape, F32)
    for _ in range(TOP_K):
        hit = e_idx == _first_max(cur, e_idx, N_EXPERTS)
        sel = jnp.where(hit, 1.0, sel)
        cur = jnp.where(hit, -jnp.inf, cur)
    picked = jnp.where(sel > 0.5, scores, 0.0)
    return picked / jnp.sum(picked, axis=0, keepdims=True) * ROUTED_SCALE


def _merge_kernel(x_ref, oa_ref, ob_ref, gs_ref, gt1_ref, sh2_ref, sc2_ref, wba_ref, wbb_ref, wout_ref, g1_ref, b1_ref,
                  wrt_ref, br_ref, x1_ref, h2_ref, cw_ref):
    d = D_MODEL
    ya = _mm(oa_ref[...], wba_ref[...])
    yb = _mm(ob_ref[...], wbb_ref[...])
    merged = gs_ref[:, :d] * ya + gs_ref[:, d:] * yb
    z = DN_ALPHA * x_ref[...] + gt1_ref[...] * _mm(merged, wout_ref[...])
    x1 = _layernorm(z, g1_ref[...], b1_ref[...])
    x1_ref[...] = x1
    h2 = x1 * (1 + sc2_ref[...]) + sh2_ref[...]
    h2_ref[...] = h2.astype(BF16)
    scores = jax.nn.sigmoid(_mm3_t(wrt_ref[...], h2))
    cw_ref[...] = _route(scores, br_ref[...]).T


def _merge(x3, oa, ob, gs, mod, wts, tile):
    bsz, t, d = x3.shape
    params = [wts["w_branch_a"], wts["w_branch_b"], wts["w_out"], wts["ln1_g"], wts["ln1_b"], wts["w_router_t"],
              wts["b_router"]]
    row = lambda w: pl.BlockSpec((None, tile, w), lambda b, i: (b, i, 0))
    return pl.pallas_call(
        _merge_kernel,
        grid=(bsz, t // tile),
        in_specs=[row(d), row(RW_WIDTH), row(MLA_HEADS * V_HEAD), row(2 * d), mod.spec(2), mod.spec(3), mod.spec(4)]
        + [_const_spec(a) for a in params],
        out_specs=[row(d), row(d), row(N_EXPERTS)],
        out_shape=[jax.ShapeDtypeStruct((bsz, t, d), F32), jax.ShapeDtypeStruct((bsz, t, d), BF16),
                   jax.ShapeDtypeStruct((bsz, t, N_EXPERTS), F32)],
        compiler_params=_cparams("parallel", "arbitrary"),
        name="merge",
    )(x3, oa, ob, gs, mod.arr, mod.arr, mod.arr, *params)


def _experts_kernel(h_ref, cw_ref, wg_ref, wu_ref, wd_ref, o_ref):
    step = pl.program_id(1)
    n_e = wg_ref.shape[0]

    @pl.when(step == 0)
    def _():
        o_ref[...] = jnp.zeros_like(o_ref)

    x = h_ref[...]
    cw = cw_ref[...]
    lane = _iota((1, N_EXPERTS), 1)
    acts = []
    for e in range(n_e):
        act = _silu(_mm(x, wg_ref[e])) * _mm(x, wu_ref[e])
        w_col = jnp.sum(jnp.where(lane == step * n_e + e, cw, 0.0), axis=1, keepdims=True)
        acts.append((act * w_col).astype(BF16))
    o_ref[...] += _mm(jnp.concatenate(acts, axis=1), wd_ref[...].reshape(n_e * EXPERT_FF, -1))


def _experts(h2, cw, wts, tile):
    n, d = h2.shape
    n_e = EXPERTS_PER_STEP
    return pl.pallas_call(
        _experts_kernel,
        grid=(n // tile, N_EXPERTS // n_e),
        in_specs=[pl.BlockSpec((tile, d), lambda i, e: (i, 0)),
                  pl.BlockSpec((tile, N_EXPERTS), lambda i, e: (i, 0)),
                  pl.BlockSpec((n_e, d, EXPERT_FF), lambda i, e: (e, 0, 0)),
                  pl.BlockSpec((n_e, d, EXPERT_FF), lambda i, e: (e, 0, 0)),
                  pl.BlockSpec((n_e, EXPERT_FF, d), lambda i, e: (e, 0, 0))],
        out_specs=pl.BlockSpec((tile, d), lambda i, e: (i, 0)),
        out_shape=jax.ShapeDtypeStruct((n, d), F32),
        compiler_params=_cparams("parallel", "arbitrary"),
        name="experts",
    )(h2, cw, wts["w_exp_gate"], wts["w_exp_up"], wts["w_exp_down"])


def _final_kernel(x1_ref, h2_ref, routed_ref, gt2_ref, wsg_ref, wsu_ref, wsd_ref, g2_ref, b2_ref, y_ref):
    h2 = h2_ref[...]
    act = _silu(jnp.dot(h2, wsg_ref[...], preferred_element_type=F32)) * jnp.dot(h2, wsu_ref[...],
                                                                               preferred_element_type=F32)
    ffn = routed_ref[...] + _mm(act, wsd_ref[...])
    y_ref[...] = _layernorm(DN_ALPHA * x1_ref[...] + gt2_ref[...] * ffn, g2_ref[...], b2_ref[...])


def _final(x1, h2, routed, mod, wts, tile):
    bsz, t, d = x1.shape
    params = [wts["w_sh_gate"], wts["w_sh_up"], wts["w_sh_down"], wts["ln2_g"], wts["ln2_b"]]
    row = lambda w: pl.BlockSpec((None, tile, w), lambda b, i: (b, i, 0))
    return pl.pallas_call(
        _final_kernel,
        grid=(bsz, t // tile),
        in_specs=[row(d), row(d), row(d), mod.spec(5)] + [_const_spec(a) for a in params],
        out_specs=row(d),
        out_shape=jax.ShapeDtypeStruct((bsz, t, d), F32),
        compiler_params=_cparams("parallel", "arbitrary"),
        name="final",
    )(x1, h2, routed, mod.arr, *params)


def _rope_rotation_columns(w):
    half = QK_ROPE // 2
    return jnp.concatenate([-w[..., half:], w[..., :half]], axis=-1)


def _prepare_weights(P):
    d = D_MODEL
    w = {}
    w_in = P["w_in"]
    o1 = RW_COLS
    o2 = o1 + Q_LORA
    o3 = o2 + KV_LORA
    o4 = o3 + QK_ROPE
    w_kr = w_in[:, o3:o4]
    pad = jnp.zeros((d, LANES - QK_ROPE), F32)
    w["w_rw"] = w_in[:, :o1].astype(BF16)
    w["w_mla"] = jnp.concatenate([w_in[:, o1:o3], w_kr, pad, _rope_rotation_columns(w_kr), pad], axis=1).astype(BF16)
    w["w_gate"] = w_in[:, o4:].astype(BF16)

    row = lambda v: v.reshape(1, -1)
    w["mu_shift"] = row(P["mu_shift"])
    zl = jnp.zeros((DECAY_LORA, RW_WIDTH), F32)
    w["w_lora"] = jnp.concatenate([jnp.concatenate([P["w_decay_up"], zl], axis=1),
                                   jnp.concatenate([zl, P["w_iclr_up"]], axis=1)], axis=0).astype(BF16)
    w["w_gate_up"] = P["w_gate_up"].astype(BF16)
    for nm in ("w_decay0", "a0", "k_k", "k_a", "r_k", "lnx_g", "lnx_b", "g_qnorm", "g_kvnorm", "ln1_g", "ln1_b",
               "ln2_g", "ln2_b"):
        w[nm] = row(P[nm])
    head_of = np.arange(RW_WIDTH) // RW_HEAD_DIM
    w["ones_blk"] = jnp.asarray(head_of[:, None] == head_of[None, :], BF16)

    wq = P["w_uq"].reshape(Q_LORA, MLA_HEADS, QK_NOPE + QK_ROPE)
    zq = jnp.zeros((Q_LORA, MLA_HEADS, HEAD_SLOT - QK_NOPE - QK_ROPE), F32)
    w["w_qa"] = jnp.concatenate([wq, zq], axis=-1).reshape(Q_LORA, MLA_HEADS * HEAD_SLOT).astype(BF16)
    w["w_qb"] = jnp.concatenate([jnp.zeros((Q_LORA, MLA_HEADS, QK_NOPE), F32), _rope_rotation_columns(wq[..., QK_NOPE:]), zq],
                                axis=-1).reshape(Q_LORA, MLA_HEADS * HEAD_SLOT).astype(BF16)
    wuk = P["w_uk"].reshape(KV_LORA, MLA_HEADS, QK_NOPE)
    zk = jnp.zeros((KV_LORA, MLA_HEADS, HEAD_SLOT - QK_NOPE), F32)
    w["w_uk_pad"] = jnp.concatenate([wuk, zk], axis=-1).reshape(KV_LORA, MLA_HEADS * HEAD_SLOT).astype(BF16)
    w["w_ukt_pad"] = jnp.transpose(jnp.concatenate([wuk, zk], axis=-1), (1, 2, 0)).astype(BF16)
    w["w_uv"] = P["w_uv"].astype(BF16)
    w["w_uv_t"] = P["w_uv"].T.astype(BF16)
    half = QK_ROPE // 2
    inv = ROPE_THETA ** (-jnp.arange(half, dtype=F32) / half)
    w["invf"] = jnp.concatenate([jnp.zeros((QK_NOPE,), F32), inv, inv,
                                 jnp.zeros((HEAD_SLOT - QK_NOPE - QK_ROPE,), F32)]).reshape(1, HEAD_SLOT)

    for nm in ("w_branch_a", "w_branch_b", "w_out", "w_sh_gate", "w_sh_up", "w_sh_down"):
        w[nm] = P[nm].astype(BF16)
    for nm in ("w_exp_gate", "w_exp_up", "w_exp_down"):
        w[nm] = P[nm]
    w["w_router_t"] = P["w_router"].T
    w["b_router"] = P["b_router"].reshape(N_EXPERTS, 1)
    return w


def _row_tile(t, cap):
    tile = min(t, cap)
    assert t % tile == 0 and tile % 16 == 0, (t, tile)
    return tile


def _layer_tail(x3, oa, ob, gs, mod, wts, tile, moe_tile):
    bsz, t, d = x3.shape
    x1, h2, cw = _merge(x3, oa, ob, gs, mod, wts, tile)
    routed = _experts(h2.reshape(bsz * t, d), cw.reshape(bsz * t, N_EXPERTS), wts, moe_tile).reshape(bsz, t, d)
    return _final(x1, h2, routed, mod, wts, tile)


def _layer_prompt(x, mod_rows, wts):
    bsz, t, d = x.shape
    tile = _row_tile(t, 256)
    mod = _Mod(mod_rows.reshape(bsz * 6, 1, d), per_token=False, tile=tile)
    p_rw, gs, q, k, v_t, ckv, k_rope = _inproj_prompt(x, mod, wts, tile)
    oa, state = _rw_scan(p_rw, jnp.zeros((bsz, 1, RW_COLS), F32), wts)
    ob = _attention_prompt(q, k, v_t, _row_tile(t, 512), _row_tile(t, 1024))
    y = _layer_tail(x, oa, ob, gs, mod, wts, _row_tile(t, 512), _row_tile(bsz * t, 1024))
    return y, ckv, k_rope, _unpack_state(state), p_rw[:, -1]


def _layer_sample(x, mod_rows, state, shift, page_table, cache_ckv, cache_kr, wts):
    n, s_new, d = x.shape
    assert s_new == 1
    past = page_table.shape[1] * PAGE_SIZE
    x3 = x.reshape(1, n, d)
    mod = _Mod(mod_rows, per_token=True, tile=n)
    p_rw, low, gs = _inproj(x3, mod, wts, n)
    p_rw2 = p_rw.reshape(n, RW_COLS)
    rw = dict(zip(RW_OUT_NAMES, _rwprep_sample(p_rw2, shift, wts)))
    new_state_t, oa = _rw_step(jnp.transpose(state, (1, 2, 3, 0)), rw, wts)
    new_state = jnp.transpose(new_state_t, (3, 0, 1, 2))
    pos = jnp.full((n, 1), past, F32)
    qlat, qr, ckv, k_rope = _mlaprep_sample(low.reshape(n, MLA_LOW_COLS), pos, wts)
    o_lat = _attention_sample(page_table, qlat, qr, ckv, k_rope, cache_ckv, jnp.swapaxes(cache_kr, 1, 2))
    ob = _latent_to_heads(o_lat, wts["w_uv"])
    y = _layer_tail(x3, oa.reshape(1, n, RW_WIDTH), ob.reshape(1, n, MLA_HEADS * V_HEAD), gs, mod, wts, n, n)
    return (y.reshape(n, 1, d), ckv.reshape(n, 1, KV_LORA), k_rope.reshape(n, 1, QK_ROPE), new_state, p_rw2)


def kernel(x_prompt, x_sample, c_prompt, c_sample, cache_ckv, cache_krope, state_wkv, state_shift, page_table, w_ada, b_ada, w_in, mu_shift, w_decay0, w_decay_up, a0, w_iclr_up, w_gate_up, k_k, k_a, r_k, lnx_g, lnx_b, w_branch_a, g_qnorm, w_uq, g_kvnorm, w_uk, w_uv, w_branch_b, w_out, ln1_g, ln1_b, w_router, b_router, w_exp_gate, w_exp_up, w_exp_down, w_sh_gate, w_sh_up, w_sh_down, ln2_g, ln2_b):
    params = dict(w_ada=w_ada, b_ada=b_ada, w_in=w_in, mu_shift=mu_shift, w_decay0=w_decay0, w_decay_up=w_decay_up,
                  a0=a0, w_iclr_up=w_iclr_up, w_gate_up=w_gate_up, k_k=k_k, k_a=k_a, r_k=r_k, lnx_g=lnx_g,
                  lnx_b=lnx_b, w_branch_a=w_branch_a, g_qnorm=g_qnorm, w_uq=w_uq, g_kvnorm=g_kvnorm, w_uk=w_uk,
                  w_uv=w_uv, w_branch_b=w_branch_b, w_out=w_out, ln1_g=ln1_g, ln1_b=ln1_b, w_router=w_router,
                  b_router=b_router, w_exp_gate=w_exp_gate, w_exp_up=w_exp_up, w_exp_down=w_exp_down,
                  w_sh_gate=w_sh_gate, w_sh_up=w_sh_up, w_sh_down=w_sh_down, ln2_g=ln2_g, ln2_b=ln2_b)
    depth = w_in.shape[0]
    bp = x_prompt.shape[0]
    bd = x_sample.shape[0]
    n_c = bp + bd
    c_all = jnp.concatenate([c_prompt, c_sample, jnp.zeros((-n_c % 8, D_MODEL), F32)], axis=0)
    yp, ys = x_prompt, x_sample
    outs = [[] for _ in range(8)]
    for l in range(depth):
        wts = _prepare_weights({name: arr[l] for name, arr in params.items()})
        mod = _adaln_mod(c_all, params["w_ada"][l], params["b_ada"][l].reshape(1, -1))
        yp, *rest_p = _layer_prompt(yp, mod[:bp], wts)
        ys, *rest_s = _layer_sample(ys, mod[bp:n_c], state_wkv[l], state_shift[l], page_table, cache_ckv[l],
                                    cache_krope[l], wts)
        for acc, val in zip(outs, rest_p + rest_s):
            acc.append(val)
    return (yp, ys) + tuple(jnp.stack(o) for o in outs)
```

```python
import math

import numpy as np
import jax
import jax.numpy as jnp
from jax import lax
from jax.experimental import pallas as pl
from jax.experimental.pallas import tpu as pltpu

F32 = jnp.float32
BF16 = jnp.bfloat16

D_MODEL = 1024
PAGE_SIZE = 128
RW_HEADS = 8
RW_HEAD_DIM = 64
RW_WIDTH = RW_HEADS * RW_HEAD_DIM
DECAY_LORA = 64
ICLR_LORA = 64
GATE_LORA = 128
RW_COLS = 3 * RW_WIDTH + DECAY_LORA + ICLR_LORA + GATE_LORA
GN_EPS = 64e-5
MLA_HEADS = 8
QK_NOPE = 64
QK_ROPE = 32
V_HEAD = 64
Q_LORA = 384
KV_LORA = 256
ROPE_THETA = 10000.0
ATTN_SCALE = (QK_NOPE + QK_ROPE) ** -0.5
N_EXPERTS = 64
TOP_K = 8
N_GROUPS = 8
TOPK_GROUPS = 4
EXPERT_FF = 256
SHARED_FF = 256
ROUTED_SCALE = 2.5
DEPTH = 1
DN_ALPHA = (2 * DEPTH) ** 0.25
LN_EPS = 1e-5
RMS_EPS = 1e-6

LANES = 128
HEAD_SLOT = LANES
MLA_LOW_COLS = Q_LORA + KV_LORA + 2 * LANES
SCAN_CHUNK = 64
SCAN_SUBCHUNKS = 8
SAMPLE_ATTN_UNROLL = 32
ATTN_HEADS_PER_STEP = 4
EXPERTS_PER_STEP = 4
VMEM_LIMIT = 56 * 1024 * 1024


def _cparams(*sem):
    return pltpu.CompilerParams(dimension_semantics=sem, vmem_limit_bytes=VMEM_LIMIT)


def _mm(a, b):
    return jnp.dot(a.astype(BF16), b.astype(BF16), preferred_element_type=F32)


def _mm_t(a, b):
    return lax.dot_general(a.astype(BF16), b.astype(BF16), (((1,), (1,)), ((), ())), preferred_element_type=F32)


def _mm_tl(a, b):
    return lax.dot_general(a.astype(BF16), b.astype(BF16), (((0,), (0,)), ((), ())), preferred_element_type=F32)


def _split(x):
    hi = x.astype(BF16)
    lo = (x - hi.astype(F32)).astype(BF16)
    return hi, lo


def _three_pass(f, a, b):
    ah, al = _split(a)
    bh, bl = _split(b)
    return f(ah, bh) + (f(ah, bl) + f(al, bh))


def _mm3_t(a, b):
    return _three_pass(_mm_t, a, b)


def _mm_exact_rhs(a, b_exact):
    ah, al = _split(a)
    return _mm(ah, b_exact) + _mm(al, b_exact)


def _silu(x):
    return x * jax.nn.sigmoid(x)


def _iota(shape, dim):
    return lax.broadcasted_iota(jnp.int32, shape, dim)


def _mod_kernel(c_ref, w_ref, b_ref, o_ref):
    o_ref[...] = _mm(_silu(c_ref[...]), w_ref[...]) + b_ref[...]


def _adaln_mod(c_all, w_ada, b_ada):
    n = c_all.shape[0]
    d = D_MODEL
    return pl.pallas_call(
        _mod_kernel,
        grid=(6,),
        in_specs=[pl.BlockSpec((n, d), lambda j: (0, 0)),
                  pl.BlockSpec((d, d), lambda j: (0, j)),
                  pl.BlockSpec((1, d), lambda j: (0, j))],
        out_specs=pl.BlockSpec((n, d), lambda j: (0, j)),
        out_shape=jax.ShapeDtypeStruct((n, 6 * d), F32),
        compiler_params=_cparams("arbitrary"),
        name="adaln_mod",
    )(c_all, w_ada, b_ada)


class _Mod:
    def __init__(self, arr, per_token, tile):
        self.arr = arr
        self.per_token = per_token
        self.tile = tile

    def spec(self, j):
        if self.per_token:
            return pl.BlockSpec((self.tile, D_MODEL), lambda b, i, *_: (i, j))
        return pl.BlockSpec((None, 1, D_MODEL), lambda b, i, *_: (b * 6 + j, 0, 0))


def _const_spec(arr):
    nd = arr.ndim
    return pl.BlockSpec(arr.shape, lambda *_: (0,) * nd, pipeline_mode=pl.Buffered(1))


def _inproj_kernel(x_ref, sh_ref, sc_ref, wrw_ref, wmla_ref, wg_ref, prw_ref, mla_ref, gs_ref):
    h = (x_ref[...] * (1 + sc_ref[...]) + sh_ref[...]).astype(BF16)
    prw_ref[...] = jnp.dot(h, wrw_ref[...], preferred_element_type=F32)
    mla_ref[...] = jnp.dot(h, wmla_ref[...], preferred_element_type=F32)
    gs_ref[...] = jax.nn.sigmoid(jnp.dot(h, wg_ref[...], preferred_element_type=F32))


def _inproj(x3, mod, wts, tile):
    bsz, t, d = x3.shape
    row = lambda w: pl.BlockSpec((None, tile, w), lambda b, i: (b, i, 0))
    return pl.pallas_call(
        _inproj_kernel,
        grid=(bsz, t // tile),
        in_specs=[row(d), mod.spec(0), mod.spec(1),
                  _const_spec(wts["w_rw"]), _const_spec(wts["w_mla"]), _const_spec(wts["w_gate"])],
        out_specs=[row(RW_COLS), row(MLA_LOW_COLS), row(2 * d)],
        out_shape=[jax.ShapeDtypeStruct((bsz, t, RW_COLS), F32),
                   jax.ShapeDtypeStruct((bsz, t, MLA_LOW_COLS), F32),
                   jax.ShapeDtypeStruct((bsz, t, 2 * d), F32)],
        compiler_params=_cparams("parallel", "arbitrary"),
        name="inproj",
    )(x3, mod.arr, mod.arr, wts["w_rw"], wts["w_mla"], wts["w_gate"])


RW_OUT_NAMES = ("r", "kmod", "v", "kkn", "bvec", "logw", "bonus", "g")


def _rw_elementwise(p, pp, mu, w_lora, w_gate_up, w_decay0, a0, k_k, k_a, r_k, ones_blk):
    w = RW_WIDTH
    pm = p + (pp - p) * mu
    r, k, v = pm[:, 0:w], pm[:, w:2 * w], pm[:, 2 * w:3 * w]
    xwa = pm[:, 3 * w:3 * w + DECAY_LORA + ICLR_LORA]
    xg = pm[:, 3 * w + DECAY_LORA + ICLR_LORA:]
    lane = _iota((1, DECAY_LORA + ICLR_LORA), 1)
    z = jnp.where(lane < DECAY_LORA, jnp.tanh(xwa), xwa)
    lora = _mm(z, w_lora)
    y = -(w_decay0 + lora[:, :w])
    softplus = jnp.maximum(y, 0.0) + jnp.log1p(jnp.exp(-jnp.abs(y)))
    w_log = -softplus - 0.5
    logw = -jnp.exp(w_log)
    a = jax.nn.sigmoid(a0 + lora[:, w:])
    g = _mm(jax.nn.sigmoid(xg), w_gate_up)
    kk = k * k_k
    kk = kk * lax.rsqrt(_mm_exact_rhs(kk * kk, ones_blk) + 1e-12)
    kmod = k * (1.0 + (a - 1.0) * k_a)
    bonus = _mm_exact_rhs(r * kmod * r_k, ones_blk) * v
    return r, kmod, v, -kk, kk * a, logw, bonus, g


def _rwprep_sample_kernel(p_ref, pp_ref, mu_ref, wl_ref, wgu_ref, wd0_ref, a0_ref, kk_ref, ka_ref, rk_ref,
                          ones_ref, *out_refs):
    outs = _rw_elementwise(p_ref[...], pp_ref[...], mu_ref[...], wl_ref[...], wgu_ref[...], wd0_ref[...], a0_ref[...],
                           kk_ref[...], ka_ref[...], rk_ref[...], ones_ref[...])
    for o_ref, val in zip(out_refs, outs):
        o_ref[...] = val


def _rw_param_list(wts):
    return [wts["mu_shift"], wts["w_lora"], wts["w_gate_up"], wts["w_decay0"], wts["a0"], wts["k_k"], wts["k_a"],
            wts["r_k"], wts["ones_blk"]]


def _rwprep_sample(p_rw, p_prev, wts):
    n = p_rw.shape[0]
    params = _rw_param_list(wts)
    return pl.pallas_call(
        _rwprep_sample_kernel,
        grid=(1,),
        in_specs=[_const_spec(p_rw), _const_spec(p_prev)] + [_const_spec(a) for a in params],
        out_specs=[pl.BlockSpec((n, RW_WIDTH), lambda i: (0, 0))] * 8,
        out_shape=[jax.ShapeDtypeStruct((n, RW_WIDTH), F32)] * 8,
        compiler_params=_cparams("arbitrary"),
        name="rwprep_sample",
    )(p_rw, p_prev, *params)


def _group_norm_pair(o, ones_pair):
    mu = _mm_exact_rhs(o, ones_pair) * (1.0 / RW_HEAD_DIM)
    d = o - mu
    var = _mm_exact_rhs(d * d, ones_pair) * (1.0 / RW_HEAD_DIM)
    return d * lax.rsqrt(var + GN_EPS)


def _scan_kernel(p_ref, s0_ref, mu_ref, wl_ref, wgu_ref, wd0_ref, a0_ref, kk_ref, ka_ref, rk_ref, ones_ref,
                 lng_ref, lnb_ref, o_ref, st_ref, h_scr, prev_scr):
    c = pl.program_id(1)
    n_steps = pl.num_programs(1)
    n_pairs = h_scr.shape[0]
    cl = SCAN_CHUNK
    n_rows = p_ref.shape[0]
    n_sub = n_rows // cl

    @pl.when(c == 0)
    def _():
        h_scr[...] = jnp.zeros_like(h_scr)
        prev_scr[...] = s0_ref[...]

    p = p_ref[...]
    pp = jnp.where(_iota(p.shape, 0) == 0, prev_scr[...], pltpu.roll(p, 1, 0))
    prev_scr[...] = p[n_rows - 1:n_rows, :]
    r_all, k_all, v_all, a_all, b_all, lw_all, bon_all, g_all = _rw_elementwise(
        p, pp, mu_ref[...], wl_ref[...], wgu_ref[...], wd0_ref[...], a0_ref[...], kk_ref[...], ka_ref[...],
        rk_ref[...], ones_ref[...])

    row = _iota((cl, cl), 0)
    col = _iota((cl, cl), 1)
    tri_incl = row >= col
    tri_strict = row > col
    tri_ones = tri_incl.astype(BF16)
    eye = (row == col).astype(F32)
    head0 = _iota((1, LANES), 1) < RW_HEAD_DIM
    head_mask = (head0, jnp.logical_not(head0))
    head0_2 = (_iota((1, 2 * LANES), 1) & RW_HEAD_DIM) == 0
    r128 = _iota((LANES, LANES), 0)
    c128 = _iota((LANES, LANES), 1)
    same_head = (r128 < RW_HEAD_DIM) == (c128 < RW_HEAD_DIM)
    diag = r128 == c128
    ones_pair = same_head.astype(BF16)
    units = [(s, p) for s in range(n_sub) for p in range(n_pairs)]
    heads = [(s, p, h) for (s, p) in units for h in range(2)]
    rows = {s: slice(cl * s, cl * (s + 1)) for s in range(n_sub)}
    lanes = {p: slice(LANES * p, LANES * (p + 1)) for p in range(n_pairs)}
    at = lambda ref, u: ref[rows[u[0]], lanes[u[1]]]

    lw = {u: at(lw_all, u) for u in units}
    lg = {}
    for u in units:
        l1 = lw[u].astype(BF16)
        rem = lw[u] - l1.astype(F32)
        l2 = rem.astype(BF16)
        l3 = (rem - l2.astype(F32)).astype(BF16)
        lg[u] = _mm(tri_ones, l1) + (_mm(tri_ones, l2) + _mm(tri_ones, l3))
    vv = {u: at(v_all, u) for u in units}
    rh, ah, bh, kh, bt, kt, g_last = {}, {}, {}, {}, {}, {}, {}
    for u in units:
        bb = at(b_all, u)
        kk = at(k_all, u)
        lg_last = lg[u][cl - 1:cl, :]
        rh[u] = at(r_all, u) * jnp.exp(lg[u])
        ah[u] = at(a_all, u) * jnp.exp(lg[u] - lw[u])
        inv = jnp.exp(-lg[u])
        bh[u] = bb * inv
        kh[u] = kk * inv
        tail = jnp.exp(lg_last - lg[u])
        bt[u] = bb * tail
        kt[u] = kk * tail
        g_last[u] = jnp.exp(lg_last)

    xb, xk = {}, {}
    for (s, p, h) in heads:
        ar_h = jnp.where(head_mask[h], jnp.concatenate([ah[s, p], rh[s, p]], axis=0), 0.0)
        xb[s, p, h] = _mm_t(ar_h, bh[s, p])
        xk[s, p, h] = _mm_t(ar_h, kh[s, p])
    l_ab = {k: jnp.where(tri_strict, x[:cl], 0.0) for k, x in xb.items()}
    m_rb = {k: jnp.where(tri_incl, x[cl:], 0.0) for k, x in xb.items()}
    l_ak = {k: jnp.where(tri_strict, x[:cl], 0.0) for k, x in xk.items()}
    m_rk = {k: jnp.where(tri_incl, x[cl:], 0.0) for k, x in xk.items()}
    lv_h = {k: _mm(l_ak[k], vv[k[:2]]) for k in heads}
    mrkv_h = {k: _mm(m_rk[k], vv[k[:2]]) for k in heads}
    ktv = {u: _mm_tl(kt[u], vv[u]) for u in units}

    base_bits = 3
    blk = lambda bits: (row >> bits) == (col >> bits)
    sum_half = _iota((1, 2 * cl), 1) >= cl
    x = {k: jnp.concatenate([jnp.where(blk(base_bits), l_ab[k], 0.0), eye], axis=1) for k in heads}
    for _ in range(base_bits):
        x = {k: _mm(x[k][:, :cl], x[k]) + jnp.where(sum_half, x[k], 0.0) for k in heads}
    t_inv = {k: x[k][:, cl:] for k in heads}
    for bits in range(base_bits, int(math.log2(cl))):
        lower_left = jnp.logical_and(blk(bits + 1), jnp.logical_not(blk(bits)))
        y = {k: _mm(jnp.where(lower_left, l_ab[k], 0.0), t_inv[k]) for k in heads}
        t_inv = {k: t_inv[k] + _mm(t_inv[k], y[k]) for k in heads}

    pick = lambda d, u: jnp.where(head0, d[u + (0,)], d[u + (1,)])
    pick2 = lambda d, u: jnp.where(head0_2, d[u + (0,)], d[u + (1,)])
    z = {u: jnp.concatenate([ah[u], pick(lv_h, u)], axis=1) for u in units}
    tz = {k: _mm(t_inv[k], z[k[:2]]) for k in heads}
    w12 = {u: pick2(tz, u) for u in units}
    q12_h = {k: _mm(m_rb[k], w12[k[:2]]) for k in heads}
    g12 = {u: _mm_tl(bt[u], w12[u]) for u in units}
    q1, q2, g1, g2 = {}, {}, {}, {}
    for u in units:
        q12 = pick2(q12_h, u)
        q1[u] = rh[u] + q12[:, :LANES]
        q2[u] = q12[:, LANES:] + pick(mrkv_h, u)
        g1[u] = jnp.where(same_head, g12[u][:, :LANES], 0.0) + jnp.where(diag, g_last[u], 0.0)
        g2[u] = jnp.where(same_head, g12[u][:, LANES:] + ktv[u], 0.0)

    state = {p: h_scr[p] for p in range(n_pairs)}
    outs = {}
    for s in range(n_sub):
        for p in range(n_pairs):
            outs[s, p] = _mm(q1[s, p], state[p]) + q2[s, p]
        state = {p: _mm(g1[s, p], state[p]) + g2[s, p] for p in range(n_pairs)}
    for p in range(n_pairs):
        h_scr[p] = state[p]

    for u in units:
        y = _group_norm_pair(outs[u], ones_pair) * lng_ref[:, lanes[u[1]]] + lnb_ref[:, lanes[u[1]]]
        o_ref[rows[u[0]], lanes[u[1]]] = (y + at(bon_all, u)) * at(g_all, u)

    @pl.when(c == n_steps - 1)
    def _():
        st_ref[...] = h_scr[...]


def _rw_scan(p_rw, shift0, wts):
    bsz, t, _ = p_rw.shape
    w = RW_WIDTH
    n_pairs = w // LANES
    step_rows = SCAN_CHUNK * SCAN_SUBCHUNKS
    assert t % step_rows == 0, (t, step_rows)
    blk = lambda width: pl.BlockSpec((None, step_rows, width), lambda b, c: (b, c, 0))
    params = _rw_param_list(wts) + [wts["lnx_g"], wts["lnx_b"]]
    return pl.pallas_call(
        _scan_kernel,
        grid=(bsz, t // step_rows),
        in_specs=[blk(RW_COLS), pl.BlockSpec((None, 1, RW_COLS), lambda b, c: (b, 0, 0))]
        + [_const_spec(a) for a in params],
        out_specs=[blk(w), pl.BlockSpec((None, n_pairs, LANES, LANES), lambda b, c: (b, 0, 0, 0))],
        out_shape=[jax.ShapeDtypeStruct((bsz, t, w), F32),
                   jax.ShapeDtypeStruct((bsz, n_pairs, LANES, LANES), F32)],
        scratch_shapes=[pltpu.VMEM((n_pairs, LANES, LANES), F32), pltpu.VMEM((1, RW_COLS), F32)],
        compiler_params=_cparams("parallel", "arbitrary"),
        name="rw_scan",
    )(p_rw, shift0, *params)


def _unpack_state(st):
    hd = RW_HEAD_DIM
    blocks = [st[:, p, j * hd:(j + 1) * hd, j * hd:(j + 1) * hd] for p in range(st.shape[1]) for j in range(2)]
    return jnp.swapaxes(jnp.stack(blocks, axis=1), -1, -2)


def _rwstep_kernel(s_ref, r_ref, k_ref, v_ref, a_ref, b_ref, lw_ref, bon_ref, g_ref, lng_ref, lnb_ref, so_ref, o_ref,
                   acc_scr, vt_scr):
    hd = RW_HEAD_DIM
    vt_scr[...] = v_ref[...].T
    decay2 = jnp.exp(lw_ref[...]).T
    kkn2 = a_ref[...].T
    kka2 = b_ref[...].T
    kmod2 = k_ref[...].T
    r2 = r_ref[...].T
    normed = []
    for h in range(2):
        ch = slice(hd * h, hd * (h + 1))
        decay, kkn, kka, kmod, r = decay2[ch], kkn2[ch], kka2[ch], kmod2[ch], r2[ch]

        def body(i, carry):
            s = s_ref[h, i]
            sa = jnp.sum(s * kkn, axis=0, keepdims=True)
            s_new = s * decay + sa * kka + vt_scr[pl.ds(hd * h + i, 1), :] * kmod
            so_ref[h, i] = s_new
            acc_scr[pl.ds(hd * h + i, 1), :] = jnp.sum(s_new * r, axis=0, keepdims=True)
            return carry

        lax.fori_loop(0, hd, body, 0, unroll=8)
        o = acc_scr[ch, :]
        mu = jnp.mean(o, axis=0, keepdims=True)
        d = o - mu
        var = jnp.mean(d * d, axis=0, keepdims=True)
        normed.append(d * lax.rsqrt(var + GN_EPS))
    y = jnp.concatenate(normed, axis=0).T * lng_ref[...] + lnb_ref[...]
    o_ref[...] = (y + bon_ref[...]) * g_ref[...]


def _rw_step(state_t, rw, wts):
    n = state_t.shape[-1]
    hd = RW_HEAD_DIM
    vec = pl.BlockSpec((n, 2 * hd), lambda p: (0, p))
    par = pl.BlockSpec((1, 2 * hd), lambda p: (0, p))
    st = pl.BlockSpec((2, hd, hd, n), lambda p: (p, 0, 0, 0))
    ins = [rw[nm] for nm in ("r", "kmod", "v", "kkn", "bvec", "logw", "bonus", "g")]
    return pl.pallas_call(
        _rwstep_kernel,
        grid=(RW_HEADS // 2,),
        in_specs=[st] + [vec] * 8 + [par, par],
        out_specs=[st, vec],
        out_shape=[jax.ShapeDtypeStruct(state_t.shape, F32), jax.ShapeDtypeStruct((n, RW_WIDTH), F32)],
        scratch_shapes=[pltpu.VMEM((2 * hd, n), F32), pltpu.VMEM((2 * hd, n), F32)],
        compiler_params=_cparams("parallel"),
        name="rw_step",
    )(state_t, *ins, wts["lnx_g"], wts["lnx_b"])


def _rms(x, g):
    return x * lax.rsqrt(jnp.mean(x * x, axis=-1, keepdims=True) + RMS_EPS) * g


def _mla_common(low, pos, gq, gkv, wqa, wqb, invf, q_scale):
    cq = _rms(low[:, :Q_LORA], gq)
    ckv = _rms(low[:, Q_LORA:Q_LORA + KV_LORA], gkv)
    kr = low[:, Q_LORA + KV_LORA:Q_LORA + KV_LORA + LANES]
    kr_rot = low[:, Q_LORA + KV_LORA + LANES:]
    ang = pos * invf
    cos = jnp.cos(ang)
    sin = jnp.sin(ang)
    cos8 = jnp.concatenate([cos] * MLA_HEADS, axis=1)
    sin8 = jnp.concatenate([sin] * MLA_HEADS, axis=1)
    q = (_mm(cq, wqa) * cos8 + _mm(cq, wqb) * sin8) * q_scale
    k_rope = kr * pltpu.roll(cos, LANES - QK_NOPE, 1) + kr_rot * pltpu.roll(sin, LANES - QK_NOPE, 1)
    return q, ckv, k_rope


def _inproj_prompt_kernel(x_ref, sh_ref, sc_ref, wrw_ref, wmla_ref, wg_ref, gq_ref, gkv_ref, wqa_ref, wqb_ref, wuk_ref,
                          wuv_ref, invf_ref, prw_ref, gs_ref, q_ref, k_ref, v_ref, ckv_ref, kr_ref):
    i = pl.program_id(1)
    rows = x_ref.shape[0]
    h = (x_ref[...] * (1 + sc_ref[...]) + sh_ref[...]).astype(BF16)
    prw_ref[...] = jnp.dot(h, wrw_ref[...], preferred_element_type=F32)
    gs_ref[...] = jax.nn.sigmoid(jnp.dot(h, wg_ref[...], preferred_element_type=F32))
    low = jnp.dot(h, wmla_ref[...], preferred_element_type=F32)
    pos = (i * rows + _iota((rows, 1), 0)).astype(F32)
    q, ckv, k_rope = _mla_common(low, pos, gq_ref[...], gkv_ref[...], wqa_ref[...], wqb_ref[...], invf_ref[...],
                                 ATTN_SCALE * math.log2(math.e))
    q_ref[...] = q.astype(BF16)
    ckv_ref[...] = ckv
    kr_ref[...] = k_rope[:, :QK_ROPE]
    k_slot = pltpu.roll(k_rope, QK_NOPE, 1)
    k_ref[...] = (_mm(ckv, wuk_ref[...]) + jnp.concatenate([k_slot] * MLA_HEADS, axis=1)).astype(BF16)
    v_ref[...] = _mm_t(wuv_ref[...], ckv).astype(BF16)


def _inproj_prompt(x, mod, wts, tile):
    bsz, t, d = x.shape
    params = [wts["w_rw"], wts["w_mla"], wts["w_gate"], wts["g_qnorm"], wts["g_kvnorm"], wts["w_qa"], wts["w_qb"],
              wts["w_uk_pad"], wts["w_uv_t"], wts["invf"]]
    row = lambda w: pl.BlockSpec((None, tile, w), lambda b, i: (b, i, 0))
    slots = MLA_HEADS * HEAD_SLOT
    return pl.pallas_call(
        _inproj_prompt_kernel,
        grid=(bsz, t // tile),
        in_specs=[row(d), mod.spec(0), mod.spec(1)] + [_const_spec(a) for a in params],
        out_specs=[row(RW_COLS), row(2 * d), row(slots), row(slots),
                   pl.BlockSpec((None, MLA_HEADS * V_HEAD, tile), lambda b, i: (b, 0, i)), row(KV_LORA), row(QK_ROPE)],
        out_shape=[jax.ShapeDtypeStruct((bsz, t, RW_COLS), F32), jax.ShapeDtypeStruct((bsz, t, 2 * d), F32),
                   jax.ShapeDtypeStruct((bsz, t, slots), BF16), jax.ShapeDtypeStruct((bsz, t, slots), BF16),
                   jax.ShapeDtypeStruct((bsz, MLA_HEADS * V_HEAD, t), BF16),
                   jax.ShapeDtypeStruct((bsz, t, KV_LORA), F32), jax.ShapeDtypeStruct((bsz, t, QK_ROPE), F32)],
        compiler_params=_cparams("parallel", "arbitrary"),
        name="inproj_prompt",
    )(x, mod.arr, mod.arr, *params)


def _mlaprep_sample_kernel(low_ref, gq_ref, gkv_ref, wqa_ref, wqb_ref, wukt_ref, invf_ref, pos_ref,
                           qlat_ref, qr_ref, ckv_ref, kr_ref):
    q, ckv, k_rope = _mla_common(low_ref[...], pos_ref[...], gq_ref[...], gkv_ref[...], wqa_ref[...], wqb_ref[...],
                                 invf_ref[...], ATTN_SCALE)
    ckv_ref[...] = ckv
    kr_ref[...] = k_rope[:, :QK_ROPE]
    rope_lanes = _iota((1, HEAD_SLOT), 1) < QK_ROPE
    for h in range(MLA_HEADS):
        slot = q[:, HEAD_SLOT * h:HEAD_SLOT * (h + 1)]
        qlat_ref[:, KV_LORA * h:KV_LORA * (h + 1)] = _mm(slot, wukt_ref[h])
        qr_ref[:, HEAD_SLOT * h:HEAD_SLOT * (h + 1)] = jnp.where(rope_lanes, pltpu.roll(slot, LANES - QK_NOPE, 1), 0.0)


def _mlaprep_sample(low, pos, wts):
    n = low.shape[0]
    params = [wts["g_qnorm"], wts["g_kvnorm"], wts["w_qa"], wts["w_qb"], wts["w_ukt_pad"], wts["invf"], pos]
    full = lambda shape: pl.BlockSpec(shape, lambda i: (0,) * len(shape))
    return pl.pallas_call(
        _mlaprep_sample_kernel,
        grid=(1,),
        in_specs=[_const_spec(low)] + [_const_spec(a) for a in params],
        out_specs=[full((n, MLA_HEADS * KV_LORA)), full((n, MLA_HEADS * HEAD_SLOT)), full((n, KV_LORA)),
                   full((n, QK_ROPE))],
        out_shape=[jax.ShapeDtypeStruct((n, MLA_HEADS * KV_LORA), F32),
                   jax.ShapeDtypeStruct((n, MLA_HEADS * HEAD_SLOT), F32),
                   jax.ShapeDtypeStruct((n, KV_LORA), F32), jax.ShapeDtypeStruct((n, QK_ROPE), F32)],
        compiler_params=_cparams("arbitrary"),
        name="mlaprep_sample",
    )(low, *params)


def _attn_kernel(qi_ref, ki_ref, q_ref, k_ref, vt_ref, o_ref, m_scr, l_scr, acc_scr):
    step = pl.program_id(2)
    qi = qi_ref[step]
    ki = ki_ref[step]
    bq = q_ref.shape[0]
    bk = k_ref.shape[0]
    on_diagonal = ki == (qi * bq) // bk

    @pl.when(ki == 0)
    def _():
        m_scr[...] = jnp.full(m_scr.shape, -jnp.inf, F32)
        l_scr[...] = jnp.zeros_like(l_scr)
        acc_scr[...] = jnp.zeros_like(acc_scr)

    n_heads = m_scr.shape[0]

    def accumulate(masked):
        scores = [_mm_t(k_ref[:, HEAD_SLOT * h:HEAD_SLOT * (h + 1)], q_ref[:, HEAD_SLOT * h:HEAD_SLOT * (h + 1)])
                  for h in range(n_heads)]
        if masked:
            visible = ki * bk + _iota((bk, bq), 0) <= qi * bq + _iota((bk, bq), 1)
            scores = [jnp.where(visible, s, -jnp.inf) for s in scores]
        for h, s in enumerate(scores):
            m_prev = m_scr[h]
            m_new = jnp.maximum(m_prev, jnp.max(s, axis=0, keepdims=True))
            p = jnp.exp2(s - m_new)
            alpha = jnp.exp2(m_prev - m_new)
            l_scr[h] = alpha * l_scr[h] + jnp.sum(p, axis=0, keepdims=True)
            acc_scr[h] = alpha * acc_scr[h] + _mm(vt_ref[V_HEAD * h:V_HEAD * (h + 1), :], p)
            m_scr[h] = m_new

    @pl.when(jnp.logical_not(on_diagonal))
    def _():
        accumulate(False)

    @pl.when(on_diagonal)
    def _():
        accumulate(True)
        out_t = jnp.concatenate([acc_scr[h] / l_scr[h] for h in range(n_heads)], axis=0)
        o_ref[...] = out_t.T


def _attention_prompt(q, k, v_t, bq, bk):
    bsz, t, _ = q.shape
    assert bk % bq == 0
    hps = ATTN_HEADS_PER_STEP
    pairs = [(qi, ki) for qi in range(t // bq) for ki in range((qi * bq) // bk + 1)]
    qi_of = jnp.asarray([p[0] for p in pairs], jnp.int32)
    ki_of = jnp.asarray([p[1] for p in pairs], jnp.int32)
    grid_spec = pltpu.PrefetchScalarGridSpec(
        num_scalar_prefetch=2,
        grid=(bsz, MLA_HEADS // hps, len(pairs)),
        in_specs=[pl.BlockSpec((None, bq, hps * HEAD_SLOT), lambda b, g, s, qi, ki: (b, qi[s], g)),
                  pl.BlockSpec((None, bk, hps * HEAD_SLOT), lambda b, g, s, qi, ki: (b, ki[s], g)),
                  pl.BlockSpec((None, hps * V_HEAD, bk), lambda b, g, s, qi, ki: (b, g, ki[s]))],
        out_specs=pl.BlockSpec((None, bq, hps * V_HEAD), lambda b, g, s, qi, ki: (b, qi[s], g)),
        scratch_shapes=[pltpu.VMEM((hps, 1, bq), F32), pltpu.VMEM((hps, 1, bq), F32),
                        pltpu.VMEM((hps, V_HEAD, bq), F32)],
    )
    return pl.pallas_call(
        _attn_kernel,
        grid_spec=grid_spec,
        out_shape=jax.ShapeDtypeStruct((bsz, t, MLA_HEADS * V_HEAD), F32),
        compiler_params=_cparams("parallel", "parallel", "arbitrary"),
        name="attn_prompt",
    )(qi_of, ki_of, q, k, v_t)


def _sample_attn_kernel(pt_ref, qlat_ref, qr_ref, cn_ref, kn_ref, ckv_hbm, kr_hbm, o_ref, kbuf, rbuf, s_scr, sem):
    b = pl.program_id(0)
    n_seq = pl.num_programs(0)
    n_pages = kbuf.shape[1]
    slot = lax.rem(b, 2)

    def page_copies(seq, slt):
        out = []
        for pg in range(n_pages):
            page = pt_ref[seq, pg]
            out.append(pltpu.make_async_copy(ckv_hbm.at[page], kbuf.at[slt, pg], sem.at[slt, 0]))
            out.append(pltpu.make_async_copy(kr_hbm.at[page], rbuf.at[slt, pg], sem.at[slt, 1]))
        return out

    @pl.when(b == 0)
    def _():
        for cp in page_copies(0, 0):
            cp.start()

    @pl.when(b + 1 < n_seq)
    def _():
        for cp in page_copies(b + 1, 1 - slot):
            cp.start()

    for cp in page_copies(b, slot):
        cp.wait()

    ql = qlat_ref[...]
    qr = qr_ref[:, :QK_ROPE]
    n_chunks = n_pages // 2
    rows = 2 * PAGE_SIZE

    def score(i, carry):
        kc = kbuf[slot, pl.ds(2 * i, 2)].reshape(rows, KV_LORA)
        rc = jnp.concatenate([rbuf[slot, 2 * i], rbuf[slot, 2 * i + 1]], axis=1)
        s_scr[i] = _mm_t(ql, kc) + jnp.dot(qr, rc, preferred_element_type=F32)
        return carry

    lax.fori_loop(0, n_chunks, score, 0, unroll=SAMPLE_ATTN_UNROLL)
    cn = cn_ref[...]
    kn = kn_ref[...]
    s_new = jnp.sum(ql * cn, axis=-1, keepdims=True) + jnp.sum(qr * kn, axis=-1, keepdims=True)
    s_all = s_scr[...]
    m = jnp.maximum(jnp.max(jnp.max(s_all, axis=0), axis=-1, keepdims=True), s_new)
    p_all = jnp.exp(s_all - m)
    p_new = jnp.exp(s_new - m)
    denom = jnp.sum(jnp.sum(p_all, axis=0), axis=-1, keepdims=True) + p_new
    s_scr[...] = p_all

    def accumulate(i, acc):
        kc = kbuf[slot, pl.ds(2 * i, 2)].reshape(rows, KV_LORA)
        return acc + _mm(s_scr[i], kc)

    acc = lax.fori_loop(0, n_chunks, accumulate, jnp.zeros((ql.shape[0], KV_LORA), F32), unroll=SAMPLE_ATTN_UNROLL)
    o_ref[...] = (acc + p_new * cn) / denom


def _attention_sample(page_table, qlat, qr, ckv_new, kr_new, cache_ckv, cache_kr):
    n, n_pages = page_table.shape
    grid_spec = pltpu.PrefetchScalarGridSpec(
        num_scalar_prefetch=1,
        grid=(n,),
        in_specs=[pl.BlockSpec((None, MLA_HEADS, KV_LORA), lambda b, pt: (b, 0, 0)),
                  pl.BlockSpec((None, MLA_HEADS, HEAD_SLOT), lambda b, pt: (b, 0, 0)),
                  pl.BlockSpec((None, 1, KV_LORA), lambda b, pt: (b, 0, 0)),
                  pl.BlockSpec((None, 1, QK_ROPE), lambda b, pt: (b, 0, 0)),
                  pl.BlockSpec(memory_space=pl.ANY),
                  pl.BlockSpec(memory_space=pl.ANY)],
        out_specs=pl.BlockSpec((None, MLA_HEADS, KV_LORA), lambda b, pt: (b, 0, 0)),
        scratch_shapes=[pltpu.VMEM((2, n_pages, PAGE_SIZE, KV_LORA), F32),
                        pltpu.VMEM((2, n_pages, QK_ROPE, PAGE_SIZE), F32),
                        pltpu.VMEM((n_pages // 2, MLA_HEADS, 2 * PAGE_SIZE), F32),
                        pltpu.SemaphoreType.DMA((2, 2))],
    )
    return pl.pallas_call(
        _sample_attn_kernel,
        grid_spec=grid_spec,
        out_shape=jax.ShapeDtypeStruct((n, MLA_HEADS, KV_LORA), F32),
        compiler_params=_cparams("arbitrary"),
        name="attn_sample",
    )(page_table, qlat.reshape(n, MLA_HEADS, KV_LORA), qr.reshape(n, MLA_HEADS, HEAD_SLOT),
      ckv_new.reshape(n, 1, KV_LORA), kr_new.reshape(n, 1, QK_ROPE), cache_ckv, cache_kr)


def _uv_kernel(ol_ref, wuv_ref, o_ref):
    outs = [_mm(ol_ref[:, KV_LORA * h:KV_LORA * (h + 1)], wuv_ref[:, V_HEAD * h:V_HEAD * (h + 1)])
            for h in range(MLA_HEADS)]
    o_ref[...] = jnp.concatenate(outs, axis=1)


def _latent_to_heads(o_lat, w_uv):
    n = o_lat.shape[0]
    o_lat = o_lat.reshape(n, MLA_HEADS * KV_LORA)
    return pl.pallas_call(
        _uv_kernel,
        grid=(1,),
        in_specs=[_const_spec(o_lat), _const_spec(w_uv)],
        out_specs=pl.BlockSpec((n, MLA_HEADS * V_HEAD), lambda i: (0, 0)),
        out_shape=jax.ShapeDtypeStruct((n, MLA_HEADS * V_HEAD), F32),
        compiler_params=_cparams("arbitrary"),
        name="latent_to_heads",
    )(o_lat, w_uv)


def _layernorm(z, g, b):
    mu = jnp.mean(z, axis=-1, keepdims=True)
    d = z - mu
    var = jnp.mean(d * d, axis=-1, keepdims=True)
    return d * lax.rsqrt(var + LN_EPS) * g + b


def _first_max(x, idx, sentinel):
    mx = jnp.max(x, axis=0, keepdims=True)
    return jnp.min(jnp.where(x == mx, idx, sentinel), axis=0, keepdims=True)


def _route(scores, bias):
    n_tok = scores.shape[1]
    per_group = N_EXPERTS // N_GROUPS
    sb = scores + bias
    sb3 = sb.reshape(N_GROUPS, per_group, n_tok)
    member = _iota(sb3.shape, 1)
    m1 = jnp.max(sb3, axis=1, keepdims=True)
    first = jnp.min(jnp.where(sb3 == m1, member, per_group), axis=1, keepdims=True)
    m2 = jnp.max(jnp.where(member == first, -jnp.inf, sb3), axis=1, keepdims=True)
    g_score = (m1 + m2).reshape(N_GROUPS, n_tok)
    g_idx = _iota(g_score.shape, 0)
    g_sel = jnp.zeros(g_score.shape, F32)
    for _ in range(TOPK_GROUPS):
        hit = g_idx == _first_max(g_score, g_idx, N_GROUPS)
        g_sel = jnp.where(hit, 1.0, g_sel)
        g_score = jnp.where(hit, -jnp.inf, g_score)
    e_mask = jnp.broadcast_to(g_sel.reshape(N_GROUPS, 1, n_tok), sb3.shape).reshape(N_EXPERTS, n_tok) > 0.5
    cur = jnp.where(e_mask, sb, -jnp.inf)
    e_idx = _iota(cur.shape, 0)
    sel = jnp.zeros(cur.shape, F32)
    for _ in range(TOP_K):
        hit = e_idx == _first_max(cur, e_idx, N_EXPERTS)
        sel = jnp.where(hit, 1.0, sel)
        cur = jnp.where(hit, -jnp.inf, cur)
    picked = jnp.where(sel > 0.5, scores, 0.0)
    return picked / jnp.sum(picked, axis=0, keepdims=True) * ROUTED_SCALE


def _merge_kernel(x_ref, oa_ref, ob_ref, gs_ref, gt1_ref, sh2_ref, sc2_ref, wba_ref, wbb_ref, wout_ref, g1_ref, b1_ref,
                  wrt_ref, br_ref, x1_ref, h2_ref, cw_ref):
    d = D_MODEL
    ya = _mm(oa_ref[...], wba_ref[...])
    yb = _mm(ob_ref[...], wbb_ref[...])
    merged = gs_ref[:, :d] * ya + gs_ref[:, d:] * yb
    z = DN_ALPHA * x_ref[...] + gt1_ref[...] * _mm(merged, wout_ref[...])
    x1 = _layernorm(z, g1_ref[...], b1_ref[...])
    x1_ref[...] = x1
    h2 = x1 * (1 + sc2_ref[...]) + sh2_ref[...]
    h2_ref[...] = h2.astype(BF16)
    scores = jax.nn.sigmoid(_mm3_t(wrt_ref[...], h2))
    cw_ref[...] = _route(scores, br_ref[...]).T


def _merge(x3, oa, ob, gs, mod, wts, tile):
    bsz, t, d = x3.shape
    params = [wts["w_branch_a"], wts["w_branch_b"], wts["w_out"], wts["ln1_g"], wts["ln1_b"], wts["w_router_t"],
              wts["b_router"]]
    row = lambda w: pl.BlockSpec((None, tile, w), lambda b, i: (b, i, 0))
    return pl.pallas_call(
        _merge_kernel,
        grid=(bsz, t // tile),
        in_specs=[row(d), row(RW_WIDTH), row(MLA_HEADS * V_HEAD), row(2 * d), mod.spec(2), mod.spec(3), mod.spec(4)]
        + [_const_spec(a) for a in params],
        out_specs=[row(d), row(d), row(N_EXPERTS)],
        out_shape=[jax.ShapeDtypeStruct((bsz, t, d), F32), jax.ShapeDtypeStruct((bsz, t, d), BF16),
                   jax.ShapeDtypeStruct((bsz, t, N_EXPERTS), F32)],
        compiler_params=_cparams("parallel", "arbitrary"),
        name="merge",
    )(x3, oa, ob, gs, mod.arr, mod.arr, mod.arr, *params)


def _experts_kernel(h_ref, cw_ref, wg_ref, wu_ref, wd_ref, o_ref):
    step = pl.program_id(1)
    n_e = wg_ref.shape[0]

    @pl.when(step == 0)
    def _():
        o_ref[...] = jnp.zeros_like(o_ref)

    x = h_ref[...]
    cw = cw_ref[...]
    lane = _iota((1, N_EXPERTS), 1)
    acts = []
    for e in range(n_e):
        act = _silu(_mm(x, wg_ref[e])) * _mm(x, wu_ref[e])
        w_col = jnp.sum(jnp.where(lane == step * n_e + e, cw, 0.0), axis=1, keepdims=True)
        acts.append((act * w_col).astype(BF16))
    o_ref[...] += _mm(jnp.concatenate(acts, axis=1), wd_ref[...].reshape(n_e * EXPERT_FF, -1))


def _experts(h2, cw, wts, tile):
    n, d = h2.shape
    n_e = EXPERTS_PER_STEP
    return pl.pallas_call(
        _experts_kernel,
        grid=(n // tile, N_EXPERTS // n_e),
        in_specs=[pl.BlockSpec((tile, d), lambda i, e: (i, 0)),
                  pl.BlockSpec((tile, N_EXPERTS), lambda i, e: (i, 0)),
                  pl.BlockSpec((n_e, d, EXPERT_FF), lambda i, e: (e, 0, 0)),
                  pl.BlockSpec((n_e, d, EXPERT_FF), lambda i, e: (e, 0, 0)),
                  pl.BlockSpec((n_e, EXPERT_FF, d), lambda i, e: (e, 0, 0))],
        out_specs=pl.BlockSpec((tile, d), lambda i, e: (i, 0)),
        out_shape=jax.ShapeDtypeStruct((n, d), F32),
        compiler_params=_cparams("parallel", "arbitrary"),
        name="experts",
    )(h2, cw, wts["w_exp_gate"], wts["w_exp_up"], wts["w_exp_down"])


def _final_kernel(x1_ref, h2_ref, routed_ref, gt2_ref, wsg_ref, wsu_ref, wsd_ref, g2_ref, b2_ref, y_ref):
    h2 = h2_ref[...]
    act = _silu(jnp.dot(h2, wsg_ref[...], preferred_element_type=F32)) * jnp.dot(h2, wsu_ref[...],
                                                                               preferred_element_type=F32)
    ffn = routed_ref[...] + _mm(act, wsd_ref[...])
    y_ref[...] = _layernorm(DN_ALPHA * x1_ref[...] + gt2_ref[...] * ffn, g2_ref[...], b2_ref[...])


def _final(x1, h2, routed, mod, wts, tile):
    bsz, t, d = x1.shape
    params = [wts["w_sh_gate"], wts["w_sh_up"], wts["w_sh_down"], wts["ln2_g"], wts["ln2_b"]]
    row = lambda w: pl.BlockSpec((None, tile, w), lambda b, i: (b, i, 0))
    return pl.pallas_call(
        _final_kernel,
        grid=(bsz, t // tile),
        in_specs=[row(d), row(d), row(d), mod.spec(5)] + [_const_spec(a) for a in params],
        out_specs=row(d),
        out_shape=jax.ShapeDtypeStruct((bsz, t, d), F32),
        compiler_params=_cparams("parallel", "arbitrary"),
        name="final",
    )(x1, h2, routed, mod.arr, *params)


def _rope_rotation_columns(w):
    half = QK_ROPE // 2
    return jnp.concatenate([-w[..., half:], w[..., :half]], axis=-1)


def _prepare_weights(P):
    d = D_MODEL
    w = {}
    w_in = P["w_in"]
    o1 = RW_COLS
    o2 = o1 + Q_LORA
    o3 = o2 + KV_LORA
    o4 = o3 + QK_ROPE
    w_kr = w_in[:, o3:o4]
    pad = jnp.zeros((d, LANES - QK_ROPE), F32)
    w["w_rw"] = w_in[:, :o1].astype(BF16)
    w["w_mla"] = jnp.concatenate([w_in[:, o1:o3], w_kr, pad, _rope_rotation_columns(w_kr), pad], axis=1).astype(BF16)
    w["w_gate"] = w_in[:, o4:].astype(BF16)

    row = lambda v: v.reshape(1, -1)
    w["mu_shift"] = row(P["mu_shift"])
    zl = jnp.zeros((DECAY_LORA, RW_WIDTH), F32)
    w["w_lora"] = jnp.concatenate([jnp.concatenate([P["w_decay_up"], zl], axis=1),
                                   jnp.concatenate([zl, P["w_iclr_up"]], axis=1)], axis=0).astype(BF16)
    w["w_gate_up"] = P["w_gate_up"].astype(BF16)
    for nm in ("w_decay0", "a0", "k_k", "k_a", "r_k", "lnx_g", "lnx_b", "g_qnorm", "g_kvnorm", "ln1_g", "ln1_b",
               "ln2_g", "ln2_b"):
        w[nm] = row(P[nm])
    head_of = np.arange(RW_WIDTH) // RW_HEAD_DIM
    w["ones_blk"] = jnp.asarray(head_of[:, None] == head_of[None, :], BF16)

    wq = P["w_uq"].reshape(Q_LORA, MLA_HEADS, QK_NOPE + QK_ROPE)
    zq = jnp.zeros((Q_LORA, MLA_HEADS, HEAD_SLOT - QK_NOPE - QK_ROPE), F32)
    w["w_qa"] = jnp.concatenate([wq, zq], axis=-1).reshape(Q_LORA, MLA_HEADS * HEAD_SLOT).astype(BF16)
    w["w_qb"] = jnp.concatenate([jnp.zeros((Q_LORA, MLA_HEADS, QK_NOPE), F32), _rope_rotation_columns(wq[..., QK_NOPE:]), zq],
                                axis=-1).reshape(Q_LORA, MLA_HEADS * HEAD_SLOT).astype(BF16)
    wuk = P["w_uk"].reshape(KV_LORA, MLA_HEADS, QK_NOPE)
    zk = jnp.zeros((KV_LORA, MLA_HEADS, HEAD_SLOT - QK_NOPE), F32)
    w["w_uk_pad"] = jnp.concatenate([wuk, zk], axis=-1).reshape(KV_LORA, MLA_HEADS * HEAD_SLOT).astype(BF16)
    w["w_ukt_pad"] = jnp.transpose(jnp.concatenate([wuk, zk], axis=-1), (1, 2, 0)).astype(BF16)
    w["w_uv"] = P["w_uv"].astype(BF16)
    w["w_uv_t"] = P["w_uv"].T.astype(BF16)
    half = QK_ROPE // 2
    inv = ROPE_THETA ** (-jnp.arange(half, dtype=F32) / half)
    w["invf"] = jnp.concatenate([jnp.zeros((QK_NOPE,), F32), inv, inv,
                                 jnp.zeros((HEAD_SLOT - QK_NOPE - QK_ROPE,), F32)]).reshape(1, HEAD_SLOT)

    for nm in ("w_branch_a", "w_branch_b", "w_out", "w_sh_gate", "w_sh_up", "w_sh_down"):
        w[nm] = P[nm].astype(BF16)
    for nm in ("w_exp_gate", "w_exp_up", "w_exp_down"):
        w[nm] = P[nm]
    w["w_router_t"] = P["w_router"].T
    w["b_router"] = P["b_router"].reshape(N_EXPERTS, 1)
    return w


def _row_tile(t, cap):
    tile = min(t, cap)
    assert t % tile == 0 and tile % 16 == 0, (t, tile)
    return tile


def _layer_tail(x3, oa, ob, gs, mod, wts, tile, moe_tile):
    bsz, t, d = x3.shape
    x1, h2, cw = _merge(x3, oa, ob, gs, mod, wts, tile)
    routed = _experts(h2.reshape(bsz * t, d), cw.reshape(bsz * t, N_EXPERTS), wts, moe_tile).reshape(bsz, t, d)
    return _final(x1, h2, routed, mod, wts, tile)


def _layer_prompt(x, mod_rows, wts):
    bsz, t, d = x.shape
    tile = _row_tile(t, 256)
    mod = _Mod(mod_rows.reshape(bsz * 6, 1, d), per_token=False, tile=tile)
    p_rw, gs, q, k, v_t, ckv, k_rope = _inproj_prompt(x, mod, wts, tile)
    oa, state = _rw_scan(p_rw, jnp.zeros((bsz, 1, RW_COLS), F32), wts)
    ob = _attention_prompt(q, k, v_t, _row_tile(t, 512), _row_tile(t, 1024))
    y = _layer_tail(x, oa, ob, gs, mod, wts, _row_tile(t, 1024), _row_tile(bsz * t, 1024))
    return y, ckv, k_rope, _unpack_state(state), p_rw[:, -1]


def _layer_sample(x, mod_rows, state, shift, page_table, cache_ckv, cache_kr, wts):
    n, s_new, d = x.shape
    assert s_new == 1
    past = page_table.shape[1] * PAGE_SIZE
    x3 = x.reshape(1, n, d)
    mod = _Mod(mod_rows, per_token=True, tile=n)
    p_rw, low, gs = _inproj(x3, mod, wts, n)
    p_rw2 = p_rw.reshape(n, RW_COLS)
    rw = dict(zip(RW_OUT_NAMES, _rwprep_sample(p_rw2, shift, wts)))
    new_state_t, oa = _rw_step(jnp.transpose(state, (1, 2, 3, 0)), rw, wts)
    new_state = jnp.transpose(new_state_t, (3, 0, 1, 2))
    pos = jnp.full((n, 1), past, F32)
    qlat, qr, ckv, k_rope = _mlaprep_sample(low.reshape(n, MLA_LOW_COLS), pos, wts)
    o_lat = _attention_sample(page_table, qlat, qr, ckv, k_rope, cache_ckv, jnp.swapaxes(cache_kr, 1, 2))
    ob = _latent_to_heads(o_lat, wts["w_uv"])
    y = _layer_tail(x3, oa.reshape(1, n, RW_WIDTH), ob.reshape(1, n, MLA_HEADS * V_HEAD), gs, mod, wts, n, n)
    return (y.reshape(n, 1, d), ckv.reshape(n, 1, KV_LORA), k_rope.reshape(n, 1, QK_ROPE), new_state, p_rw2)


def kernel(x_prompt, x_sample, c_prompt, c_sample, cache_ckv, cache_krope, state_wkv, state_shift, page_table, w_ada, b_ada, w_in, mu_shift, w_decay0, w_decay_up, a0, w_iclr_up, w_gate_up, k_k, k_a, r_k, lnx_g, lnx_b, w_branch_a, g_qnorm, w_uq, g_kvnorm, w_uk, w_uv, w_branch_b, w_out, ln1_g, ln1_b, w_router, b_router, w_exp_gate, w_exp_up, w_exp_down, w_sh_gate, w_sh_up, w_sh_down, ln2_g, ln2_b):
    params = dict(w_ada=w_ada, b_ada=b_ada, w_in=w_in, mu_shift=mu_shift, w_decay0=w_decay0, w_decay_up=w_decay_up,
                  a0=a0, w_iclr_up=w_iclr_up, w_gate_up=w_gate_up, k_k=k_k, k_a=k_a, r_k=r_k, lnx_g=lnx_g,
                  lnx_b=lnx_b, w_branch_a=w_branch_a, g_qnorm=g_qnorm, w_uq=w_uq, g_kvnorm=g_kvnorm, w_uk=w_uk,
                  w_uv=w_uv, w_branch_b=w_branch_b, w_out=w_out, ln1_g=ln1_g, ln1_b=ln1_b, w_router=w_router,
                  b_router=b_router, w_exp_gate=w_exp_gate, w_exp_up=w_exp_up, w_exp_down=w_exp_down,
                  w_sh_gate=w_sh_gate, w_sh_up=w_sh_up, w_sh_down=w_sh_down, ln2_g=ln2_g, ln2_b=ln2_b)
    depth = w_in.shape[0]
    bp = x_prompt.shape[0]
    bd = x_sample.shape[0]
    n_c = bp + bd
    c_all = jnp.concatenate([c_prompt, c_sample, jnp.zeros((-n_c % 8, D_MODEL), F32)], axis=0)
    yp, ys = x_prompt, x_sample
    outs = [[] for _ in range(8)]
    for l in range(depth):
        wts = _prepare_weights({name: arr[l] for name, arr in params.items()})
        mod = _adaln_mod(c_all, params["w_ada"][l], params["b_ada"][l].reshape(1, -1))
        yp, *rest_p = _layer_prompt(yp, mod[:bp], wts)
        ys, *rest_s = _layer_sample(ys, mod[bp:n_c], state_wkv[l], state_shift[l], page_table, cache_ckv[l],
                                    cache_krope[l], wts)
        for acc, val in zip(outs, rest_p + rest_s):
            acc.append(val)
    return (yp, ys) + tuple(jnp.stack(o) for o in outs)
```
